```python
import numpy as np
import jax
import jax.numpy as jnp
from jax import lax

D_MODEL = 1024
BATCH = 8
SEQ = 2048
DEPTH = 1

GRID_W = 64
CTX_LEN = 256

NA_HEADS = 16
NA_HEAD_DIM = 64
NA_WIDTH = NA_HEADS * NA_HEAD_DIM
WIN_ROWS = 8
WIN_COLS = 16
COL_BLOCK = 16
KEY_COLS = COL_BLOCK + WIN_COLS
N_COL_BLOCKS = GRID_W // COL_BLOCK

SSM_WIDTH = 2 * D_MODEL
SSM_HEAD_DIM = 64
SSM_HEADS = SSM_WIDTH // SSM_HEAD_DIM
SSM_GROUPS = 8
HEADS_PER_GROUP = SSM_HEADS // SSM_GROUPS
SSM_STATE = 128
SSM_CONV = 5
SSM_CHUNK = 128
N_DIRS = 2
CONV_WIDTH = SSM_WIDTH + 2 * SSM_GROUPS * SSM_STATE

PROJ_SPLITS = (NA_WIDTH, NA_WIDTH, NA_WIDTH, NA_WIDTH, SSM_WIDTH, CONV_WIDTH, N_DIRS * SSM_HEADS, D_MODEL, D_MODEL)
PROJ_WIDTH = sum(PROJ_SPLITS)
EPS = 1e-6

kernel_name = 'hybrid_natten_ssd_prefix_block'


def rmsnorm(x, g):
    xf = x.astype(jnp.float32)
    y = xf * lax.rsqrt(jnp.mean(xf * xf, axis=-1, keepdims=True) + EPS)
    return (y * g.astype(jnp.float32)).astype(x.dtype)


def adaln(cond, w_mod, b_mod):
    m = jax.nn.silu(cond) @ w_mod + b_mod
    return jnp.split(m, 3, axis=-1)


def split_projection(p):
    offsets = np.cumsum(PROJ_SPLITS)[:-1].tolist()
    return jnp.split(p, offsets, axis=-1)


def to_heads(t):
    return t.reshape(*t.shape[:-1], NA_HEADS, NA_HEAD_DIM)


def neighborhood_attention(q, k, v, k_ctx, v_ctx, rpb):
    b, s, h, d = q.shape
    rows = s // GRID_W
    kh = min(WIN_ROWS, rows)
    qg = (q * (d ** -0.5)).reshape(b, rows, N_COL_BLOCKS, COL_BLOCK, h, d)
    kg = k.reshape(b, rows, GRID_W, h, d)
    vg = v.reshape(b, rows, GRID_W, h, d)
    q_cols = np.arange(GRID_W).reshape(N_COL_BLOCKS, COL_BLOCK)
    win_c0 = np.clip(q_cols - WIN_COLS // 2, 0, GRID_W - WIN_COLS)
    key_c0 = np.clip(q_cols[:, 0] - WIN_COLS // 2, 0, GRID_W - KEY_COLS)
    key_cols = key_c0[:, None] + np.arange(KEY_COLS)
    kc = key_cols[:, None, :]
    col_in_win = (kc >= win_c0[..., None]) & (kc < win_c0[..., None] + WIN_COLS)
    col_bias_idx = np.clip(kc - q_cols[..., None] + WIN_COLS - 1, 0, 2 * WIN_COLS - 2)
    n_win = kh * KEY_COLS

    def one_row(args):
        r, q_row = args
        r0 = jnp.clip(r - kh // 2, 0, rows - kh)
        k_blk = lax.dynamic_slice_in_dim(kg, r0, kh, axis=1)[:, :, key_cols]
        v_blk = lax.dynamic_slice_in_dim(vg, r0, kh, axis=1)[:, :, key_cols]
        s_win = jnp.einsum('bjqhd,bajkhd->bhjqak', q_row, k_blk).astype(jnp.float32)
        dr_idx = r0 + jnp.arange(kh) - r + WIN_ROWS - 1
        bias = rpb[:, dr_idx][:, :, col_bias_idx]
        bias = jnp.transpose(bias, (0, 2, 3, 1, 4)).astype(jnp.float32)
        s_win = jnp.where(col_in_win[:, :, None, :], s_win + bias, -jnp.inf)
        s_ctx = jnp.einsum('bjqhd,bchd->bhjqc', q_row, k_ctx).astype(jnp.float32)
        scores = jnp.concatenate([s_win.reshape(b, h, N_COL_BLOCKS, COL_BLOCK, n_win), s_ctx], axis=-1)
        p = jax.nn.softmax(scores, axis=-1).astype(v.dtype)
        p_win = p[..., :n_win].reshape(b, h, N_COL_BLOCKS, COL_BLOCK, kh, KEY_COLS)
        p_ctx = p[..., n_win:]
        o = (jnp.einsum('bhjqak,bajkhd->bjqhd', p_win, v_blk)
             + jnp.einsum('bhjqc,bchd->bjqhd', p_ctx, v_ctx))
        return o.reshape(b, GRID_W, h, d)

    out = lax.map(one_row, (jnp.arange(rows), jnp.moveaxis(qg, 1, 0)))
    return jnp.moveaxis(out, 0, 1).reshape(b, s, h, d)


def context_attention(q, k, v):
    s = jnp.einsum('bqhd,bkhd->bhqk', q * (q.shape[-1] ** -0.5), k).astype(jnp.float32)
    p = jax.nn.softmax(s, axis=-1).astype(v.dtype)
    return jnp.einsum('bhqk,bkhd->bqhd', p, v)


def depthwise_conv(u, w, bias):
    pad = w.shape[0] // 2
    out = lax.conv_general_dilated(u, w[:, None, :].astype(u.dtype), window_strides=(1,),
                                   padding=[(pad, pad)], dimension_numbers=('NWC', 'WIO', 'NWC'),
                                   feature_group_count=u.shape[-1])
    return out + bias


def split_xbc(u):
    b, l, _ = u.shape
    xs, bm, cm = jnp.split(u, [SSM_WIDTH, SSM_WIDTH + SSM_GROUPS * SSM_STATE], axis=-1)
    xs = xs.reshape(b, l, SSM_GROUPS, HEADS_PER_GROUP, SSM_HEAD_DIM)
    bm = bm.reshape(b, l, SSM_GROUPS, SSM_STATE)
    cm = cm.reshape(b, l, SSM_GROUPS, SSM_STATE)
    return xs, bm, cm


def dt_and_decay(dt_raw, a_log, dt_bias):
    dt = jax.nn.softplus(dt_raw.astype(jnp.float32) + dt_bias.astype(jnp.float32))
    a = -jnp.exp(a_log.astype(jnp.float32)) * dt
    shp = dt.shape[:-1] + (SSM_GROUPS, HEADS_PER_GROUP)
    return dt.reshape(shp), a.reshape(shp)


def to_chunks(t):
    return t.reshape(t.shape[0], t.shape[1] // SSM_CHUNK, SSM_CHUNK, *t.shape[2:])


def from_chunks(t):
    return t.reshape(t.shape[0], t.shape[1] * t.shape[2], *t.shape[3:])


def ssd_chunk_states(xdt, a, bm, h0):
    a_cum = jnp.cumsum(a, axis=2)
    a_tot = a_cum[:, :, -1]
    states = jnp.einsum('bcqgn,bcqgr,bcqgrp->bcgrpn', bm, jnp.exp(a_tot[:, :, None] - a_cum), xdt)

    def step(h, inp):
        s_c, a_c = inp
        return jnp.exp(a_c)[..., None, None] * h + s_c, h

    h_final, h_start = lax.scan(step, h0, (jnp.moveaxis(states, 1, 0), jnp.moveaxis(a_tot, 1, 0)))
    return a_cum, jnp.moveaxis(h_start, 0, 1), h_final


def ssd_scan(xdt, a, bm, cm, h0):
    a_cum, h_start, h_final = ssd_chunk_states(xdt, a, bm, h0)
    q = a_cum.shape[2]
    tri = jnp.tril(jnp.ones((q, q), dtype=bool))[:, :, None, None]
    seg = a_cum[:, :, :, None] - a_cum[:, :, None, :]
    decay = jnp.exp(jnp.where(tri, seg, -jnp.inf))
    cb = jnp.einsum('bcign,bcjgn->bcijg', cm, bm)
    y_diag = jnp.einsum('bcijgr,bcjgrp->bcigrp', cb[..., None] * decay, xdt)
    y_off = jnp.einsum('bcign,bcgrpn->bcigrp', cm, h_start) * jnp.exp(a_cum)[..., None]
    return y_diag + y_off, h_final


def ssd_mixer(u_lat, dt_lat, u_ctx, dt_ctx, a_log, dt_bias, d_skip, ctx_out):
    xs_l, bm_l, cm_l = split_xbc(u_lat)
    xs_c, bm_c, cm_c = split_xbc(u_ctx)
    skip = d_skip.reshape(SSM_GROUPS, HEADS_PER_GROUP, 1)
    h_zero = jnp.zeros((u_lat.shape[0], SSM_GROUPS, HEADS_PER_GROUP, SSM_HEAD_DIM, SSM_STATE), jnp.float32)
    y_l = xs_l * skip
    y_c = xs_c * skip if ctx_out else None
    for d in range(N_DIRS):
        orient = (lambda t: jnp.flip(t, axis=1)) if d == 1 else (lambda t: t)
        hs = slice(d * SSM_HEADS, (d + 1) * SSM_HEADS)
        dtc, ac = dt_and_decay(dt_ctx[..., hs], a_log[d], dt_bias[d])
        xc = to_chunks(orient(xs_c * dtc[..., None]))
        ac = to_chunks(orient(ac))
        bc = to_chunks(orient(bm_c))
        if ctx_out:
            yc, h_ctx = ssd_scan(xc, ac, bc, to_chunks(orient(cm_c)), h_zero)
            y_c = y_c + orient(from_chunks(yc))
        else:
            h_ctx = ssd_chunk_states(xc, ac, bc, h_zero)[2]
        dtl, al = dt_and_decay(dt_lat[..., hs], a_log[d], dt_bias[d])
        yl, _ = ssd_scan(to_chunks(orient(xs_l * dtl[..., None])), to_chunks(orient(al)),
                         to_chunks(orient(bm_l)), to_chunks(orient(cm_l)), h_ctx)
        y_l = y_l + orient(from_chunks(yl))
    y_l = y_l.reshape(*u_lat.shape[:2], SSM_WIDTH)
    if ctx_out:
        y_c = y_c.reshape(*u_ctx.shape[:2], SSM_WIDTH)
    return y_l, y_c


def gated_group_rmsnorm(y, z, g):
    b, l, w = y.shape
    u = (y * jax.nn.silu(z)).astype(jnp.float32).reshape(b, l, SSM_GROUPS, w // SSM_GROUPS)
    u = u * lax.rsqrt(jnp.mean(u * u, axis=-1, keepdims=True) + EPS)
    return u.reshape(b, l, w) * g.astype(jnp.float32)


def merge_branches(y_na, y_ssm, g_na, g_ssm, w_na_out, w_ssm_out, w_out):
    m = (jax.nn.sigmoid(g_na) * (y_na @ w_na_out)
         + jax.nn.sigmoid(g_ssm) * (y_ssm.astype(y_na.dtype) @ w_ssm_out))
    return m @ w_out


def hybrid_layer(x, cx, c, c_ctx, w_mod, b_mod, g_pre, g_post, w_in, conv_w, conv_b, a_log, dt_bias,
                 d_skip, ssm_norm_g, rpb, w_na_out, w_ssm_out, w_out, update_ctx):
    b, s, _ = x.shape
    shift_l, scale_l, gate_l = adaln(c, w_mod, b_mod)
    shift_c, scale_c, gate_c = adaln(c_ctx, w_mod, b_mod)
    h_l = rmsnorm(x, g_pre) * (1 + scale_l[:, None]) + shift_l[:, None]
    h_c = rmsnorm(cx, g_pre) * (1 + scale_c) + shift_c
    q_l, k_l, v_l, zna_l, zssm_l, xbc_l, dt_l, gna_l, gssm_l = split_projection(h_l @ w_in)
    q_c, k_c, v_c, zna_c, zssm_c, xbc_c, dt_c, gna_c, gssm_c = split_projection(h_c @ w_in)
    o_na = neighborhood_attention(to_heads(q_l), to_heads(k_l), to_heads(v_l), to_heads(k_c), to_heads(v_c), rpb)
    y_na = o_na.reshape(b, s, NA_WIDTH) * jax.nn.silu(zna_l)
    u_l = jax.nn.silu(depthwise_conv(xbc_l, conv_w, conv_b))
    u_c = jax.nn.silu(depthwise_conv(xbc_c, conv_w, conv_b))
    ys_l, ys_c = ssd_mixer(u_l, dt_l, u_c, dt_c, a_log, dt_bias, d_skip, update_ctx)
    y_ssm = gated_group_rmsnorm(ys_l, zssm_l, ssm_norm_g)
    merged = merge_branches(y_na, y_ssm, gna_l, gssm_l, w_na_out, w_ssm_out, w_out)
    x_new = (x + gate_l[:, None] * rmsnorm(merged, g_post)).astype(x.dtype)
    if not update_ctx:
        return x_new, cx
    o_c = context_attention(to_heads(q_c), to_heads(k_c), to_heads(v_c))
    y_na_c = o_c.reshape(*cx.shape[:2], NA_WIDTH) * jax.nn.silu(zna_c)
    y_ssm_c = gated_group_rmsnorm(ys_c, zssm_c, ssm_norm_g)
    merged_c = merge_branches(y_na_c, y_ssm_c, gna_c, gssm_c, w_na_out, w_ssm_out, w_out)
    cx_new = (cx + gate_c * rmsnorm(merged_c, g_post)).astype(cx.dtype)
    return x_new, cx_new


def setup_inputs(seed: int = 0) -> dict:
    key = jax.random.key(seed)
    ks = jax.random.split(key, 20)
    f32 = jnp.float32

    def nrm(k, shape, scale):
        return jax.random.normal(k, shape, f32) * scale

    dt0 = jnp.exp(jax.random.uniform(ks[12], (DEPTH, N_DIRS, SSM_HEADS), f32,
                                     float(np.log(1e-3)), float(np.log(1e-1))))
    return {
        'x': nrm(ks[0], (BATCH, SEQ, D_MODEL), 1.0),
        'c': nrm(ks[1], (BATCH, D_MODEL), 1.0),
        'ctx': nrm(ks[2], (BATCH, CTX_LEN, D_MODEL), 1.0),
        'c_ctx': nrm(ks[3], (D_MODEL,), 1.0),
        'w_mod': nrm(ks[4], (DEPTH, D_MODEL, 3 * D_MODEL), 0.5 * D_MODEL ** -0.5),
        'b_mod': nrm(ks[5], (DEPTH, 3 * D_MODEL), 0.01),
        'g_pre': 1.0 + nrm(ks[6], (DEPTH, D_MODEL), 0.01),
        'g_post': 1.0 + nrm(ks[7], (DEPTH, D_MODEL), 0.01),
        'w_in': nrm(ks[8], (DEPTH, D_MODEL, PROJ_WIDTH), D_MODEL ** -0.5),
        'conv_w': nrm(ks[9], (DEPTH, SSM_CONV, CONV_WIDTH), SSM_CONV ** -0.5),
        'conv_b': nrm(ks[10], (DEPTH, CONV_WIDTH), 0.01),
        'a_log': jnp.log(jax.random.uniform(ks[11], (DEPTH, N_DIRS, SSM_HEADS), f32, 1.0, 16.0)),
        'dt_bias': dt0 + jnp.log(-jnp.expm1(-dt0)),
        'd_skip': 1.0 + nrm(ks[13], (DEPTH, SSM_HEADS), 0.01),
        'ssm_norm_g': 1.0 + nrm(ks[14], (DEPTH, SSM_WIDTH), 0.01),
        'rpb': nrm(ks[15], (DEPTH, NA_HEADS, 2 * WIN_ROWS - 1, 2 * WIN_COLS - 1), 0.02),
        'w_na_out': nrm(ks[16], (DEPTH, NA_WIDTH, D_MODEL), NA_WIDTH ** -0.5),
        'w_ssm_out': nrm(ks[17], (DEPTH, SSM_WIDTH, D_MODEL), SSM_WIDTH ** -0.5),
        'w_out': nrm(ks[18], (DEPTH, D_MODEL, D_MODEL), D_MODEL ** -0.5),
    }


def reference(x, c, ctx, c_ctx, w_mod, b_mod, g_pre, g_post, w_in, conv_w, conv_b, a_log, dt_bias,
              d_skip, ssm_norm_g, rpb, w_na_out, w_ssm_out, w_out):
    cx = ctx
    for layer in range(DEPTH):
        x, cx = hybrid_layer(x, cx, c, c_ctx, w_mod[layer], b_mod[layer], g_pre[layer], g_post[layer],
                             w_in[layer], conv_w[layer], conv_b[layer], a_log[layer], dt_bias[layer],
                             d_skip[layer], ssm_norm_g[layer], rpb[layer], w_na_out[layer],
                             w_ssm_out[layer], w_out[layer], update_ctx=layer < DEPTH - 1)
    return x
```

```python
import functools

import numpy as np
import jax
import jax.numpy as jnp
from jax import lax
from jax.experimental import pallas as pl
from jax.experimental.pallas import tpu as pltpu

F32 = jnp.float32
BF16 = jnp.bfloat16

D_MODEL = 1024
GRID_W = 64
NA_HEADS = 16
NA_HEAD_DIM = 64
WIN_ROWS = 8
WIN_COLS = 16
SSM_WIDTH = 2 * D_MODEL
SSM_HEAD_DIM = 64
SSM_HEADS = SSM_WIDTH // SSM_HEAD_DIM
SSM_GROUPS = 8
HEADS_PER_GROUP = SSM_HEADS // SSM_GROUPS
SSM_STATE = 128
SSM_CONV = 5
SSM_CHUNK = 128
N_DIRS = 2
GROUP_X = HEADS_PER_GROUP * SSM_HEAD_DIM
EPS = 1e-6
NEG = -1e30

COL_Q, COL_K, COL_V, COL_ZNA = 0, 1024, 2048, 3072
COL_ZSSM, COL_X, COL_B, COL_C = 4096, 6144, 8192, 9216
COL_GNA, COL_GSSM, NP_COLS = 10240, 11264, 12288
CCOL_K, CCOL_V, CCOL_X, CCOL_B, NPC_COLS = 0, 1024, 2048, 4096, 6144

Q_ROWS = 4
Q_BLK = Q_ROWS * GRID_W
KEY_ROWS = 12
KEY_BLK = KEY_ROWS * GRID_W
N_BIAS_CASES = 5
BIAS_CASE_BLOCK = (0, 1, 2, 6, 7)

VMEM_LIMIT = 56 * 1024 * 1024


def _sigmoid(v):
    return 1.0 / (1.0 + jnp.exp(-v))


def _dot(a, b):
    return jnp.dot(a, b, preferred_element_type=F32)


def _dot_nt(a, b):
    return lax.dot_general(a, b, (((1,), (1,)), ((), ())), preferred_element_type=F32)


def _dot_tn(a, b):
    return lax.dot_general(a, b, (((0,), (0,)), ((), ())), preferred_element_type=F32)


def _adaln_kernel(cond_ref, w_ref, b_ref, o_ref):
    cnd = cond_ref[...]
    act = cnd * _sigmoid(cnd)
    o_ref[...] = jnp.dot(act, w_ref[...], preferred_element_type=F32,
                         precision=lax.Precision.HIGHEST) + b_ref[...]


def _adaln(cond, w_mod, b_mod):
    rows, d = cond.shape
    n = w_mod.shape[1]
    tn = 1024
    return pl.pallas_call(
        _adaln_kernel,
        grid=(n // tn,),
        in_specs=[pl.BlockSpec((rows, d), lambda j: (0, 0)),
                  pl.BlockSpec((d, tn), lambda j: (0, j)),
                  pl.BlockSpec((1, tn), lambda j: (0, j))],
        out_specs=pl.BlockSpec((rows, tn), lambda j: (0, j)),
        out_shape=jax.ShapeDtypeStruct((rows, n), F32),
        name="adaln",
    )(cond, w_mod, b_mod.reshape(1, n))


def _proj_kernel(x_ref, shift_ref, scale_ref, g_ref, w_ref, wdt_ref, p_ref, dtt_ref, *, chunks):
    x = x_ref[...]
    ms = jnp.mean(x * x, axis=-1, keepdims=True)
    h = x * lax.rsqrt(ms + EPS) * g_ref[...]
    h = h * (1.0 + scale_ref[...]) + shift_ref[...]
    hb = h.astype(BF16)
    for src, dst, width in chunks:
        p_ref[:, dst:dst + width] = _dot(hb, w_ref[:, src:src + width]).astype(BF16)
    dt = _dot(hb, wdt_ref[...])
    dtt_ref[...] = dt.T[:N_DIRS * SSM_HEADS]


def _projection(x3, shift, scale, g_pre, w_main, w_dt, col_ranges, tm):
    b, l, d = x3.shape
    chunks, dst = [], 0
    cw = 512
    for lo, hi in col_ranges:
        for src in range(lo, hi, cw):
            chunks.append((src, dst, cw))
            dst += cw
    n_out = dst
    n_dt = N_DIRS * SSM_HEADS
    return pl.pallas_call(
        functools.partial(_proj_kernel, chunks=tuple(chunks)),
        grid=(b, l // tm),
        in_specs=[pl.BlockSpec((None, tm, d), lambda i, j: (i, j, 0)),
                  pl.BlockSpec((None, 1, d), lambda i, j: (i, 0, 0)),
                  pl.BlockSpec((None, 1, d), lambda i, j: (i, 0, 0)),
                  pl.BlockSpec((1, d), lambda i, j: (0, 0)),
                  pl.BlockSpec(w_main.shape, lambda i, j: (0, 0), pipeline_mode=pl.Buffered(1)),
                  pl.BlockSpec(w_dt.shape, lambda i, j: (0, 0), pipeline_mode=pl.Buffered(1))],
        out_specs=[pl.BlockSpec((None, tm, n_out), lambda i, j: (i, j, 0)),
                   pl.BlockSpec((None, n_dt, tm), lambda i, j: (i, 0, j))],
        out_shape=[jax.ShapeDtypeStruct((b, l, n_out), BF16),
                   jax.ShapeDtypeStruct((b, n_dt, l), F32)],
        compiler_params=pltpu.CompilerParams(
            dimension_semantics=("parallel", "parallel"), vmem_limit_bytes=VMEM_LIMIT),
        name="projection",
    )(x3, shift, scale, g_pre, w_main, w_dt)


def _bias_block_plan():
    rows = 2048 // GRID_W
    plan = np.full((N_BIAS_CASES, Q_ROWS, KEY_ROWS), -1, np.int64)
    for case, blk in enumerate(BIAS_CASE_BLOCK):
        u0 = int(np.clip(Q_ROWS * blk - WIN_ROWS // 2, 0, rows - KEY_ROWS))
        for rho in range(Q_ROWS):
            r = Q_ROWS * blk + rho
            r0 = int(np.clip(r - WIN_ROWS // 2, 0, rows - WIN_ROWS))
            for a in range(KEY_ROWS):
                krow = u0 + a
                if r0 <= krow < r0 + WIN_ROWS:
                    plan[case, rho, a] = krow - r + WIN_ROWS - 1
    return plan


def _bias_kernel(rpb_ref, o_ref, toep_ref, *, plan):
    head = pl.program_id(0)
    n_dr = 2 * WIN_ROWS - 1
    n_dc = 2 * WIN_COLS - 1
    qc = lax.broadcasted_iota(jnp.int32, (GRID_W, GRID_W), 0)
    kc = lax.broadcasted_iota(jnp.int32, (GRID_W, GRID_W), 1)
    diff = kc - qc + (WIN_COLS - 1)
    c0 = jnp.clip(qc - WIN_COLS // 2, 0, GRID_W - WIN_COLS)
    in_win = (kc >= c0) & (kc < c0 + WIN_COLS)
    base = head * (n_dr * n_dc)
    for dr in range(n_dr):
        blk = jnp.zeros((GRID_W, GRID_W), F32)
        for j in range(n_dc):
            blk = jnp.where(diff == j, rpb_ref[base + dr * n_dc + j], blk)
        toep_ref[dr] = jnp.where(in_win, blk, NEG)
    neg = jnp.full((GRID_W, GRID_W), NEG, F32)
    for case in range(N_BIAS_CASES):
        for rho in range(Q_ROWS):
            for a in range(KEY_ROWS):
                dr = int(plan[case, rho, a])
                val = neg if dr < 0 else toep_ref[dr]
                o_ref[case, rho * GRID_W:(rho + 1) * GRID_W, a * GRID_W:(a + 1) * GRID_W] = val.astype(BF16)


def _bias_table(rpb):
    plan = _bias_block_plan()
    return pl.pallas_call(
        functools.partial(_bias_kernel, plan=plan),
        grid=(NA_HEADS,),
        in_specs=[pl.BlockSpec(memory_space=pltpu.SMEM)],
        out_specs=pl.BlockSpec((N_BIAS_CASES, None, Q_BLK, KEY_BLK), lambda h: (0, h, 0, 0)),
        out_shape=jax.ShapeDtypeStruct((N_BIAS_CASES, NA_HEADS, Q_BLK, KEY_BLK), BF16),
        scratch_shapes=[pltpu.VMEM((2 * WIN_ROWS - 1, GRID_W, GRID_W), F32)],
        name="bias_table",
    )(rpb.reshape(-1))


def _na_kernel(q_ref, k0_ref, k1_ref, k2_ref, kc_ref, v0_ref, v1_ref, v2_ref, vc_ref, z_ref, bias_ref, o_ref):
    lane = lax.broadcasted_iota(jnp.int32, (1, 2 * NA_HEAD_DIM), 1)
    k_refs = (k0_ref, k1_ref, k2_ref, kc_ref)
    v_refs = (v0_ref, v1_ref, v2_ref, vc_ref)
    scale = NA_HEAD_DIM ** -0.5
    for pair in range(NA_HEADS // 2):
        cs = slice(pair * 128, (pair + 1) * 128)
        qp = q_ref[:, cs] * jnp.asarray(scale, BF16)
        kt = [r[:, cs] for r in k_refs]
        vt = [r[:, cs] for r in v_refs]
        acc = jnp.zeros((Q_BLK, 128), F32)
        for hh in range(2):
            head_lanes = (lane < NA_HEAD_DIM) if hh == 0 else (lane >= NA_HEAD_DIM)
            qm = jnp.where(head_lanes, qp, jnp.zeros_like(qp))
            s = [_dot_nt(qm, kk) for kk in kt]
            for t in range(3):
                s[t] = s[t] + bias_ref[2 * pair + hh, :, t * 256:(t + 1) * 256].astype(F32)
            mx = jnp.max(jnp.maximum(jnp.maximum(s[0], s[1]), jnp.maximum(s[2], s[3])), axis=-1, keepdims=True)
            e = [jnp.exp(st - mx) for st in s]
            den = jnp.sum((e[0] + e[1]) + (e[2] + e[3]), axis=-1, keepdims=True)
            o = None
            for t in range(4):
                vm = jnp.where(head_lanes, vt[t], jnp.zeros_like(vt[t]))
                c = _dot(e[t].astype(BF16), vm)
                o = c if o is None else o + c
            acc = acc + o * (1.0 / den)
        z = z_ref[:, cs].astype(F32)
        o_ref[:, cs] = (acc * (z * _sigmoid(z))).astype(BF16)


def _neighborhood_attention(p3, pc3, bias_tab):
    b, s, _ = p3.shape
    n_blk = s // Q_BLK

    def key_blk(i):
        return jnp.clip(i - 1, 0, n_blk - KEY_ROWS // Q_ROWS)

    def case_of(i):
        return jnp.minimum(i, 2) + jnp.maximum(i - (n_blk - 3), 0)

    cq, ck, cv, cz = COL_Q // 1024, COL_K // 1024, COL_V // 1024, COL_ZNA // 1024
    blk = (None, Q_BLK, 1024)
    in_specs = [pl.BlockSpec(blk, lambda i, j: (j, i, cq))]
    in_specs += [pl.BlockSpec(blk, functools.partial(lambda i, j, t: (j, key_blk(i) + t, ck), t=t)) for t in range(3)]
    in_specs += [pl.BlockSpec(blk, lambda i, j: (j, 0, CCOL_K // 1024))]
    in_specs += [pl.BlockSpec(blk, functools.partial(lambda i, j, t: (j, key_blk(i) + t, cv), t=t)) for t in range(3)]
    in_specs += [pl.BlockSpec(blk, lambda i, j: (j, 0, CCOL_V // 1024))]
    in_specs += [pl.BlockSpec(blk, lambda i, j: (j, i, cz))]
    in_specs += [pl.BlockSpec((None, NA_HEADS, Q_BLK, KEY_BLK), lambda i, j: (case_of(i), 0, 0, 0))]
    return pl.pallas_call(
        _na_kernel,
        grid=(n_blk, b),
        in_specs=in_specs,
        out_specs=pl.BlockSpec(blk, lambda i, j: (j, i, 0)),
        out_shape=jax.ShapeDtypeStruct((b, s, NA_HEADS * NA_HEAD_DIM), BF16),
        compiler_params=pltpu.CompilerParams(
            dimension_semantics=("parallel", "parallel"), vmem_limit_bytes=VMEM_LIMIT),
        name="neighborhood_attention",
    )(p3, p3, p3, p3, pc3, p3, p3, p3, pc3, p3, bias_tab)


def _softplus(v):
    return jnp.maximum(v, 0.0) + jnp.log(1.0 + jnp.exp(-jnp.abs(v)))


def _expand_heads(cols, c0):
    lane = lax.broadcasted_iota(jnp.int32, (1, GROUP_X), 1)
    out = cols[:, c0 + 3:c0 + 4]
    for r in (2, 1, 0):
        out = jnp.where(lane < (r + 1) * SSM_HEAD_DIM, cols[:, c0 + r:c0 + r + 1], out)
    return out


def _ssd_kernel(xs_ref, bm_ref, cm_ref, z_ref, xsc_ref, bmc_ref, dtt_ref, dttc_ref,
                cwx_ref, cwb_ref, cwc_ref, cbx_ref, cbb_ref, cbc_ref,
                alog_ref, dtb_ref, skip_ref, ng_ref, o_ref,
                pad_ref, padc_ref, ux_ref, ub_ref, uc_ref, uxc_ref, ubc_ref,
                rows_ref, cols_ref, rowsc_ref, colsc_ref, y_ref):
    q = SSM_CHUNK
    s_len = xs_ref.shape[0]
    c_len = xsc_ref.shape[0]
    n_lat = s_len // q
    n_ctx = c_len // q
    halo = 8

    zeros_halo = jnp.zeros((halo, pad_ref.shape[1]), F32)
    pad_ref[0:halo, :] = zeros_halo
    pad_ref[s_len + halo:s_len + 2 * halo, :] = zeros_halo
    pad_ref[halo:s_len + halo, 0:GROUP_X] = xs_ref[...].astype(F32)
    pad_ref[halo:s_len + halo, GROUP_X:GROUP_X + SSM_STATE] = bm_ref[...].astype(F32)
    pad_ref[halo:s_len + halo, GROUP_X + SSM_STATE:] = cm_ref[...].astype(F32)
    zeros_halo_c = jnp.zeros((halo, padc_ref.shape[1]), F32)
    padc_ref[0:halo, :] = zeros_halo_c
    padc_ref[c_len + halo:c_len + 2 * halo, :] = zeros_halo_c
    padc_ref[halo:c_len + halo, 0:GROUP_X] = xsc_ref[...].astype(F32)
    padc_ref[halo:c_len + halo, GROUP_X:] = bmc_ref[...].astype(F32)

    cw = jnp.concatenate([cwx_ref[...], cwb_ref[...], cwc_ref[...]], axis=1)
    cb = jnp.concatenate([cbx_ref[...], cbb_ref[...], cbc_ref[...]], axis=1)

    def conv_chunk(src_ref, start, width):
        win = src_ref[pl.ds(start, q + 2 * halo), :]
        acc = jnp.broadcast_to(cb[:, :width], (q, width))
        for k in range(SSM_CONV):
            off = halo - SSM_CONV // 2 + k
            acc = acc + cw[k:k + 1, :width] * win[off:off + q, :]
        return acc * _sigmoid(acc)

    def conv_lat(c, carry):
        start = pl.multiple_of(c * q, q)
        u = conv_chunk(pad_ref, start, GROUP_X + 2 * SSM_STATE)
        ux_ref[pl.ds(start, q), :] = u[:, :GROUP_X]
        ub_ref[pl.ds(start, q), :] = u[:, GROUP_X:GROUP_X + SSM_STATE].astype(BF16)
        uc_ref[pl.ds(start, q), :] = u[:, GROUP_X + SSM_STATE:].astype(BF16)
        return carry

    lax.fori_loop(0, n_lat, conv_lat, 0)
    for c in range(n_ctx):
        u = conv_chunk(padc_ref, c * q, GROUP_X + SSM_STATE)
        uxc_ref[c * q:(c + 1) * q, :] = u[:, :GROUP_X]
        ubc_ref[c * q:(c + 1) * q, :] = u[:, GROUP_X:].astype(BF16)

    ki = lax.broadcasted_iota(jnp.int32, (q, q), 0)
    ii = lax.broadcasted_iota(jnp.int32, (q, q), 1)
    tri_f = (ki <= ii).astype(F32)
    tri_b = (ki >= ii).astype(F32)
    row_id = lax.broadcasted_iota(jnp.int32, (2 * HEADS_PER_GROUP, q), 0)
    neg_a = -jnp.exp(alog_ref[...])
    dt_bias = dtb_ref[...]

    def dt_forms(raw):
        dt8 = _softplus(raw + dt_bias)
        a8 = neg_a * dt8
        cum_f = jnp.dot(a8, tri_f, preferred_element_type=F32, precision=lax.Precision.HIGHEST)
        cum_b = jnp.dot(a8, tri_b, preferred_element_type=F32, precision=lax.Precision.HIGHEST)
        acum8 = jnp.where(row_id < HEADS_PER_GROUP, cum_f, cum_b)
        return jnp.concatenate([dt8, acum8], axis=0)

    for c in range(n_lat):
        rf = dt_forms(dtt_ref[:, c * q:(c + 1) * q])
        rows_ref[c] = rf
        cols_ref[c] = rf.T
    for c in range(n_ctx):
        rf = dt_forms(dttc_ref[:, c * q:(c + 1) * q])
        rowsc_ref[c] = rf
        colsc_ref[c] = rf.T

    lane_x = lax.broadcasted_iota(jnp.int32, (1, GROUP_X), 1)
    skip = skip_ref[...]
    norm_g = ng_ref[...]

    def chunk_terms(d, rows, cols, xs, bmat):
        dt_exp = _expand_heads(cols, d * HEADS_PER_GROUP)
        acum_exp = _expand_heads(cols, 2 * HEADS_PER_GROUP + d * HEADS_PER_GROUP)
        edge = q - 1 if d == 0 else 0
        atot_exp = acum_exp[edge:edge + 1, :]
        xdt = xs * dt_exp
        xw = (xdt * jnp.exp(atot_exp - acum_exp)).astype(BF16)
        st = _dot_tn(bmat, xw)
        return xdt.astype(BF16), acum_exp, atot_exp, st

    for d in range(N_DIRS):
        def ctx_step(j, h, d=d):
            c = j if d == 0 else n_ctx - 1 - j
            start = pl.multiple_of(c * q, q)
            _, _, atot_exp, st = chunk_terms(d, rowsc_ref[c], colsc_ref[c],
                                             uxc_ref[pl.ds(start, q), :], ubc_ref[pl.ds(start, q), :])
            return jnp.exp(atot_exp) * h + st

        def lat_step(j, h, d=d):
            c = j if d == 0 else n_lat - 1 - j
            start = pl.multiple_of(c * q, q)
            rows = rows_ref[c]
            cols = cols_ref[c]
            xs = ux_ref[pl.ds(start, q), :]
            bmat = ub_ref[pl.ds(start, q), :]
            cmat = uc_ref[pl.ds(start, q), :]
            xdt_b, acum_exp, atot_exp, st = chunk_terms(d, rows, cols, xs, bmat)
            cbm = _dot_nt(cmat, bmat)
            keep = (ki >= ii) if d == 0 else (ki <= ii)
            y = _dot(cmat, h.astype(BF16)) * jnp.exp(acum_exp)
            for r in range(HEADS_PER_GROUP):
                col = 2 * HEADS_PER_GROUP + d * HEADS_PER_GROUP + r
                seg = cols[:, col:col + 1] - rows[col:col + 1, :]
                dec = jnp.where(keep, jnp.exp(jnp.minimum(seg, 0.0)), 0.0)
                m = (cbm * dec).astype(BF16)
                in_head = (lane_x >= r * SSM_HEAD_DIM) & (lane_x < (r + 1) * SSM_HEAD_DIM)
                y = y + _dot(m, jnp.where(in_head, xdt_b, jnp.zeros_like(xdt_b)))
            if d == 0:
                y_ref[pl.ds(start, q), :] = xs * skip + y
            else:
                tot = y_ref[pl.ds(start, q), :] + y
                z = z_ref[pl.ds(start, q), :].astype(F32)
                u = tot * (z * _sigmoid(z))
                u = u * lax.rsqrt(jnp.mean(u * u, axis=-1, keepdims=True) + EPS)
                o_ref[pl.ds(start, q), :] = (u * norm_g).astype(BF16)
            return jnp.exp(atot_exp) * h + st

        h0 = jnp.zeros((SSM_STATE, GROUP_X), F32)
        h_ctx = lax.fori_loop(0, n_ctx, ctx_step, h0)
        lax.fori_loop(0, n_lat, lat_step, h_ctx)


def _ssd_mixer(p3, pc3, dtt, dttc, conv_w, conv_b, alog8, dtb8, skip_exp, norm_g):
    b, s, _ = p3.shape
    c_len = pc3.shape[1]
    g = SSM_GROUPS
    n_lat, n_ctx = s // SSM_CHUNK, c_len // SSM_CHUNK
    nx = GROUP_X
    ns = SSM_STATE
    in_specs = [
        pl.BlockSpec((None, s, nx), lambda i, j: (i, 0, COL_X // nx + j)),
        pl.BlockSpec((None, s, ns), lambda i, j: (i, 0, COL_B // ns + j)),
        pl.BlockSpec((None, s, ns), lambda i, j: (i, 0, COL_C // ns + j)),
        pl.BlockSpec((None, s, nx), lambda i, j: (i, 0, COL_ZSSM // nx + j)),
        pl.BlockSpec((None, c_len, nx), lambda i, j: (i, 0, CCOL_X // nx + j)),
        pl.BlockSpec((None, c_len, ns), lambda i, j: (i, 0, CCOL_B // ns + j)),
        pl.BlockSpec((None, 2 * HEADS_PER_GROUP, s), lambda i, j: (i, j, 0)),
        pl.BlockSpec((None, 2 * HEADS_PER_GROUP, c_len), lambda i, j: (i, j, 0)),
        pl.BlockSpec((SSM_CONV, nx), lambda i, j: (0, j)),
        pl.BlockSpec((SSM_CONV, ns), lambda i, j: (0, SSM_WIDTH // ns + j)),
        pl.BlockSpec((SSM_CONV, ns), lambda i, j: (0, SSM_WIDTH // ns + g + j)),
        pl.BlockSpec((1, nx), lambda i, j: (0, j)),
        pl.BlockSpec((1, ns), lambda i, j: (0, SSM_WIDTH // ns + j)),
        pl.BlockSpec((1, ns), lambda i, j: (0, SSM_WIDTH // ns + g + j)),
        pl.BlockSpec((None, 2 * HEADS_PER_GROUP, 1), lambda i, j: (j, 0, 0)),
        pl.BlockSpec((None, 2 * HEADS_PER_GROUP, 1), lambda i, j: (j, 0, 0)),
        pl.BlockSpec((None, 1, nx), lambda i, j: (j, 0, 0)),
        pl.BlockSpec((1, nx), lambda i, j: (0, j)),
    ]
    halo = 8
    scratch = [
        pltpu.VMEM((s + 2 * halo, nx + 2 * ns), F32),
        pltpu.VMEM((c_len + 2 * halo, nx + ns), F32),
        pltpu.VMEM((s, nx), F32), pltpu.VMEM((s, ns), BF16), pltpu.VMEM((s, ns), BF16),
        pltpu.VMEM((c_len, nx), F32), pltpu.VMEM((c_len, ns), BF16),
        pltpu.VMEM((n_lat, 16, SSM_CHUNK), F32), pltpu.VMEM((n_lat, SSM_CHUNK, 16), F32),
        pltpu.VMEM((n_ctx, 16, SSM_CHUNK), F32), pltpu.VMEM((n_ctx, SSM_CHUNK, 16), F32),
        pltpu.VMEM((s, nx), F32),
    ]
    return pl.pallas_call(
        _ssd_kernel,
        grid=(b, g),
        in_specs=in_specs,
        out_specs=pl.BlockSpec((None, s, nx), lambda i, j: (i, 0, j)),
        out_shape=jax.ShapeDtypeStruct((b, s, SSM_WIDTH), BF16),
        scratch_shapes=scratch,
        compiler_params=pltpu.CompilerParams(
            dimension_semantics=("parallel", "parallel"), vmem_limit_bytes=VMEM_LIMIT),
        name="ssd_mixer",
    )(p3, p3, p3, p3, pc3, pc3, dtt, dttc, conv_w, conv_w, conv_w, conv_b, conv_b, conv_b,
      alog8, dtb8, skip_exp, norm_g)


def _merge_kernel(yna_ref, yssm_ref, gna_ref, gssm_ref, x_ref, gate_ref, wna_ref, wssm_ref, wout_ref, gpost_ref, o_ref):
    a = _dot(yna_ref[...], wna_ref[...])
    s = _dot(yssm_ref[...], wssm_ref[...])
    m = _sigmoid(gna_ref[...].astype(F32)) * a + _sigmoid(gssm_ref[...].astype(F32)) * s
    o = _dot(m.astype(BF16), wout_ref[...])
    r = o * lax.rsqrt(jnp.mean(o * o, axis=-1, keepdims=True) + EPS) * gpost_ref[...]
    o_ref[...] = x_ref[...] + gate_ref[...] * r


def _merge(y_na, y_ssm, p3, x3, gate, w_na, w_ssm, w_out, g_post, tm):
    b, s, d = x3.shape
    const = lambda i, j: (0, 0)
    return pl.pallas_call(
        _merge_kernel,
        grid=(b, s // tm),
        in_specs=[pl.BlockSpec((None, tm, y_na.shape[-1]), lambda i, j: (i, j, 0)),
                  pl.BlockSpec((None, tm, y_ssm.shape[-1]), lambda i, j: (i, j, 0)),
                  pl.BlockSpec((None, tm, d), lambda i, j: (i, j, COL_GNA // d)),
                  pl.BlockSpec((None, tm, d), lambda i, j: (i, j, COL_GSSM // d)),
                  pl.BlockSpec((None, tm, d), lambda i, j: (i, j, 0)),
                  pl.BlockSpec((None, 1, d), lambda i, j: (i, 0, 0)),
                  pl.BlockSpec(w_na.shape, const),
                  pl.BlockSpec(w_ssm.shape, const),
                  pl.BlockSpec(w_out.shape, const),
                  pl.BlockSpec((1, d), const)],
        out_specs=pl.BlockSpec((None, tm, d), lambda i, j: (i, j, 0)),
        out_shape=jax.ShapeDtypeStruct((b, s, d), x3.dtype),
        compiler_params=pltpu.CompilerParams(
            dimension_semantics=("parallel", "parallel"), vmem_limit_bytes=VMEM_LIMIT),
        name="merge_out",
    )(y_na, y_ssm, p3, p3, x3, gate, w_na, w_ssm, w_out, g_post)


def kernel(x, c, ctx, c_ctx, w_mod, b_mod, g_pre, g_post, w_in, conv_w, conv_b, a_log, dt_bias, d_skip,
           ssm_norm_g, rpb, w_na_out, w_ssm_out, w_out):
    assert w_mod.shape[0] == 1, "single-layer block"
    b, s, d = x.shape
    g, hpg = SSM_GROUPS, HEADS_PER_GROUP

    w_in0 = w_in[0]
    col_dt = COL_GNA
    w_main = jnp.concatenate([w_in0[:, :col_dt], w_in0[:, col_dt + N_DIRS * SSM_HEADS:]], axis=1).astype(BF16)
    dt_perm = np.array([dd * SSM_HEADS + gg * hpg + r for gg in range(g) for dd in range(N_DIRS) for r in range(hpg)])
    w_dt = jnp.pad(w_in0[:, col_dt + dt_perm], ((0, 0), (0, 128 - N_DIRS * SSM_HEADS))).astype(BF16)
    per_group = lambda p: jnp.transpose(p.reshape(N_DIRS, g, hpg), (1, 0, 2)).reshape(g, N_DIRS * hpg, 1)
    alog8 = per_group(a_log[0])
    dtb8 = per_group(dt_bias[0])
    skip_exp = jnp.repeat(d_skip[0], SSM_HEAD_DIM).reshape(g, 1, GROUP_X)

    cond = jnp.zeros((16, d), F32).at[:b].set(c).at[b].set(c_ctx)
    mod = _adaln(cond, w_mod[0], b_mod[0])
    shift_l, scale_l, gate_l = (mod[:b, k * d:(k + 1) * d].reshape(b, 1, d) for k in range(3))
    shift_c, scale_c = (jnp.broadcast_to(mod[b, k * d:(k + 1) * d].reshape(1, 1, d), (b, 1, d)) for k in range(2))

    p3, dtt = _projection(x, shift_l, scale_l, g_pre, w_main, w_dt, ((0, NP_COLS),), tm=256)
    pc3, dttc = _projection(ctx, shift_c, scale_c, g_pre, w_main, w_dt,
                            ((COL_K, COL_ZNA), (COL_X, COL_GNA)), tm=256)

    bias_tab = _bias_table(rpb[0])
    y_na = _neighborhood_attention(p3, pc3, bias_tab)

    y_ssm = _ssd_mixer(p3, pc3, dtt, dttc, conv_w[0], conv_b[0].reshape(1, -1), alog8, dtb8, skip_exp,
                       ssm_norm_g)

    return _merge(y_na, y_ssm, p3, x, gate_l, w_na_out[0].astype(BF16), w_ssm_out[0].astype(BF16),
                  w_out[0].astype(BF16), g_post, tm=512)
```

```python
import functools

import numpy as np
import jax
import jax.numpy as jnp
from jax import lax
from jax.experimental import pallas as pl
from jax.experimental.pallas import tpu as pltpu

F32 = jnp.float32
BF16 = jnp.bfloat16

D_MODEL = 1024
GRID_W = 64
NA_HEADS = 16
NA_HEAD_DIM = 64
WIN_ROWS = 8
WIN_COLS = 16
SSM_WIDTH = 2 * D_MODEL
SSM_HEAD_DIM = 64
SSM_HEADS = SSM_WIDTH // SSM_HEAD_DIM
SSM_GROUPS = 8
HEADS_PER_GROUP = SSM_HEADS // SSM_GROUPS
SSM_STATE = 128
SSM_CONV = 5
SSM_CHUNK = 128
N_DIRS = 2
GROUP_X = HEADS_PER_GROUP * SSM_HEAD_DIM
EPS = 1e-6
NEG = -1e30

COL_Q, COL_K, COL_V, COL_ZNA = 0, 1024, 2048, 3072
COL_ZSSM, COL_X, COL_B, COL_C = 4096, 6144, 8192, 9216
COL_GNA, COL_GSSM, NP_COLS = 10240, 11264, 12288
CCOL_K, CCOL_V, CCOL_X, CCOL_B, NPC_COLS = 0, 1024, 2048, 4096, 6144

Q_ROWS = 4
Q_BLK = Q_ROWS * GRID_W
KEY_ROWS = 12
KEY_BLK = KEY_ROWS * GRID_W
N_BIAS_CASES = 5
BIAS_CASE_BLOCK = (0, 1, 2, 6, 7)

VMEM_LIMIT = 56 * 1024 * 1024


def _sigmoid(v):
    return 1.0 / (1.0 + jnp.exp(-v))


def _dot(a, b):
    return jnp.dot(a, b, preferred_element_type=F32)


def _dot_nt(a, b):
    return lax.dot_general(a, b, (((1,), (1,)), ((), ())), preferred_element_type=F32)


def _adaln_kernel(cond_ref, w_ref, b_ref, o_ref):
    cnd = cond_ref[...]
    act = cnd * _sigmoid(cnd)
    o_ref[...] = jnp.dot(act, w_ref[...], preferred_element_type=F32,
                         precision=lax.Precision.HIGHEST) + b_ref[...]


def _adaln(cond, w_mod, b_mod):
    rows, d = cond.shape
    n = w_mod.shape[1]
    tn = 1024
    return pl.pallas_call(
        _adaln_kernel,
        grid=(n // tn,),
        in_specs=[pl.BlockSpec((rows, d), lambda j: (0, 0)),
                  pl.BlockSpec((d, tn), lambda j: (0, j)),
                  pl.BlockSpec((1, tn), lambda j: (0, j))],
        out_specs=pl.BlockSpec((rows, tn), lambda j: (0, j)),
        out_shape=jax.ShapeDtypeStruct((rows, n), F32),
        name="adaln",
    )(cond, w_mod, b_mod.reshape(1, n))


def _proj_kernel(x_ref, shift_ref, scale_ref, g_ref, w_ref, wdt_ref, p_ref, dtt_ref, *, chunks):
    x = x_ref[...]
    ms = jnp.mean(x * x, axis=-1, keepdims=True)
    h = x * lax.rsqrt(ms + EPS) * g_ref[...]
    h = h * (1.0 + scale_ref[...]) + shift_ref[...]
    hb = h.astype(BF16)
    for src, dst, width in chunks:
        p_ref[:, dst:dst + width] = _dot(hb, w_ref[:, src:src + width]).astype(BF16)
    dt = _dot(hb, wdt_ref[...])
    dtt_ref[...] = dt.T[:N_DIRS * SSM_HEADS]


def _projection(x3, shift, scale, g_pre, w_main, w_dt, col_ranges, tm):
    b, l, d = x3.shape
    chunks, dst = [], 0
    cw = 512
    for lo, hi in col_ranges:
        for src in range(lo, hi, cw):
            chunks.append((src, dst, cw))
            dst += cw
    n_out = dst
    n_dt = N_DIRS * SSM_HEADS
    return pl.pallas_call(
        functools.partial(_proj_kernel, chunks=tuple(chunks)),
        grid=(b, l // tm),
        in_specs=[pl.BlockSpec((None, tm, d), lambda i, j: (i, j, 0)),
                  pl.BlockSpec((None, 1, d), lambda i, j: (i, 0, 0)),
                  pl.BlockSpec((None, 1, d), lambda i, j: (i, 0, 0)),
                  pl.BlockSpec((1, d), lambda i, j: (0, 0)),
                  pl.BlockSpec(w_main.shape, lambda i, j: (0, 0), pipeline_mode=pl.Buffered(1)),
                  pl.BlockSpec(w_dt.shape, lambda i, j: (0, 0), pipeline_mode=pl.Buffered(1))],
        out_specs=[pl.BlockSpec((None, tm, n_out), lambda i, j: (i, j, 0)),
                   pl.BlockSpec((None, n_dt, tm), lambda i, j: (i, 0, j))],
        out_shape=[jax.ShapeDtypeStruct((b, l, n_out), BF16),
                   jax.ShapeDtypeStruct((b, n_dt, l), F32)],
        compiler_params=pltpu.CompilerParams(
            dimension_semantics=("parallel", "parallel"), vmem_limit_bytes=VMEM_LIMIT),
        name="projection",
    )(x3, shift, scale, g_pre, w_main, w_dt)


def _bias_block_plan():
    rows = 2048 // GRID_W
    plan = np.full((N_BIAS_CASES, Q_ROWS, KEY_ROWS), -1, np.int64)
    for case, blk in enumerate(BIAS_CASE_BLOCK):
        u0 = int(np.clip(Q_ROWS * blk - WIN_ROWS // 2, 0, rows - KEY_ROWS))
        for rho in range(Q_ROWS):
            r = Q_ROWS * blk + rho
            r0 = int(np.clip(r - WIN_ROWS // 2, 0, rows - WIN_ROWS))
            for a in range(KEY_ROWS):
                krow = u0 + a
                if r0 <= krow < r0 + WIN_ROWS:
                    plan[case, rho, a] = krow - r + WIN_ROWS - 1
    return plan


def _bias_kernel(rpb_ref, o_ref, toep_ref, *, plan):
    head = pl.program_id(0)
    n_dr = 2 * WIN_ROWS - 1
    n_dc = 2 * WIN_COLS - 1
    qc = lax.broadcasted_iota(jnp.int32, (GRID_W, GRID_W), 0)
    kc = lax.broadcasted_iota(jnp.int32, (GRID_W, GRID_W), 1)
    diff = kc - qc + (WIN_COLS - 1)
    c0 = jnp.clip(qc - WIN_COLS // 2, 0, GRID_W - WIN_COLS)
    in_win = (kc >= c0) & (kc < c0 + WIN_COLS)
    base = head * (n_dr * n_dc)
    for dr in range(n_dr):
        blk = jnp.zeros((GRID_W, GRID_W), F32)
        for j in range(n_dc):
            blk = jnp.where(diff == j, rpb_ref[base + dr * n_dc + j], blk)
        toep_ref[dr] = jnp.where(in_win, blk, NEG)
    neg = jnp.full((GRID_W, GRID_W), NEG, F32)
    for case in range(N_BIAS_CASES):
        for rho in range(Q_ROWS):
            for a in range(KEY_ROWS):
                dr = int(plan[case, rho, a])
                val = neg if dr < 0 else toep_ref[dr]
                o_ref[case, rho * GRID_W:(rho + 1) * GRID_W, a * GRID_W:(a + 1) * GRID_W] = val.astype(BF16)


def _bias_table(rpb):
    plan = _bias_block_plan()
    return pl.pallas_call(
        functools.partial(_bias_kernel, plan=plan),
        grid=(NA_HEADS,),
        in_specs=[pl.BlockSpec(memory_space=pltpu.SMEM)],
        out_specs=pl.BlockSpec((N_BIAS_CASES, None, Q_BLK, KEY_BLK), lambda h: (0, h, 0, 0)),
        out_shape=jax.ShapeDtypeStruct((N_BIAS_CASES, NA_HEADS, Q_BLK, KEY_BLK), BF16),
        scratch_shapes=[pltpu.VMEM((2 * WIN_ROWS - 1, GRID_W, GRID_W), F32)],
        name="bias_table",
    )(rpb.reshape(-1))


def _na_kernel(q_ref, k0_ref, k1_ref, k2_ref, kc_ref, v0_ref, v1_ref, v2_ref, vc_ref, z_ref, bias_ref, o_ref):
    lane = lax.broadcasted_iota(jnp.int32, (1, 2 * NA_HEAD_DIM), 1)
    k_refs = (k0_ref, k1_ref, k2_ref, kc_ref)
    v_refs = (v0_ref, v1_ref, v2_ref, vc_ref)
    scale = NA_HEAD_DIM ** -0.5
    for pair in range(NA_HEADS // 2):
        cs = slice(pair * 128, (pair + 1) * 128)
        qp = q_ref[:, cs] * jnp.asarray(scale, BF16)
        kt = [r[:, cs] for r in k_refs]
        vt = [r[:, cs] for r in v_refs]
        acc = jnp.zeros((Q_BLK, 128), F32)
        for hh in range(2):
            head_lanes = (lane < NA_HEAD_DIM) if hh == 0 else (lane >= NA_HEAD_DIM)
            qm = jnp.where(head_lanes, qp, jnp.zeros_like(qp))
            s = [_dot_nt(qm, kk) for kk in kt]
            for t in range(3):
                s[t] = s[t] + bias_ref[2 * pair + hh, :, t * 256:(t + 1) * 256].astype(F32)
            mx = jnp.max(jnp.maximum(jnp.maximum(s[0], s[1]), jnp.maximum(s[2], s[3])), axis=-1, keepdims=True)
            e = [jnp.exp(st - mx) for st in s]
            den = jnp.sum((e[0] + e[1]) + (e[2] + e[3]), axis=-1, keepdims=True)
            o = None
            for t in range(4):
                vm = jnp.where(head_lanes, vt[t], jnp.zeros_like(vt[t]))
                c = _dot(e[t].astype(BF16), vm)
                o = c if o is None else o + c
            acc = acc + o * (1.0 / den)
        z = z_ref[:, cs].astype(F32)
        o_ref[:, cs] = (acc * (z * _sigmoid(z))).astype(BF16)


def _neighborhood_attention(p3, pc3, bias_tab):
    b, s, _ = p3.shape
    n_blk = s // Q_BLK

    def key_blk(i):
        return jnp.clip(i - 1, 0, n_blk - KEY_ROWS // Q_ROWS)

    def case_of(i):
        return jnp.minimum(i, 2) + jnp.maximum(i - (n_blk - 3), 0)

    cq, ck, cv, cz = COL_Q // 1024, COL_K // 1024, COL_V // 1024, COL_ZNA // 1024
    blk = (None, Q_BLK, 1024)
    in_specs = [pl.BlockSpec(blk, lambda i, j: (j, i, cq))]
    in_specs += [pl.BlockSpec(blk, functools.partial(lambda i, j, t: (j, key_blk(i) + t, ck), t=t)) for t in range(3)]
    in_specs += [pl.BlockSpec(blk, lambda i, j: (j, 0, CCOL_K // 1024))]
    in_specs += [pl.BlockSpec(blk, functools.partial(lambda i, j, t: (j, key_blk(i) + t, cv), t=t)) for t in range(3)]
    in_specs += [pl.BlockSpec(blk, lambda i, j: (j, 0, CCOL_V // 1024))]
    in_specs += [pl.BlockSpec(blk, lambda i, j: (j, i, cz))]
    in_specs += [pl.BlockSpec((None, NA_HEADS, Q_BLK, KEY_BLK), lambda i, j: (case_of(i), 0, 0, 0))]
    return pl.pallas_call(
        _na_kernel,
        grid=(n_blk, b),
        in_specs=in_specs,
        out_specs=pl.BlockSpec(blk, lambda i, j: (j, i, 0)),
        out_shape=jax.ShapeDtypeStruct((b, s, NA_HEADS * NA_HEAD_DIM), BF16),
        compiler_params=pltpu.CompilerParams(
            dimension_semantics=("parallel", "parallel"), vmem_limit_bytes=VMEM_LIMIT),
        name="neighborhood_attention",
    )(p3, p3, p3, p3, pc3, p3, p3, p3, pc3, p3, bias_tab)


LOG2E = 1.4426950408889634


def _softplus(v):
    return jnp.maximum(v, 0.0) + jnp.log(1.0 + jnp.exp(-jnp.abs(v)))


def _head_stack(v, head_lanes):
    return jnp.concatenate([jnp.where(m, v, jnp.zeros_like(v)) for m in head_lanes], axis=0)


def _ssd_kernel(xs_ref, bm_ref, cm_ref, z_ref, xsc_ref, bmc_ref, dtt_ref, dttc_ref,
                cwx_ref, cwb_ref, cwc_ref, cbx_ref, cbb_ref, cbc_ref,
                alog_ref, dtb_ref, skip_ref, ng_ref, o_ref,
                pad_ref, padc_ref, ux_ref, ubt_ref, uc_ref, uxc_ref, ubtc_ref,
                rows_ref, cols_ref, dec_ref, rowsc_ref, decc_ref, y_ref, st_ref, cc_ref):
    q = SSM_CHUNK
    hpg = HEADS_PER_GROUP
    s_len = xs_ref.shape[0]
    c_len = xsc_ref.shape[0]
    n_lat = s_len // q
    n_ctx = c_len // q
    halo = 8

    zeros_halo = jnp.zeros((halo, pad_ref.shape[1]), F32)
    pad_ref[0:halo, :] = zeros_halo
    pad_ref[s_len + halo:s_len + 2 * halo, :] = zeros_halo
    pad_ref[halo:s_len + halo, 0:GROUP_X] = xs_ref[...].astype(F32)
    pad_ref[halo:s_len + halo, GROUP_X:GROUP_X + SSM_STATE] = bm_ref[...].astype(F32)
    pad_ref[halo:s_len + halo, GROUP_X + SSM_STATE:] = cm_ref[...].astype(F32)
    zeros_halo_c = jnp.zeros((halo, padc_ref.shape[1]), F32)
    padc_ref[0:halo, :] = zeros_halo_c
    padc_ref[c_len + halo:c_len + 2 * halo, :] = zeros_halo_c
    padc_ref[halo:c_len + halo, 0:GROUP_X] = xsc_ref[...].astype(F32)
    padc_ref[halo:c_len + halo, GROUP_X:] = bmc_ref[...].astype(F32)

    cw = jnp.concatenate([cwx_ref[...], cwb_ref[...], cwc_ref[...]], axis=1)
    cb = jnp.concatenate([cbx_ref[...], cbb_ref[...], cbc_ref[...]], axis=1)

    def conv_chunk(src_ref, start, width):
        win = src_ref[pl.ds(start, q + 2 * halo), :]
        acc = jnp.broadcast_to(cb[:, :width], (q, width))
        for k in range(SSM_CONV):
            off = halo - SSM_CONV // 2 + k
            acc = acc + cw[k:k + 1, :width] * win[off:off + q, :]
        return acc * _sigmoid(acc)

    def conv_lat(c, carry):
        start = pl.multiple_of(c * q, q)
        u = conv_chunk(pad_ref, start, GROUP_X + 2 * SSM_STATE)
        ux_ref[pl.ds(start, q), :] = u[:, :GROUP_X]
        ubt_ref[c] = u[:, GROUP_X:GROUP_X + SSM_STATE].T.astype(BF16)
        uc_ref[pl.ds(start, q), :] = u[:, GROUP_X + SSM_STATE:].astype(BF16)
        return carry

    lax.fori_loop(0, n_lat, conv_lat, 0)
    for c in range(n_ctx):
        u = conv_chunk(padc_ref, c * q, GROUP_X + SSM_STATE)
        uxc_ref[c * q:(c + 1) * q, :] = u[:, :GROUP_X]
        ubtc_ref[c] = u[:, GROUP_X:].T.astype(BF16)

    ki = lax.broadcasted_iota(jnp.int32, (q, q), 0)
    ii = lax.broadcasted_iota(jnp.int32, (q, q), 1)
    tri_f = (ki <= ii).astype(F32)
    tri_b = (ki >= ii).astype(F32)
    is_fwd_row = lax.broadcasted_iota(jnp.int32, (N_DIRS * hpg, 1), 0) < hpg
    lane_x = lax.broadcasted_iota(jnp.int32, (1, GROUP_X), 1)
    head_lanes = [(lane_x >= r * SSM_HEAD_DIM) & (lane_x < (r + 1) * SSM_HEAD_DIM) for r in range(hpg)]
    neg_a2 = -jnp.exp(alog_ref[...]) * LOG2E
    dt_bias = dtb_ref[...]

    def dt_forms(raw):
        dt8 = _softplus(raw + dt_bias)
        a8 = neg_a2 * dt8
        cum_f = jnp.dot(a8, tri_f, preferred_element_type=F32, precision=lax.Precision.HIGHEST)
        cum_b = jnp.dot(a8, tri_b, preferred_element_type=F32, precision=lax.Precision.HIGHEST)
        acum8 = jnp.where(is_fwd_row, cum_f, cum_b)
        atot8 = jnp.where(is_fwd_row, acum8[:, q - 1:q], acum8[:, 0:1])
        s8 = dt8 * jnp.exp2(atot8 - acum8)
        dec8 = jnp.exp2(atot8)
        dec_rows = []
        for d in range(N_DIRS):
            row = jnp.broadcast_to(dec8[d * hpg + hpg - 1:d * hpg + hpg, :], (1, GROUP_X))
            for r in range(hpg - 2, -1, -1):
                row = jnp.where(lane_x < (r + 1) * SSM_HEAD_DIM, dec8[d * hpg + r:d * hpg + r + 1, :], row)
            dec_rows.append(row)
        return jnp.concatenate([dt8, acum8, s8], axis=0), acum8.T, jnp.concatenate(dec_rows, axis=0)

    for c in range(n_lat):
        rows_ref[c], cols_ref[c], dec_ref[c] = dt_forms(dtt_ref[:, c * q:(c + 1) * q])
    for c in range(n_ctx):
        rowsc_ref[c], _, decc_ref[c] = dt_forms(dttc_ref[:, c * q:(c + 1) * q])

    def state_term(d, rows, btf, xm):
        parts = [(btf * rows[2 * N_DIRS * hpg + d * hpg + r:2 * N_DIRS * hpg + d * hpg + r + 1, :]).astype(BF16)
                 for r in range(hpg)]
        return _dot(jnp.concatenate(parts, axis=1), xm)

    h_init = []
    for d in range(N_DIRS):
        h = jnp.zeros((SSM_STATE, GROUP_X), F32)
        for c in (range(n_ctx) if d == 0 else reversed(range(n_ctx))):
            xm = _head_stack(uxc_ref[c * q:(c + 1) * q, :].astype(BF16), head_lanes)
            h = decc_ref[c][d:d + 1, :] * h + state_term(d, rowsc_ref[c], ubtc_ref[c].astype(F32), xm)
        h_init.append(h)

    skip = skip_ref[...]
    norm_g = ng_ref[...]
    keep = (ki >= ii, ki <= ii)

    def chunk_body(c, carry):
        start = pl.multiple_of(c * q, q)
        rows = rows_ref[c]
        cols = cols_ref[c]
        xs = ux_ref[pl.ds(start, q), :]
        xm = _head_stack(xs.astype(BF16), head_lanes)
        cmat = uc_ref[pl.ds(start, q), :]
        bt = ubt_ref[c]
        cbm = _dot(cmat, bt)
        cf = cmat.astype(F32)
        btf = bt.astype(F32)
        m_parts = []
        c_parts = [[], []]
        for r in range(hpg):
            t = None
            for d in range(N_DIRS):
                k = d * hpg + r
                acol = jnp.broadcast_to(cols[:, k:k + 1], (q, q))
                arow = rows[N_DIRS * hpg + k:N_DIRS * hpg + k + 1, :]
                term = jnp.where(keep[d], jnp.exp2(acol - arow) * rows[k:k + 1, :], 0.0)
                t = term if t is None else t + term
                c_parts[d].append((cf * jnp.exp2(acol)).astype(BF16))
            m_parts.append((cbm * t).astype(BF16))
        y_ref[pl.ds(start, q), :] = xs * skip + _dot(jnp.concatenate(m_parts, axis=1), xm)
        for d in range(N_DIRS):
            st_ref[d, c] = state_term(d, rows, btf, xm)
            cc_ref[d, c] = jnp.concatenate(c_parts[d], axis=1)
        return carry

    lax.fori_loop(0, n_lat, chunk_body, 0, unroll=2)

    def scan_body(j, carry, finalize):
        new = []
        chunk_ids = (j, n_lat - 1 - j)
        for d in range(N_DIRS):
            h = carry[d]
            c = chunk_ids[d]
            start = pl.multiple_of(c * q, q)
            y_off = _dot(cc_ref[d, c], _head_stack(h.astype(BF16), head_lanes))
            y_ref[pl.ds(start, q), :] = y_ref[pl.ds(start, q), :] + y_off
            new.append(dec_ref[c][d:d + 1, :] * h + st_ref[d, c])
        if finalize:
            for c in chunk_ids:
                start = pl.multiple_of(c * q, q)
                z = z_ref[pl.ds(start, q), :].astype(F32)
                u = y_ref[pl.ds(start, q), :] * (z * _sigmoid(z))
                u = u * lax.rsqrt(jnp.mean(u * u, axis=-1, keepdims=True) + EPS)
                o_ref[pl.ds(start, q), :] = (u * norm_g).astype(BF16)
        return tuple(new)

    half = n_lat // 2
    carry = lax.fori_loop(0, half, functools.partial(scan_body, finalize=False), tuple(h_init), unroll=2)
    lax.fori_loop(half, n_lat, functools.partial(scan_body, finalize=True), carry, unroll=2)


def _ssd_mixer(p3, pc3, dtt, dttc, conv_w, conv_b, alog8, dtb8, skip_exp, norm_g):
    b, s, _ = p3.shape
    c_len = pc3.shape[1]
    g = SSM_GROUPS
    n_lat, n_ctx = s // SSM_CHUNK, c_len // SSM_CHUNK
    assert n_lat % 2 == 0
    nx = GROUP_X
    ns = SSM_STATE
    nh = N_DIRS * HEADS_PER_GROUP
    in_specs = [
        pl.BlockSpec((None, s, nx), lambda i, j: (i, 0, COL_X // nx + j)),
        pl.BlockSpec((None, s, ns), lambda i, j: (i, 0, COL_B // ns + j)),
        pl.BlockSpec((None, s, ns), lambda i, j: (i, 0, COL_C // ns + j)),
        pl.BlockSpec((None, s, nx), lambda i, j: (i, 0, COL_ZSSM // nx + j)),
        pl.BlockSpec((None, c_len, nx), lambda i, j: (i, 0, CCOL_X // nx + j)),
        pl.BlockSpec((None, c_len, ns), lambda i, j: (i, 0, CCOL_B // ns + j)),
        pl.BlockSpec((None, nh, s), lambda i, j: (i, j, 0)),
        pl.BlockSpec((None, nh, c_len), lambda i, j: (i, j, 0)),
        pl.BlockSpec((SSM_CONV, nx), lambda i, j: (0, j)),
        pl.BlockSpec((SSM_CONV, ns), lambda i, j: (0, SSM_WIDTH // ns + j)),
        pl.BlockSpec((SSM_CONV, ns), lambda i, j: (0, SSM_WIDTH // ns + g + j)),
        pl.BlockSpec((1, nx), lambda i, j: (0, j)),
        pl.BlockSpec((1, ns), lambda i, j: (0, SSM_WIDTH // ns + j)),
        pl.BlockSpec((1, ns), lambda i, j: (0, SSM_WIDTH // ns + g + j)),
        pl.BlockSpec((None, nh, 1), lambda i, j: (j, 0, 0)),
        pl.BlockSpec((None, nh, 1), lambda i, j: (j, 0, 0)),
        pl.BlockSpec((None, 1, nx), lambda i, j: (j, 0, 0)),
        pl.BlockSpec((1, nx), lambda i, j: (0, j)),
    ]
    halo = 8
    scratch = [
        pltpu.VMEM((s + 2 * halo, nx + 2 * ns), F32),
        pltpu.VMEM((c_len + 2 * halo, nx + ns), F32),
        pltpu.VMEM((s, nx), F32),
        pltpu.VMEM((n_lat, ns, SSM_CHUNK), BF16),
        pltpu.VMEM((s, ns), BF16),
        pltpu.VMEM((c_len, nx), F32),
        pltpu.VMEM((n_ctx, ns, SSM_CHUNK), BF16),
        pltpu.VMEM((n_lat, 3 * nh, SSM_CHUNK), F32),
        pltpu.VMEM((n_lat, SSM_CHUNK, nh), F32),
        pltpu.VMEM((n_lat, N_DIRS, nx), F32),
        pltpu.VMEM((n_ctx, 3 * nh, SSM_CHUNK), F32),
        pltpu.VMEM((n_ctx, N_DIRS, nx), F32),
        pltpu.VMEM((s, nx), F32),
        pltpu.VMEM((N_DIRS, n_lat, ns, nx), F32),
        pltpu.VMEM((N_DIRS, n_lat, SSM_CHUNK, HEADS_PER_GROUP * ns), BF16),
    ]
    return pl.pallas_call(
        _ssd_kernel,
        grid=(b, g),
        in_specs=in_specs,
        out_specs=pl.BlockSpec((None, s, nx), lambda i, j: (i, 0, j)),
        out_shape=jax.ShapeDtypeStruct((b, s, SSM_WIDTH), BF16),
        scratch_shapes=scratch,
        compiler_params=pltpu.CompilerParams(
            dimension_semantics=("parallel", "parallel"), vmem_limit_bytes=VMEM_LIMIT),
        name="ssd_mixer",
    )(p3, p3, p3, p3, pc3, pc3, dtt, dttc, conv_w, conv_w, conv_w, conv_b, conv_b, conv_b,
      alog8, dtb8, skip_exp, norm_g)


def _merge_kernel(yna_ref, yssm_ref, gna_ref, gssm_ref, x_ref, gate_ref, wna_ref, wssm_ref, wout_ref, gpost_ref, o_ref):
    a = _dot(yna_ref[...], wna_ref[...])
    s = _dot(yssm_ref[...], wssm_ref[...])
    m = _sigmoid(gna_ref[...].astype(F32)) * a + _sigmoid(gssm_ref[...].astype(F32)) * s
    o = _dot(m.astype(BF16), wout_ref[...])
    r = o * lax.rsqrt(jnp.mean(o * o, axis=-1, keepdims=True) + EPS) * gpost_ref[...]
    o_ref[...] = x_ref[...] + gate_ref[...] * r


def _merge(y_na, y_ssm, p3, x3, gate, w_na, w_ssm, w_out, g_post, tm):
    b, s, d = x3.shape
    const = lambda i, j: (0, 0)
    return pl.pallas_call(
        _merge_kernel,
        grid=(b, s // tm),
        in_specs=[pl.BlockSpec((None, tm, y_na.shape[-1]), lambda i, j: (i, j, 0)),
                  pl.BlockSpec((None, tm, y_ssm.shape[-1]), lambda i, j: (i, j, 0)),
                  pl.BlockSpec((None, tm, d), lambda i, j: (i, j, COL_GNA // d)),
                  pl.BlockSpec((None, tm, d), lambda i, j: (i, j, COL_GSSM // d)),
                  pl.BlockSpec((None, tm, d), lambda i, j: (i, j, 0)),
                  pl.BlockSpec((None, 1, d), lambda i, j: (i, 0, 0)),
                  pl.BlockSpec(w_na.shape, const),
                  pl.BlockSpec(w_ssm.shape, const),
                  pl.BlockSpec(w_out.shape, const),
                  pl.BlockSpec((1, d), const)],
        out_specs=pl.BlockSpec((None, tm, d), lambda i, j: (i, j, 0)),
        out_shape=jax.ShapeDtypeStruct((b, s, d), x3.dtype),
        compiler_params=pltpu.CompilerParams(
            dimension_semantics=("parallel", "parallel"), vmem_limit_bytes=VMEM_LIMIT),
        name="merge_out",
    )(y_na, y_ssm, p3, p3, x3, gate, w_na, w_ssm, w_out, g_post)


def kernel(x, c, ctx, c_ctx, w_mod, b_mod, g_pre, g_post, w_in, conv_w, conv_b, a_log, dt_bias, d_skip,
           ssm_norm_g, rpb, w_na_out, w_ssm_out, w_out):
    assert w_mod.shape[0] == 1, "single-layer block"
    b, s, d = x.shape
    g, hpg = SSM_GROUPS, HEADS_PER_GROUP

    w_in0 = w_in[0]
    col_dt = COL_GNA
    w_main = jnp.concatenate([w_in0[:, :col_dt], w_in0[:, col_dt + N_DIRS * SSM_HEADS:]], axis=1).astype(BF16)
    dt_perm = np.array([dd * SSM_HEADS + gg * hpg + r for gg in range(g) for dd in range(N_DIRS) for r in range(hpg)])
    w_dt = jnp.pad(w_in0[:, col_dt + dt_perm], ((0, 0), (0, 128 - N_DIRS * SSM_HEADS))).astype(BF16)
    per_group = lambda p: jnp.transpose(p.reshape(N_DIRS, g, hpg), (1, 0, 2)).reshape(g, N_DIRS * hpg, 1)
    alog8 = per_group(a_log[0])
    dtb8 = per_group(dt_bias[0])
    skip_exp = jnp.repeat(d_skip[0], SSM_HEAD_DIM).reshape(g, 1, GROUP_X)

    cond = jnp.zeros((16, d), F32).at[:b].set(c).at[b].set(c_ctx)
    mod = _adaln(cond, w_mod[0], b_mod[0])
    shift_l, scale_l, gate_l = (mod[:b, k * d:(k + 1) * d].reshape(b, 1, d) for k in range(3))
    shift_c, scale_c = (jnp.broadcast_to(mod[b, k * d:(k + 1) * d].reshape(1, 1, d), (b, 1, d)) for k in range(2))

    p3, dtt = _projection(x, shift_l, scale_l, g_pre, w_main, w_dt, ((0, NP_COLS),), tm=256)
    pc3, dttc = _projection(ctx, shift_c, scale_c, g_pre, w_main, w_dt,
                            ((COL_K, COL_ZNA), (COL_X, COL_GNA)), tm=256)

    bias_tab = _bias_table(rpb[0])
    y_na = _neighborhood_attention(p3, pc3, bias_tab)

    y_ssm = _ssd_mixer(p3, pc3, dtt, dttc, conv_w[0], conv_b[0].reshape(1, -1), alog8, dtb8, skip_exp,
                       ssm_norm_g)

    return _merge(y_na, y_ssm, p3, x, gate_l, w_na_out[0].astype(BF16), w_ssm_out[0].astype(BF16),
                  w_out[0].astype(BF16), g_post, tm=512)
```

```python
import functools

import numpy as np
import jax
import jax.numpy as jnp
from jax import lax
from jax.experimental import pallas as pl
from jax.experimental.pallas import tpu as pltpu

F32 = jnp.float32
BF16 = jnp.bfloat16

D_MODEL = 1024
GRID_W = 64
NA_HEADS = 16
NA_HEAD_DIM = 64
WIN_ROWS = 8
WIN_COLS = 16
SSM_WIDTH = 2 * D_MODEL
SSM_HEAD_DIM = 64
SSM_HEADS = SSM_WIDTH // SSM_HEAD_DIM
SSM_GROUPS = 8
HEADS_PER_GROUP = SSM_HEADS // SSM_GROUPS
SSM_STATE = 128
SSM_CONV = 5
SSM_CHUNK = 128
CONV_HALO = 16
CONV_BLK = 64
N_DIRS = 2
GROUP_X = HEADS_PER_GROUP * SSM_HEAD_DIM
EPS = 1e-6
NEG = -1e30
LOG2E = 1.4426950408889634

COL_Q, COL_K, COL_V, COL_ZNA = 0, 1024, 2048, 3072
COL_ZSSM, COL_X, COL_B, COL_C = 4096, 6144, 8192, 9216
COL_GNA, COL_GSSM, NP_COLS = 10240, 11264, 12288
CCOL_K, CCOL_V, CCOL_X, CCOL_B, NPC_COLS = 0, 1024, 2048, 4096, 6144

Q_ROWS = 4
Q_BLK = Q_ROWS * GRID_W
KEY_ROWS = 12
KEY_BLK = KEY_ROWS * GRID_W
N_BIAS_CASES = 5
BIAS_CASE_BLOCK = (0, 1, 2, 6, 7)

VMEM_LIMIT = 56 * 1024 * 1024


def _sigmoid(v):
    return 1.0 / (1.0 + jnp.exp(-v))


def _dot(a, b):
    return jnp.dot(a, b, preferred_element_type=F32)


def _dot_nt(a, b):
    return lax.dot_general(a, b, (((1,), (1,)), ((), ())), preferred_element_type=F32)


def _adaln_kernel(cond_ref, w_ref, b_ref, o_ref):
    cnd = cond_ref[...]
    act = cnd * _sigmoid(cnd)
    o_ref[...] = jnp.dot(act, w_ref[...], preferred_element_type=F32,
                         precision=lax.Precision.HIGHEST) + b_ref[...]


def _adaln(cond, w_mod, b_mod):
    rows, d = cond.shape
    n = w_mod.shape[1]
    tn = 1024
    return pl.pallas_call(
        _adaln_kernel,
        grid=(n // tn,),
        in_specs=[pl.BlockSpec((rows, d), lambda j: (0, 0)),
                  pl.BlockSpec((d, tn), lambda j: (0, j)),
                  pl.BlockSpec((1, tn), lambda j: (0, j))],
        out_specs=pl.BlockSpec((rows, tn), lambda j: (0, j)),
        out_shape=jax.ShapeDtypeStruct((rows, n), F32),
        name="adaln",
    )(cond, w_mod, b_mod.reshape(1, n))


def _proj_kernel(x_ref, shift_ref, scale_ref, g_ref, w_ref, wdt_ref, p_ref, dtt_ref, *, chunks):
    x = x_ref[...]
    ms = jnp.mean(x * x, axis=-1, keepdims=True)
    h = x * lax.rsqrt(ms + EPS) * g_ref[...]
    h = h * (1.0 + scale_ref[...]) + shift_ref[...]
    hb = h.astype(BF16)
    for src, dst, width in chunks:
        p_ref[:, dst:dst + width] = _dot(hb, w_ref[:, src:src + width]).astype(BF16)
    dt = _dot(hb, wdt_ref[...])
    dtt_ref[...] = dt.T[:N_DIRS * SSM_HEADS]


def _projection(x3, shift, scale, g_pre, w_main, w_dt, col_ranges, tm):
    b, l, d = x3.shape
    chunks, dst = [], 0
    cw = 512
    for lo, hi in col_ranges:
        for src in range(lo, hi, cw):
            chunks.append((src, dst, cw))
            dst += cw
    n_out = dst
    n_dt = N_DIRS * SSM_HEADS
    return pl.pallas_call(
        functools.partial(_proj_kernel, chunks=tuple(chunks)),
        grid=(b, l // tm),
        in_specs=[pl.BlockSpec((None, tm, d), lambda i, j: (i, j, 0)),
                  pl.BlockSpec((None, 1, d), lambda i, j: (i, 0, 0)),
                  pl.BlockSpec((None, 1, d), lambda i, j: (i, 0, 0)),
                  pl.BlockSpec((1, d), lambda i, j: (0, 0)),
                  pl.BlockSpec(w_main.shape, lambda i, j: (0, 0), pipeline_mode=pl.Buffered(1)),
                  pl.BlockSpec(w_dt.shape, lambda i, j: (0, 0), pipeline_mode=pl.Buffered(1))],
        out_specs=[pl.BlockSpec((None, tm, n_out), lambda i, j: (i, j, 0)),
                   pl.BlockSpec((None, n_dt, tm), lambda i, j: (i, 0, j))],
        out_shape=[jax.ShapeDtypeStruct((b, l, n_out), BF16),
                   jax.ShapeDtypeStruct((b, n_dt, l), F32)],
        compiler_params=pltpu.CompilerParams(
            dimension_semantics=("parallel", "parallel"), vmem_limit_bytes=VMEM_LIMIT),
        name="projection",
    )(x3, shift, scale, g_pre, w_main, w_dt)


def _bias_block_plan():
    rows = 2048 // GRID_W
    plan = np.full((N_BIAS_CASES, Q_ROWS, KEY_ROWS), -1, np.int64)
    for case, blk in enumerate(BIAS_CASE_BLOCK):
        u0 = int(np.clip(Q_ROWS * blk - WIN_ROWS // 2, 0, rows - KEY_ROWS))
        for rho in range(Q_ROWS):
            r = Q_ROWS * blk + rho
            r0 = int(np.clip(r - WIN_ROWS // 2, 0, rows - WIN_ROWS))
            for a in range(KEY_ROWS):
                krow = u0 + a
                if r0 <= krow < r0 + WIN_ROWS:
                    plan[case, rho, a] = krow - r + WIN_ROWS - 1
    return plan


def _bias_kernel(rpb_ref, o_ref, toep_ref, *, plan):
    head = pl.program_id(0)
    n_dr = 2 * WIN_ROWS - 1
    n_dc = 2 * WIN_COLS - 1
    qc = lax.broadcasted_iota(jnp.int32, (GRID_W, GRID_W), 0)
    kc = lax.broadcasted_iota(jnp.int32, (GRID_W, GRID_W), 1)
    diff = kc - qc + (WIN_COLS - 1)
    c0 = jnp.clip(qc - WIN_COLS // 2, 0, GRID_W - WIN_COLS)
    in_win = (kc >= c0) & (kc < c0 + WIN_COLS)
    base = head * (n_dr * n_dc)
    for dr in range(n_dr):
        blk = jnp.zeros((GRID_W, GRID_W), F32)
        for j in range(n_dc):
            blk = jnp.where(diff == j, rpb_ref[base + dr * n_dc + j] * LOG2E, blk)
        toep_ref[dr] = jnp.where(in_win, blk, NEG)
    neg = jnp.full((GRID_W, GRID_W), NEG, F32)
    for case in range(N_BIAS_CASES):
        for rho in range(Q_ROWS):
            for a in range(KEY_ROWS):
                dr = int(plan[case, rho, a])
                val = neg if dr < 0 else toep_ref[dr]
                o_ref[case, rho * GRID_W:(rho + 1) * GRID_W, a * GRID_W:(a + 1) * GRID_W] = val.astype(BF16)


def _bias_table(rpb):
    plan = _bias_block_plan()
    return pl.pallas_call(
        functools.partial(_bias_kernel, plan=plan),
        grid=(NA_HEADS,),
        in_specs=[pl.BlockSpec(memory_space=pltpu.SMEM)],
        out_specs=pl.BlockSpec((N_BIAS_CASES, None, Q_BLK, KEY_BLK), lambda h: (0, h, 0, 0)),
        out_shape=jax.ShapeDtypeStruct((N_BIAS_CASES, NA_HEADS, Q_BLK, KEY_BLK), BF16),
        scratch_shapes=[pltpu.VMEM((2 * WIN_ROWS - 1, GRID_W, GRID_W), F32)],
        name="bias_table",
    )(rpb.reshape(-1))


def _na_kernel(q_ref, k0_ref, k1_ref, k2_ref, kc_ref, v0_ref, v1_ref, v2_ref, vc_ref, z_ref, bias_ref, o_ref):
    lane = lax.broadcasted_iota(jnp.int32, (1, 2 * NA_HEAD_DIM), 1)
    k_refs = (k0_ref, k1_ref, k2_ref, kc_ref)
    v_refs = (v0_ref, v1_ref, v2_ref, vc_ref)
    lane_full = lax.broadcasted_iota(jnp.int32, (Q_BLK, 2 * NA_HEAD_DIM), 1)
    sum_lanes = (NA_HEAD_DIM, 0)
    fills = [jnp.where(lane_full == sl, 1.0, 0.0).astype(BF16) for sl in sum_lanes]
    for pair in range(NA_HEADS // 2):
        cs = slice(pair * 128, (pair + 1) * 128)
        qp = q_ref[:, cs]
        kt = [r[:, cs] for r in k_refs]
        vt = [r[:, cs] for r in v_refs]
        q_stack = jnp.concatenate([jnp.where(lane < NA_HEAD_DIM, qp, jnp.zeros_like(qp)),
                                   jnp.where(lane >= NA_HEAD_DIM, qp, jnp.zeros_like(qp))], axis=0)
        s_both = [_dot_nt(q_stack, kk) for kk in kt]
        outs = []
        for hh in range(2):
            head_lanes = (lane < NA_HEAD_DIM) if hh == 0 else (lane >= NA_HEAD_DIM)
            sum_lane = sum_lanes[hh]
            s = [sb[hh * Q_BLK:(hh + 1) * Q_BLK] for sb in s_both]
            for t in range(3):
                s[t] = s[t] + bias_ref[2 * pair + hh, :, t * 256:(t + 1) * 256].astype(F32)
            mx = jnp.max(jnp.maximum(jnp.maximum(s[0], s[1]), jnp.maximum(s[2], s[3])), axis=-1, keepdims=True)
            o = None
            for t in range(4):
                e = jnp.exp2((s[t] - mx).astype(BF16))
                c = _dot(e, jnp.where(head_lanes, vt[t], fills[hh]))
                o = c if o is None else o + c
            outs.append(o * (1.0 / o[:, sum_lane:sum_lane + 1]))
        acc = jnp.where(lane < NA_HEAD_DIM, outs[0], outs[1])
        z = z_ref[:, cs].astype(F32)
        o_ref[:, cs] = (acc * (z * _sigmoid(z))).astype(BF16)


def _neighborhood_attention(p3, pc3, bias_tab):
    b, s, _ = p3.shape
    n_blk = s // Q_BLK

    def key_blk(i):
        return jnp.clip(i - 1, 0, n_blk - KEY_ROWS // Q_ROWS)

    def case_of(i):
        return jnp.minimum(i, 2) + jnp.maximum(i - (n_blk - 3), 0)

    cq, ck, cv, cz = COL_Q // 1024, COL_K // 1024, COL_V // 1024, COL_ZNA // 1024
    blk = (None, Q_BLK, 1024)
    in_specs = [pl.BlockSpec(blk, lambda i, j: (j, i, cq))]
    in_specs += [pl.BlockSpec(blk, functools.partial(lambda i, j, t: (j, key_blk(i) + t, ck), t=t)) for t in range(3)]
    in_specs += [pl.BlockSpec(blk, lambda i, j: (j, 0, CCOL_K // 1024))]
    in_specs += [pl.BlockSpec(blk, functools.partial(lambda i, j, t: (j, key_blk(i) + t, cv), t=t)) for t in range(3)]
    in_specs += [pl.BlockSpec(blk, lambda i, j: (j, 0, CCOL_V // 1024))]
    in_specs += [pl.BlockSpec(blk, lambda i, j: (j, i, cz))]
    in_specs += [pl.BlockSpec((None, NA_HEADS, Q_BLK, KEY_BLK), lambda i, j: (case_of(i), 0, 0, 0))]
    return pl.pallas_call(
        _na_kernel,
        grid=(n_blk, b),
        in_specs=in_specs,
        out_specs=pl.BlockSpec(blk, lambda i, j: (j, i, 0)),
        out_shape=jax.ShapeDtypeStruct((b, s, NA_HEADS * NA_HEAD_DIM), BF16),
        compiler_params=pltpu.CompilerParams(
            dimension_semantics=("parallel", "parallel"), vmem_limit_bytes=VMEM_LIMIT),
        name="neighborhood_attention",
    )(p3, p3, p3, p3, pc3, p3, p3, p3, pc3, p3, bias_tab)


def _softplus(v):
    return jnp.maximum(v, 0.0) + jnp.log(1.0 + jnp.exp(-jnp.abs(v)))


def _head_stack(v, head_lanes):
    return jnp.concatenate([jnp.where(m, v, jnp.zeros_like(v)) for m in head_lanes], axis=0)


def _ssd_kernel(xs_ref, bm_ref, cm_ref, z_ref, xsc_ref, bmc_ref, dtt_ref, dttc_ref,
                cwx_ref, cwb_ref, cwc_ref, cbx_ref, cbb_ref, cbc_ref,
                alog_ref, dtb_ref, skip_ref, ng_ref, o_ref,
                pad_ref, padc_ref, tap_ref, ux_ref, ubt_ref, uc_ref, uxc_ref, ubtc_ref,
                rows_ref, cols_ref, dec_ref, rowsc_ref, decc_ref, y_ref, st_ref, cc_ref):
    q = SSM_CHUNK
    hpg = HEADS_PER_GROUP
    nh = N_DIRS * hpg
    s_len = xs_ref.shape[0]
    c_len = xsc_ref.shape[0]
    n_lat = s_len // q
    n_ctx = c_len // q
    halo = CONV_HALO
    blk = CONV_BLK
    win = blk + 2 * halo

    zeros_halo = jnp.zeros((halo, pad_ref.shape[1]), BF16)
    pad_ref[0:halo, :] = zeros_halo
    pad_ref[s_len + halo:s_len + 2 * halo, :] = zeros_halo
    pad_ref[halo:s_len + halo, 0:GROUP_X] = xs_ref[...]
    pad_ref[halo:s_len + halo, GROUP_X:GROUP_X + SSM_STATE] = bm_ref[...]
    pad_ref[halo:s_len + halo, GROUP_X + SSM_STATE:] = cm_ref[...]
    zeros_halo_c = jnp.zeros((halo, padc_ref.shape[1]), BF16)
    padc_ref[0:halo, :] = zeros_halo_c
    padc_ref[c_len + halo:c_len + 2 * halo, :] = zeros_halo_c
    padc_ref[halo:c_len + halo, 0:GROUP_X] = xsc_ref[...]
    padc_ref[halo:c_len + halo, GROUP_X:] = bmc_ref[...]

    cw = jnp.concatenate([cwx_ref[...], cwb_ref[...], cwc_ref[...]], axis=1)
    cb = jnp.concatenate([cbx_ref[...], cbb_ref[...], cbc_ref[...]], axis=1)
    for k in range(SSM_CONV):
        tap_ref[k] = jnp.broadcast_to(cw[k:k + 1, :], (win, cw.shape[1])).astype(BF16)
    off = lax.broadcasted_iota(jnp.int32, (blk, SSM_CONV * win), 1) - lax.broadcasted_iota(
        jnp.int32, (blk, SSM_CONV * win), 0)
    hit = off == halo - SSM_CONV // 2
    for k in range(1, SSM_CONV):
        hit = hit | (off == k * win + halo - SSM_CONV // 2 + k)
    shift = jnp.where(hit, 1.0, 0.0).astype(BF16)

    def conv_block(src_ref, start, width):
        w = src_ref[pl.ds(start, win), :]
        stack = jnp.concatenate([w * tap_ref[k, :, 0:width] for k in range(SSM_CONV)], axis=0)
        acc = _dot(shift, stack) + cb[:, :width]
        return acc * _sigmoid(acc)

    def conv_chunk(src_ref, start, width):
        return jnp.concatenate([conv_block(src_ref, start + i * blk, width) for i in range(q // blk)], axis=0)

    def conv_lat(c, carry):
        start = pl.multiple_of(c * q, q)
        u = conv_chunk(pad_ref, start, GROUP_X + 2 * SSM_STATE)
        ux_ref[pl.ds(start, q), :] = u[:, :GROUP_X]
        ubt_ref[c] = u[:, GROUP_X:GROUP_X + SSM_STATE].T.astype(BF16)
        uc_ref[pl.ds(start, q), :] = u[:, GROUP_X + SSM_STATE:].astype(BF16)
        return carry

    lax.fori_loop(0, n_lat, conv_lat, 0, unroll=4)
    for c in range(n_ctx):
        u = conv_chunk(padc_ref, c * q, GROUP_X + SSM_STATE)
        uxc_ref[c * q:(c + 1) * q, :] = u[:, :GROUP_X]
        ubtc_ref[c] = u[:, GROUP_X:].T.astype(BF16)

    ki = lax.broadcasted_iota(jnp.int32, (q, q), 0)
    ii = lax.broadcasted_iota(jnp.int32, (q, q), 1)
    tri_f = (ki <= ii).astype(F32)
    tri_b = (ki >= ii).astype(F32)
    lane_x = lax.broadcasted_iota(jnp.int32, (1, GROUP_X), 1)
    head_lanes = [(lane_x >= r * SSM_HEAD_DIM) & (lane_x < (r + 1) * SSM_HEAD_DIM) for r in range(hpg)]
    neg_a2 = -jnp.exp(alog_ref[...]) * LOG2E
    dt_bias = dtb_ref[...]

    def dt_forms(raw, n, rows_out, cols_out, dec_out):
        stack = lambda v: jnp.concatenate([v[:, c * q:(c + 1) * q] for c in range(n)], axis=0)
        dt_all = _softplus(raw + dt_bias)
        dt = stack(dt_all)
        a = stack(neg_a2 * dt_all)
        is_fwd = lax.broadcasted_iota(jnp.int32, (n * nh, 1), 0) % nh < hpg
        cum_f = jnp.dot(a, tri_f, preferred_element_type=F32, precision=lax.Precision.HIGHEST)
        cum_b = jnp.dot(a, tri_b, preferred_element_type=F32, precision=lax.Precision.HIGHEST)
        acum = jnp.where(is_fwd, cum_f, cum_b)
        atot = jnp.where(is_fwd, acum[:, q - 1:q], acum[:, 0:1])
        sw = dt * jnp.exp2(atot - acum)
        dec = jnp.exp2(atot)
        for c in range(n):
            sl = slice(c * nh, (c + 1) * nh)
            rows_out[c] = jnp.concatenate([dt[sl], acum[sl], sw[sl]], axis=0)
            if cols_out is not None:
                cols_out[c] = acum[sl].T
            dec_rows = []
            for d in range(N_DIRS):
                base = c * nh + d * hpg
                row = jnp.broadcast_to(dec[base + hpg - 1:base + hpg, :], (1, GROUP_X))
                for r in range(hpg - 2, -1, -1):
                    row = jnp.where(lane_x < (r + 1) * SSM_HEAD_DIM, dec[base + r:base + r + 1, :], row)
                dec_rows.append(row)
            dec_out[c] = jnp.concatenate(dec_rows, axis=0)

    dt_forms(dtt_ref[...], n_lat, rows_ref, cols_ref, dec_ref)
    dt_forms(dttc_ref[...], n_ctx, rowsc_ref, None, decc_ref)

    def state_term(d, rows, btf, xm):
        parts = [(btf * rows[2 * N_DIRS * hpg + d * hpg + r:2 * N_DIRS * hpg + d * hpg + r + 1, :]).astype(BF16)
                 for r in range(hpg)]
        return _dot(jnp.concatenate(parts, axis=1), xm)

    h_init = []
    for d in range(N_DIRS):
        h = jnp.zeros((SSM_STATE, GROUP_X), F32)
        for c in (range(n_ctx) if d == 0 else reversed(range(n_ctx))):
            xm = _head_stack(uxc_ref[c * q:(c + 1) * q, :].astype(BF16), head_lanes)
            h = decc_ref[c][d:d + 1, :] * h + state_term(d, rowsc_ref[c], ubtc_ref[c].astype(F32), xm)
        h_init.append(h)

    skip = skip_ref[...]
    norm_g = ng_ref[...]
    keep = (ki >= ii, ki <= ii)

    def chunk_body(c, carry):
        start = pl.multiple_of(c * q, q)
        rows = rows_ref[c]
        cols = cols_ref[c]
        xs = ux_ref[pl.ds(start, q), :]
        xm = _head_stack(xs.astype(BF16), head_lanes)
        cmat = uc_ref[pl.ds(start, q), :]
        bt = ubt_ref[c]
        cbm = _dot(cmat, bt)
        cf = cmat.astype(F32)
        btf = bt.astype(F32)
        m_parts = []
        c_parts = [[], []]
        for r in range(hpg):
            t = None
            for d in range(N_DIRS):
                k = d * hpg + r
                acol = jnp.broadcast_to(cols[:, k:k + 1], (q, q))
                arow = rows[N_DIRS * hpg + k:N_DIRS * hpg + k + 1, :]
                term = jnp.where(keep[d], jnp.exp2(acol - arow) * rows[k:k + 1, :], 0.0)
                t = term if t is None else t + term
                c_parts[d].append((cf * jnp.exp2(acol)).astype(BF16))
            m_parts.append((cbm * t).astype(BF16))
        y_ref[pl.ds(start, q), :] = xs * skip + _dot(jnp.concatenate(m_parts, axis=1), xm)
        for d in range(N_DIRS):
            st_ref[d, c] = state_term(d, rows, btf, xm)
            cc_ref[d, c] = jnp.concatenate(c_parts[d], axis=1)
        return carry

    lax.fori_loop(0, n_lat, chunk_body, 0, unroll=2)

    def scan_body(j, carry, finalize):
        new = []
        chunk_ids = (j, n_lat - 1 - j)
        for d in range(N_DIRS):
            h = carry[d]
            c = chunk_ids[d]
            start = pl.multiple_of(c * q, q)
            y_off = _dot(cc_ref[d, c], _head_stack(h.astype(BF16), head_lanes))
            y_ref[pl.ds(start, q), :] = y_ref[pl.ds(start, q), :] + y_off
            new.append(dec_ref[c][d:d + 1, :] * h + st_ref[d, c])
        if finalize:
            for c in chunk_ids:
                start = pl.multiple_of(c * q, q)
                z = z_ref[pl.ds(start, q), :].astype(F32)
                u = y_ref[pl.ds(start, q), :] * (z * _sigmoid(z))
                u = u * lax.rsqrt(jnp.mean(u * u, axis=-1, keepdims=True) + EPS)
                o_ref[pl.ds(start, q), :] = (u * norm_g).astype(BF16)
        return tuple(new)

    half = n_lat // 2
    carry = lax.fori_loop(0, half, functools.partial(scan_body, finalize=False), tuple(h_init), unroll=2)
    lax.fori_loop(half, n_lat, functools.partial(scan_body, finalize=True), carry, unroll=2)


def _ssd_mixer(p3, pc3, dtt, dttc, conv_w, conv_b, alog8, dtb8, skip_exp, norm_g):
    b, s, _ = p3.shape
    c_len = pc3.shape[1]
    g = SSM_GROUPS
    n_lat, n_ctx = s // SSM_CHUNK, c_len // SSM_CHUNK
    assert n_lat % 2 == 0
    nx = GROUP_X
    ns = SSM_STATE
    nh = N_DIRS * HEADS_PER_GROUP
    in_specs = [
        pl.BlockSpec((None, s, nx), lambda i, j: (i, 0, COL_X // nx + j)),
        pl.BlockSpec((None, s, ns), lambda i, j: (i, 0, COL_B // ns + j)),
        pl.BlockSpec((None, s, ns), lambda i, j: (i, 0, COL_C // ns + j)),
        pl.BlockSpec((None, s, nx), lambda i, j: (i, 0, COL_ZSSM // nx + j)),
        pl.BlockSpec((None, c_len, nx), lambda i, j: (i, 0, CCOL_X // nx + j)),
        pl.BlockSpec((None, c_len, ns), lambda i, j: (i, 0, CCOL_B // ns + j)),
        pl.BlockSpec((None, nh, s), lambda i, j: (i, j, 0)),
        pl.BlockSpec((None, nh, c_len), lambda i, j: (i, j, 0)),
        pl.BlockSpec((SSM_CONV, nx), lambda i, j: (0, j)),
        pl.BlockSpec((SSM_CONV, ns), lambda i, j: (0, SSM_WIDTH // ns + j)),
        pl.BlockSpec((SSM_CONV, ns), lambda i, j: (0, SSM_WIDTH // ns + g + j)),
        pl.BlockSpec((1, nx), lambda i, j: (0, j)),
        pl.BlockSpec((1, ns), lambda i, j: (0, SSM_WIDTH // ns + j)),
        pl.BlockSpec((1, ns), lambda i, j: (0, SSM_WIDTH // ns + g + j)),
        pl.BlockSpec((None, nh, 1), lambda i, j: (j, 0, 0)),
        pl.BlockSpec((None, nh, 1), lambda i, j: (j, 0, 0)),
        pl.BlockSpec((None, 1, nx), lambda i, j: (j, 0, 0)),
        pl.BlockSpec((1, nx), lambda i, j: (0, j)),
    ]
    halo = CONV_HALO
    scratch = [
        pltpu.VMEM((s + 2 * halo, nx + 2 * ns), BF16),
        pltpu.VMEM((c_len + 2 * halo, nx + ns), BF16),
        pltpu.VMEM((SSM_CONV, CONV_BLK + 2 * halo, nx + 2 * ns), BF16),
        pltpu.VMEM((s, nx), F32),
        pltpu.VMEM((n_lat, ns, SSM_CHUNK), BF16),
        pltpu.VMEM((s, ns), BF16),
        pltpu.VMEM((c_len, nx), F32),
        pltpu.VMEM((n_ctx, ns, SSM_CHUNK), BF16),
        pltpu.VMEM((n_lat, 3 * nh, SSM_CHUNK), F32),
        pltpu.VMEM((n_lat, SSM_CHUNK, nh), F32),
        pltpu.VMEM((n_lat, N_DIRS, nx), F32),
        pltpu.VMEM((n_ctx, 3 * nh, SSM_CHUNK), F32),
        pltpu.VMEM((n_ctx, N_DIRS, nx), F32),
        pltpu.VMEM((s, nx), F32),
        pltpu.VMEM((N_DIRS, n_lat, ns, nx), F32),
        pltpu.VMEM((N_DIRS, n_lat, SSM_CHUNK, HEADS_PER_GROUP * ns), BF16),
    ]
    return pl.pallas_call(
        _ssd_kernel,
        grid=(b, g),
        in_specs=in_specs,
        out_specs=pl.BlockSpec((None, s, nx), lambda i, j: (i, 0, j)),
        out_shape=jax.ShapeDtypeStruct((b, s, SSM_WIDTH), BF16),
        scratch_shapes=scratch,
        compiler_params=pltpu.CompilerParams(
            dimension_semantics=("parallel", "parallel"), vmem_limit_bytes=VMEM_LIMIT),
        name="ssd_mixer",
    )(p3, p3, p3, p3, pc3, pc3, dtt, dttc, conv_w, conv_w, conv_w, conv_b, conv_b, conv_b,
      alog8, dtb8, skip_exp, norm_g)


def _merge_kernel(yna_ref, yssm_ref, gna_ref, gssm_ref, x_ref, gate_ref, wna_ref, wssm_ref, wout_ref, gpost_ref, o_ref):
    a = _dot(yna_ref[...], wna_ref[...])
    s = _dot(yssm_ref[...], wssm_ref[...])
    m = _sigmoid(gna_ref[...].astype(F32)) * a + _sigmoid(gssm_ref[...].astype(F32)) * s
    o = _dot(m.astype(BF16), wout_ref[...])
    r = o * lax.rsqrt(jnp.mean(o * o, axis=-1, keepdims=True) + EPS) * gpost_ref[...]
    o_ref[...] = x_ref[...] + gate_ref[...] * r


def _merge(y_na, y_ssm, p3, x3, gate, w_na, w_ssm, w_out, g_post, tm):
    b, s, d = x3.shape
    const = lambda i, j: (0, 0)
    return pl.pallas_call(
        _merge_kernel,
        grid=(b, s // tm),
        in_specs=[pl.BlockSpec((None, tm, y_na.shape[-1]), lambda i, j: (i, j, 0)),
                  pl.BlockSpec((None, tm, y_ssm.shape[-1]), lambda i, j: (i, j, 0)),
                  pl.BlockSpec((None, tm, d), lambda i, j: (i, j, COL_GNA // d)),
                  pl.BlockSpec((None, tm, d), lambda i, j: (i, j, COL_GSSM // d)),
                  pl.BlockSpec((None, tm, d), lambda i, j: (i, j, 0)),
                  pl.BlockSpec((None, 1, d), lambda i, j: (i, 0, 0)),
                  pl.BlockSpec(w_na.shape, const),
                  pl.BlockSpec(w_ssm.shape, const),
                  pl.BlockSpec(w_out.shape, const),
                  pl.BlockSpec((1, d), const)],
        out_specs=pl.BlockSpec((None, tm, d), lambda i, j: (i, j, 0)),
        out_shape=jax.ShapeDtypeStruct((b, s, d), x3.dtype),
        compiler_params=pltpu.CompilerParams(
            dimension_semantics=("parallel", "parallel"), vmem_limit_bytes=VMEM_LIMIT),
        name="merge_out",
    )(y_na, y_ssm, p3, p3, x3, gate, w_na, w_ssm, w_out, g_post)


def kernel(x, c, ctx, c_ctx, w_mod, b_mod, g_pre, g_post, w_in, conv_w, conv_b, a_log, dt_bias, d_skip,
           ssm_norm_g, rpb, w_na_out, w_ssm_out, w_out):
    assert w_mod.shape[0] == 1, "single-layer block"
    b, s, d = x.shape
    g, hpg = SSM_GROUPS, HEADS_PER_GROUP

    w_in0 = w_in[0]
    q_scale = jnp.where(jnp.arange(w_in0.shape[1]) < COL_K, NA_HEAD_DIM ** -0.5 * LOG2E, 1.0).astype(F32)
    w_in0 = w_in0 * q_scale
    col_dt = COL_GNA
    w_main = jnp.concatenate([w_in0[:, :col_dt], w_in0[:, col_dt + N_DIRS * SSM_HEADS:]], axis=1).astype(BF16)
    dt_perm = np.array([dd * SSM_HEADS + gg * hpg + r for gg in range(g) for dd in range(N_DIRS) for r in range(hpg)])
    w_dt = jnp.pad(w_in0[:, col_dt + dt_perm], ((0, 0), (0, 128 - N_DIRS * SSM_HEADS))).astype(BF16)
    per_group = lambda p: jnp.transpose(p.reshape(N_DIRS, g, hpg), (1, 0, 2)).reshape(g, N_DIRS * hpg, 1)
    alog8 = per_group(a_log[0])
    dtb8 = per_group(dt_bias[0])
    skip_exp = jnp.repeat(d_skip[0], SSM_HEAD_DIM).reshape(g, 1, GROUP_X)

    cond = jnp.zeros((16, d), F32).at[:b].set(c).at[b].set(c_ctx)
    mod = _adaln(cond, w_mod[0], b_mod[0])
    shift_l, scale_l, gate_l = (mod[:b, k * d:(k + 1) * d].reshape(b, 1, d) for k in range(3))
    shift_c, scale_c = (jnp.broadcast_to(mod[b, k * d:(k + 1) * d].reshape(1, 1, d), (b, 1, d)) for k in range(2))

    p3, dtt = _projection(x, shift_l, scale_l, g_pre, w_main, w_dt, ((0, NP_COLS),), tm=256)
    pc3, dttc = _projection(ctx, shift_c, scale_c, g_pre, w_main, w_dt,
                            ((COL_K, COL_ZNA), (COL_X, COL_GNA)), tm=256)

    bias_tab = _bias_table(rpb[0])
    y_na = _neighborhood_attention(p3, pc3, bias_tab)

    y_ssm = _ssd_mixer(p3, pc3, dtt, dttc, conv_w[0], conv_b[0].reshape(1, -1), alog8, dtb8, skip_exp,
                       ssm_norm_g)

    return _merge(y_na, y_ssm, p3, x, gate_l, w_na_out[0].astype(BF16), w_ssm_out[0].astype(BF16),
                  w_out[0].astype(BF16), g_post, tm=512)
```

```python
import functools

import numpy as np
import jax
import jax.numpy as jnp
from jax import lax
from jax.experimental import pallas as pl
from jax.experimental.pallas import tpu as pltpu

F32 = jnp.float32
BF16 = jnp.bfloat16

D_MODEL = 1024
GRID_W = 64
NA_HEADS = 16
NA_HEAD_DIM = 64
WIN_ROWS = 8
WIN_COLS = 16
SSM_WIDTH = 2 * D_MODEL
SSM_HEAD_DIM = 64
SSM_HEADS = SSM_WIDTH // SSM_HEAD_DIM
SSM_GROUPS = 8
HEADS_PER_GROUP = SSM_HEADS // SSM_GROUPS
SSM_STATE = 128
SSM_CONV = 5
SSM_CHUNK = 128
CONV_HALO = 16
CONV_BLK = 64
N_DIRS = 2
GROUP_X = HEADS_PER_GROUP * SSM_HEAD_DIM
EPS = 1e-6
NEG = -1e30
LOG2E = 1.4426950408889634

COL_Q, COL_K, COL_V, COL_ZNA = 0, 1024, 2048, 3072
COL_ZSSM, COL_X, COL_B, COL_C = 4096, 6144, 8192, 9216
COL_GNA, COL_GSSM, NP_COLS = 10240, 11264, 12288
CCOL_K, CCOL_V, CCOL_X, CCOL_B, NPC_COLS = 0, 1024, 2048, 4096, 6144

Q_ROWS = 4
Q_BLK = Q_ROWS * GRID_W
KEY_ROWS = 12
KEY_BLK = KEY_ROWS * GRID_W
N_BIAS_CASES = 5
BIAS_CASE_BLOCK = (0, 1, 2, 6, 7)

VMEM_LIMIT = 56 * 1024 * 1024


def _sigmoid(v):
    return 1.0 / (1.0 + jnp.exp(-v))


def _dot(a, b):
    return jnp.dot(a, b, preferred_element_type=F32)


def _dot_nt(a, b):
    return lax.dot_general(a, b, (((1,), (1,)), ((), ())), preferred_element_type=F32)


def _adaln_kernel(cond_ref, w_ref, b_ref, o_ref):
    cnd = cond_ref[...]
    act = cnd * _sigmoid(cnd)
    o_ref[...] = jnp.dot(act, w_ref[...], preferred_element_type=F32,
                         precision=lax.Precision.HIGHEST) + b_ref[...]


def _adaln(cond, w_mod, b_mod):
    rows, d = cond.shape
    n = w_mod.shape[1]
    tn = 1024
    return pl.pallas_call(
        _adaln_kernel,
        grid=(n // tn,),
        in_specs=[pl.BlockSpec((rows, d), lambda j: (0, 0)),
                  pl.BlockSpec((d, tn), lambda j: (0, j)),
                  pl.BlockSpec((1, tn), lambda j: (0, j))],
        out_specs=pl.BlockSpec((rows, tn), lambda j: (0, j)),
        out_shape=jax.ShapeDtypeStruct((rows, n), F32),
        name="adaln",
    )(cond, w_mod, b_mod.reshape(1, n))


W_TILE = 512
DT_PAD = 128


def _wprep_kernel(a_ref, b_ref, dt_ref, w_ref, wdt_ref, *, n_aligned, shift, q_tiles, q_scale):
    j = pl.program_id(0)

    @pl.when(j < n_aligned)
    def _():
        scale = jnp.where(j < q_tiles, q_scale, 1.0)
        w_ref[...] = (a_ref[...] * scale).astype(BF16)

    @pl.when(j >= n_aligned)
    def _():
        cat = jnp.concatenate([a_ref[...], b_ref[:, 0:shift]], axis=1)
        w_ref[...] = cat[:, shift:].astype(BF16)

    @pl.when(j == 0)
    def _():
        dst = lax.broadcasted_iota(jnp.int32, (DT_PAD, DT_PAD), 1)
        src = lax.broadcasted_iota(jnp.int32, (DT_PAD, DT_PAD), 0)
        per_group = N_DIRS * HEADS_PER_GROUP
        grp, rem = dst // per_group, dst % per_group
        want = (rem // HEADS_PER_GROUP) * SSM_HEADS + grp * HEADS_PER_GROUP + rem % HEADS_PER_GROUP
        perm = jnp.where((src == want) & (dst < N_DIRS * SSM_HEADS), 1.0, 0.0).astype(BF16)
        wdt_ref[...] = _dot(dt_ref[...].astype(BF16), perm).astype(BF16)


def _prep_w_in(w_in, col_dt, n_dt, q_cols, q_scale):
    _, d, p = w_in.shape
    n_out = p - n_dt
    assert col_dt % W_TILE == 0 and n_out % W_TILE == 0 and q_cols % W_TILE == 0 and n_dt < 128
    lane_blocks = W_TILE // 128
    return pl.pallas_call(
        functools.partial(_wprep_kernel, n_aligned=col_dt // W_TILE, shift=n_dt, q_tiles=q_cols // W_TILE,
                          q_scale=q_scale),
        grid=(n_out // W_TILE,),
        in_specs=[pl.BlockSpec((None, d, W_TILE), lambda j: (0, 0, j)),
                  pl.BlockSpec((None, d, 128), lambda j: (0, 0, (j + 1) * lane_blocks)),
                  pl.BlockSpec((None, d, DT_PAD), lambda j: (0, 0, col_dt // DT_PAD))],
        out_specs=[pl.BlockSpec((d, W_TILE), lambda j: (0, j)),
                   pl.BlockSpec((d, DT_PAD), lambda j: (0, 0))],
        out_shape=[jax.ShapeDtypeStruct((d, n_out), BF16), jax.ShapeDtypeStruct((d, DT_PAD), BF16)],
        name="w_in_prep",
    )(w_in, w_in, w_in)


def _proj_kernel(x_ref, shift_ref, scale_ref, g_ref, w_ref, wdt_ref, p_ref, dtt_ref, *, chunks):
    x = x_ref[...]
    ms = jnp.mean(x * x, axis=-1, keepdims=True)
    h = x * lax.rsqrt(ms + EPS) * g_ref[...]
    h = h * (1.0 + scale_ref[...]) + shift_ref[...]
    hb = h.astype(BF16)
    for src, dst, width in chunks:
        p_ref[:, dst:dst + width] = _dot(hb, w_ref[:, src:src + width]).astype(BF16)
    dt = _dot(hb, wdt_ref[...])
    dtt_ref[...] = dt.T[:N_DIRS * SSM_HEADS]


def _projection(x3, shift, scale, g_pre, w_main, w_dt, col_ranges, tm):
    b, l, d = x3.shape
    chunks, dst = [], 0
    cw = 512
    for lo, hi in col_ranges:
        for src in range(lo, hi, cw):
            chunks.append((src, dst, cw))
            dst += cw
    n_out = dst
    n_dt = N_DIRS * SSM_HEADS
    return pl.pallas_call(
        functools.partial(_proj_kernel, chunks=tuple(chunks)),
        grid=(b, l // tm),
        in_specs=[pl.BlockSpec((None, tm, d), lambda i, j: (i, j, 0)),
                  pl.BlockSpec((None, 1, d), lambda i, j: (i, 0, 0)),
                  pl.BlockSpec((None, 1, d), lambda i, j: (i, 0, 0)),
                  pl.BlockSpec((1, d), lambda i, j: (0, 0)),
                  pl.BlockSpec(w_main.shape, lambda i, j: (0, 0), pipeline_mode=pl.Buffered(1)),
                  pl.BlockSpec(w_dt.shape, lambda i, j: (0, 0), pipeline_mode=pl.Buffered(1))],
        out_specs=[pl.BlockSpec((None, tm, n_out), lambda i, j: (i, j, 0)),
                   pl.BlockSpec((None, n_dt, tm), lambda i, j: (i, 0, j))],
        out_shape=[jax.ShapeDtypeStruct((b, l, n_out), BF16),
                   jax.ShapeDtypeStruct((b, n_dt, l), F32)],
        compiler_params=pltpu.CompilerParams(
            dimension_semantics=("parallel", "parallel"), vmem_limit_bytes=VMEM_LIMIT),
        name="projection",
    )(x3, shift, scale, g_pre, w_main, w_dt)


def _bias_block_plan():
    rows = 2048 // GRID_W
    plan = np.full((N_BIAS_CASES, Q_ROWS, KEY_ROWS), -1, np.int64)
    for case, blk in enumerate(BIAS_CASE_BLOCK):
        u0 = int(np.clip(Q_ROWS * blk - WIN_ROWS // 2, 0, rows - KEY_ROWS))
        for rho in range(Q_ROWS):
            r = Q_ROWS * blk + rho
            r0 = int(np.clip(r - WIN_ROWS // 2, 0, rows - WIN_ROWS))
            for a in range(KEY_ROWS):
                krow = u0 + a
                if r0 <= krow < r0 + WIN_ROWS:
                    plan[case, rho, a] = krow - r + WIN_ROWS - 1
    return plan


def _bias_kernel(rpb_ref, o_ref, toep_ref, *, plan):
    head = pl.program_id(0)
    n_dr = 2 * WIN_ROWS - 1
    n_dc = 2 * WIN_COLS - 1
    qc = lax.broadcasted_iota(jnp.int32, (GRID_W, GRID_W), 0)
    kc = lax.broadcasted_iota(jnp.int32, (GRID_W, GRID_W), 1)
    diff = kc - qc + (WIN_COLS - 1)
    c0 = jnp.clip(qc - WIN_COLS // 2, 0, GRID_W - WIN_COLS)
    in_win = (kc >= c0) & (kc < c0 + WIN_COLS)
    base = head * (n_dr * n_dc)
    for dr in range(n_dr):
        blk = jnp.zeros((GRID_W, GRID_W), F32)
        for j in range(n_dc):
            blk = jnp.where(diff == j, rpb_ref[base + dr * n_dc + j] * LOG2E, blk)
        toep_ref[dr] = jnp.where(in_win, blk, NEG)
    neg = jnp.full((GRID_W, GRID_W), NEG, F32)
    for case in range(N_BIAS_CASES):
        for rho in range(Q_ROWS):
            for a in range(KEY_ROWS):
                dr = int(plan[case, rho, a])
                val = neg if dr < 0 else toep_ref[dr]
                o_ref[case, rho * GRID_W:(rho + 1) * GRID_W, a * GRID_W:(a + 1) * GRID_W] = val.astype(BF16)


def _bias_table(rpb):
    plan = _bias_block_plan()
    return pl.pallas_call(
        functools.partial(_bias_kernel, plan=plan),
        grid=(NA_HEADS,),
        in_specs=[pl.BlockSpec(memory_space=pltpu.SMEM)],
        out_specs=pl.BlockSpec((N_BIAS_CASES, None, Q_BLK, KEY_BLK), lambda h: (0, h, 0, 0)),
        out_shape=jax.ShapeDtypeStruct((N_BIAS_CASES, NA_HEADS, Q_BLK, KEY_BLK), BF16),
        scratch_shapes=[pltpu.VMEM((2 * WIN_ROWS - 1, GRID_W, GRID_W), F32)],
        name="bias_table",
    )(rpb.reshape(-1))


def _na_kernel(q_ref, k0_ref, k1_ref, k2_ref, kc_ref, v0_ref, v1_ref, v2_ref, vc_ref, z_ref, bias_ref, o_ref):
    lane = lax.broadcasted_iota(jnp.int32, (1, 2 * NA_HEAD_DIM), 1)
    k_refs = (k0_ref, k1_ref, k2_ref, kc_ref)
    v_refs = (v0_ref, v1_ref, v2_ref, vc_ref)
    lane_full = lax.broadcasted_iota(jnp.int32, (Q_BLK, 2 * NA_HEAD_DIM), 1)
    sum_lanes = (NA_HEAD_DIM, 0)
    fills = [jnp.where(lane_full == sl, 1.0, 0.0).astype(BF16) for sl in sum_lanes]
    for pair in range(NA_HEADS // 2):
        cs = slice(pair * 128, (pair + 1) * 128)
        qp = q_ref[:, cs]
        kt = [r[:, cs] for r in k_refs]
        vt = [r[:, cs] for r in v_refs]
        q_stack = jnp.concatenate([jnp.where(lane < NA_HEAD_DIM, qp, jnp.zeros_like(qp)),
                                   jnp.where(lane >= NA_HEAD_DIM, qp, jnp.zeros_like(qp))], axis=0)
        s_both = [_dot_nt(q_stack, kk) for kk in kt]
        outs = []
        for hh in range(2):
            head_lanes = (lane < NA_HEAD_DIM) if hh == 0 else (lane >= NA_HEAD_DIM)
            sum_lane = sum_lanes[hh]
            s = [sb[hh * Q_BLK:(hh + 1) * Q_BLK] for sb in s_both]
            for t in range(3):
                s[t] = s[t] + bias_ref[2 * pair + hh, :, t * 256:(t + 1) * 256].astype(F32)
            mx = jnp.max(jnp.maximum(jnp.maximum(s[0], s[1]), jnp.maximum(s[2], s[3])), axis=-1, keepdims=True)
            o = None
            for t in range(4):
                e = jnp.exp2((s[t] - mx).astype(BF16))
                c = _dot(e, jnp.where(head_lanes, vt[t], fills[hh]))
                o = c if o is None else o + c
            outs.append(o * (1.0 / o[:, sum_lane:sum_lane + 1]))
        acc = jnp.where(lane < NA_HEAD_DIM, outs[0], outs[1])
        z = z_ref[:, cs].astype(F32)
        o_ref[:, cs] = (acc * (z * _sigmoid(z))).astype(BF16)


def _neighborhood_attention(p3, pc3, bias_tab):
    b, s, _ = p3.shape
    n_blk = s // Q_BLK

    def key_blk(i):
        return jnp.clip(i - 1, 0, n_blk - KEY_ROWS // Q_ROWS)

    def case_of(i):
        return jnp.minimum(i, 2) + jnp.maximum(i - (n_blk - 3), 0)

    cq, ck, cv, cz = COL_Q // 1024, COL_K // 1024, COL_V // 1024, COL_ZNA // 1024
    blk = (None, Q_BLK, 1024)
    in_specs = [pl.BlockSpec(blk, lambda i, j: (j, i, cq))]
    in_specs += [pl.BlockSpec(blk, functools.partial(lambda i, j, t: (j, key_blk(i) + t, ck), t=t)) for t in range(3)]
    in_specs += [pl.BlockSpec(blk, lambda i, j: (j, 0, CCOL_K // 1024))]
    in_specs += [pl.BlockSpec(blk, functools.partial(lambda i, j, t: (j, key_blk(i) + t, cv), t=t)) for t in range(3)]
    in_specs += [pl.BlockSpec(blk, lambda i, j: (j, 0, CCOL_V // 1024))]
    in_specs += [pl.BlockSpec(blk, lambda i, j: (j, i, cz))]
    in_specs += [pl.BlockSpec((None, NA_HEADS, Q_BLK, KEY_BLK), lambda i, j: (case_of(i), 0, 0, 0))]
    return pl.pallas_call(
        _na_kernel,
        grid=(n_blk, b),
        in_specs=in_specs,
        out_specs=pl.BlockSpec(blk, lambda i, j: (j, i, 0)),
        out_shape=jax.ShapeDtypeStruct((b, s, NA_HEADS * NA_HEAD_DIM), BF16),
        compiler_params=pltpu.CompilerParams(
            dimension_semantics=("parallel", "parallel"), vmem_limit_bytes=VMEM_LIMIT),
        name="neighborhood_attention",
    )(p3, p3, p3, p3, pc3, p3, p3, p3, pc3, p3, bias_tab)


def _softplus(v):
    return jnp.maximum(v, 0.0) + jnp.log(1.0 + jnp.exp(-jnp.abs(v)))


def _head_stack(v, head_lanes):
    return jnp.concatenate([jnp.where(m, v, jnp.zeros_like(v)) for m in head_lanes], axis=0)


def _ssd_kernel(xs_ref, bm_ref, cm_ref, z_ref, xsc_ref, bmc_ref, dtt_ref, dttc_ref,
                cwx_ref, cwb_ref, cwc_ref, cbx_ref, cbb_ref, cbc_ref,
                alog_ref, dtb_ref, skip_ref, ng_ref, o_ref,
                pad_ref, padc_ref, tap_ref, ux_ref, ubt_ref, uc_ref, uxc_ref, ubtc_ref,
                rows_ref, cols_ref, dec_ref, rowsc_ref, decc_ref, y_ref, st_ref, e_ref):
    q = SSM_CHUNK
    hpg = HEADS_PER_GROUP
    nh = N_DIRS * hpg
    s_len = xs_ref.shape[0]
    c_len = xsc_ref.shape[0]
    n_lat = s_len // q
    n_ctx = c_len // q
    halo = CONV_HALO
    blk = CONV_BLK
    win = blk + 2 * halo

    zeros_halo = jnp.zeros((halo, pad_ref.shape[1]), BF16)
    pad_ref[0:halo, :] = zeros_halo
    pad_ref[s_len + halo:s_len + 2 * halo, :] = zeros_halo
    pad_ref[halo:s_len + halo, 0:GROUP_X] = xs_ref[...]
    pad_ref[halo:s_len + halo, GROUP_X:GROUP_X + SSM_STATE] = bm_ref[...]
    pad_ref[halo:s_len + halo, GROUP_X + SSM_STATE:] = cm_ref[...]
    zeros_halo_c = jnp.zeros((halo, padc_ref.shape[1]), BF16)
    padc_ref[0:halo, :] = zeros_halo_c
    padc_ref[c_len + halo:c_len + 2 * halo, :] = zeros_halo_c
    padc_ref[halo:c_len + halo, 0:GROUP_X] = xsc_ref[...]
    padc_ref[halo:c_len + halo, GROUP_X:] = bmc_ref[...]

    cw = jnp.concatenate([cwx_ref[...], cwb_ref[...], cwc_ref[...]], axis=1)
    cb = jnp.concatenate([cbx_ref[...], cbb_ref[...], cbc_ref[...]], axis=1)
    for k in range(SSM_CONV):
        tap_ref[k] = jnp.broadcast_to(cw[k:k + 1, :], (win, cw.shape[1])).astype(BF16)
    off = lax.broadcasted_iota(jnp.int32, (blk, SSM_CONV * win), 1) - lax.broadcasted_iota(
        jnp.int32, (blk, SSM_CONV * win), 0)
    hit = off == halo - SSM_CONV // 2
    for k in range(1, SSM_CONV):
        hit = hit | (off == k * win + halo - SSM_CONV // 2 + k)
    shift = jnp.where(hit, 1.0, 0.0).astype(BF16)

    def conv_block(src_ref, start, width):
        w = src_ref[pl.ds(start, win), :]
        stack = jnp.concatenate([w * tap_ref[k, :, 0:width] for k in range(SSM_CONV)], axis=0)
        acc = _dot(shift, stack) + cb[:, :width]
        return acc * _sigmoid(acc)

    def conv_chunk(src_ref, start, width):
        return jnp.concatenate([conv_block(src_ref, start + i * blk, width) for i in range(q // blk)], axis=0)

    def conv_lat(c, carry):
        start = pl.multiple_of(c * q, q)
        u = conv_chunk(pad_ref, start, GROUP_X + 2 * SSM_STATE)
        ux_ref[pl.ds(start, q), :] = u[:, :GROUP_X]
        ubt_ref[c] = u[:, GROUP_X:GROUP_X + SSM_STATE].T.astype(BF16)
        uc_ref[pl.ds(start, q), :] = u[:, GROUP_X + SSM_STATE:].astype(BF16)
        return carry

    lax.fori_loop(0, n_lat, conv_lat, 0, unroll=4)
    for c in range(n_ctx):
        u = conv_chunk(padc_ref, c * q, GROUP_X + SSM_STATE)
        uxc_ref[c * q:(c + 1) * q, :] = u[:, :GROUP_X]
        ubtc_ref[c] = u[:, GROUP_X:].T.astype(BF16)

    ki = lax.broadcasted_iota(jnp.int32, (q, q), 0)
    ii = lax.broadcasted_iota(jnp.int32, (q, q), 1)
    tri_f = (ki <= ii).astype(F32)
    tri_b = (ki >= ii).astype(F32)
    lane_x = lax.broadcasted_iota(jnp.int32, (1, GROUP_X), 1)
    head_lanes = [(lane_x >= r * SSM_HEAD_DIM) & (lane_x < (r + 1) * SSM_HEAD_DIM) for r in range(hpg)]
    neg_a2 = -jnp.exp(alog_ref[...]) * LOG2E
    dt_bias = dtb_ref[...]

    def dt_forms(raw, n, rows_out, cols_out, dec_out):
        stack = lambda v: jnp.concatenate([v[:, c * q:(c + 1) * q] for c in range(n)], axis=0)
        dt_all = _softplus(raw + dt_bias)
        dt = stack(dt_all)
        a = stack(neg_a2 * dt_all)
        is_fwd = lax.broadcasted_iota(jnp.int32, (n * nh, 1), 0) % nh < hpg
        cum_f = jnp.dot(a, tri_f, preferred_element_type=F32, precision=lax.Precision.HIGHEST)
        cum_b = jnp.dot(a, tri_b, preferred_element_type=F32, precision=lax.Precision.HIGHEST)
        acum = jnp.where(is_fwd, cum_f, cum_b)
        atot = jnp.where(is_fwd, acum[:, q - 1:q], acum[:, 0:1])
        sw = dt * jnp.exp2(atot - acum)
        dec = jnp.exp2(atot)
        for c in range(n):
            sl = slice(c * nh, (c + 1) * nh)
            rows_out[c] = jnp.concatenate([dt[sl], acum[sl], sw[sl]], axis=0)
            if cols_out is not None:
                cols_out[c] = acum[sl].T
            dec_rows = []
            for d in range(N_DIRS):
                base = c * nh + d * hpg
                row = jnp.broadcast_to(dec[base + hpg - 1:base + hpg, :], (1, GROUP_X))
                for r in range(hpg - 2, -1, -1):
                    row = jnp.where(lane_x < (r + 1) * SSM_HEAD_DIM, dec[base + r:base + r + 1, :], row)
                dec_rows.append(row)
            dec_out[c] = jnp.concatenate(dec_rows, axis=0)

    dt_forms(dtt_ref[...], n_lat, rows_ref, cols_ref, dec_ref)
    dt_forms(dttc_ref[...], n_ctx, rowsc_ref, None, decc_ref)

    def state_term(d, rows, btf, xm):
        parts = [(btf * rows[2 * N_DIRS * hpg + d * hpg + r:2 * N_DIRS * hpg + d * hpg + r + 1, :]).astype(BF16)
                 for r in range(hpg)]
        return _dot(jnp.concatenate(parts, axis=1), xm)

    h_init = []
    for d in range(N_DIRS):
        h = jnp.zeros((SSM_STATE, GROUP_X), F32)
        for c in (range(n_ctx) if d == 0 else reversed(range(n_ctx))):
            xm = _head_stack(uxc_ref[c * q:(c + 1) * q, :].astype(BF16), head_lanes)
            h = decc_ref[c][d:d + 1, :] * h + state_term(d, rowsc_ref[c], ubtc_ref[c].astype(F32), xm)
        h_init.append(h)

    skip = skip_ref[...]
    norm_g = ng_ref[...]
    keep = (ki >= ii, ki <= ii)
    low_lanes = lax.broadcasted_iota(jnp.int32, (1, q), 1) < SSM_HEAD_DIM

    def chunk_body(c, carry):
        start = pl.multiple_of(c * q, q)
        rows = rows_ref[c]
        cols = cols_ref[c]
        xs = ux_ref[pl.ds(start, q), :]
        xm = _head_stack(xs.astype(BF16), head_lanes)
        cmat = uc_ref[pl.ds(start, q), :]
        bt = ubt_ref[c]
        cbm = _dot(cmat, bt)
        btf = bt.astype(F32)
        m_parts = []
        acols = [[], []]
        for r in range(hpg):
            t = None
            for d in range(N_DIRS):
                k = d * hpg + r
                acol = jnp.broadcast_to(cols[:, k:k + 1], (q, q))
                arow = rows[N_DIRS * hpg + k:N_DIRS * hpg + k + 1, :]
                term = jnp.where(keep[d], jnp.exp2(acol - arow) * rows[k:k + 1, :], 0.0)
                t = term if t is None else t + term
                acols[d].append(acol)
            m_parts.append((cbm * t).astype(BF16))
        y_ref[pl.ds(start, q), :] = xs * skip + _dot(jnp.concatenate(m_parts, axis=1), xm)
        for d in range(N_DIRS):
            st_ref[d, c] = state_term(d, rows, btf, xm)
            halves = [jnp.where(low_lanes, acols[d][2 * i], acols[d][2 * i + 1]) for i in range(hpg // 2)]
            e_ref[d, c] = jnp.exp2(jnp.concatenate(halves, axis=1))
        return carry

    lax.fori_loop(0, n_lat, chunk_body, 0, unroll=4)

    def scan_body(j, carry, finalize):
        new = []
        chunk_ids = (j, n_lat - 1 - j)
        for d in range(N_DIRS):
            h = carry[d]
            c = chunk_ids[d]
            start = pl.multiple_of(c * q, q)
            y_off = _dot(uc_ref[pl.ds(start, q), :], h.astype(BF16)) * e_ref[d, c]
            y_ref[pl.ds(start, q), :] = y_ref[pl.ds(start, q), :] + y_off
            new.append(dec_ref[c][d:d + 1, :] * h + st_ref[d, c])
        if finalize:
            for c in chunk_ids:
                start = pl.multiple_of(c * q, q)
                z = z_ref[pl.ds(start, q), :].astype(F32)
                u = y_ref[pl.ds(start, q), :] * (z * _sigmoid(z))
                u = u * lax.rsqrt(jnp.mean(u * u, axis=-1, keepdims=True) + EPS)
                o_ref[pl.ds(start, q), :] = (u * norm_g).astype(BF16)
        return tuple(new)

    half = n_lat // 2
    carry = lax.fori_loop(0, half, functools.partial(scan_body, finalize=False), tuple(h_init), unroll=2)
    lax.fori_loop(half, n_lat, functools.partial(scan_body, finalize=True), carry, unroll=2)


def _ssd_mixer(p3, pc3, dtt, dttc, conv_w, conv_b, alog8, dtb8, skip_exp, norm_g):
    b, s, _ = p3.shape
    c_len = pc3.shape[1]
    g = SSM_GROUPS
    n_lat, n_ctx = s // SSM_CHUNK, c_len // SSM_CHUNK
    assert n_lat % 2 == 0
    nx = GROUP_X
    ns = SSM_STATE
    nh = N_DIRS * HEADS_PER_GROUP
    in_specs = [
        pl.BlockSpec((None, s, nx), lambda i, j: (i, 0, COL_X // nx + j)),
        pl.BlockSpec((None, s, ns), lambda i, j: (i, 0, COL_B // ns + j)),
        pl.BlockSpec((None, s, ns), lambda i, j: (i, 0, COL_C // ns + j)),
        pl.BlockSpec((None, s, nx), lambda i, j: (i, 0, COL_ZSSM // nx + j)),
        pl.BlockSpec((None, c_len, nx), lambda i, j: (i, 0, CCOL_X // nx + j)),
        pl.BlockSpec((None, c_len, ns), lambda i, j: (i, 0, CCOL_B // ns + j)),
        pl.BlockSpec((None, nh, s), lambda i, j: (i, j, 0)),
        pl.BlockSpec((None, nh, c_len), lambda i, j: (i, j, 0)),
        pl.BlockSpec((SSM_CONV, nx), lambda i, j: (0, j)),
        pl.BlockSpec((SSM_CONV, ns), lambda i, j: (0, SSM_WIDTH // ns + j)),
        pl.BlockSpec((SSM_CONV, ns), lambda i, j: (0, SSM_WIDTH // ns + g + j)),
        pl.BlockSpec((1, nx), lambda i, j: (0, j)),
        pl.BlockSpec((1, ns), lambda i, j: (0, SSM_WIDTH // ns + j)),
        pl.BlockSpec((1, ns), lambda i, j: (0, SSM_WIDTH // ns + g + j)),
        pl.BlockSpec((None, nh, 1), lambda i, j: (j, 0, 0)),
        pl.BlockSpec((None, nh, 1), lambda i, j: (j, 0, 0)),
        pl.BlockSpec((None, 1, nx), lambda i, j: (j, 0, 0)),
        pl.BlockSpec((1, nx), lambda i, j: (0, j)),
    ]
    halo = CONV_HALO
    scratch = [
        pltpu.VMEM((s + 2 * halo, nx + 2 * ns), BF16),
        pltpu.VMEM((c_len + 2 * halo, nx + ns), BF16),
        pltpu.VMEM((SSM_CONV, CONV_BLK + 2 * halo, nx + 2 * ns), BF16),
        pltpu.VMEM((s, nx), F32),
        pltpu.VMEM((n_lat, ns, SSM_CHUNK), BF16),
        pltpu.VMEM((s, ns), BF16),
        pltpu.VMEM((c_len, nx), F32),
        pltpu.VMEM((n_ctx, ns, SSM_CHUNK), BF16),
        pltpu.VMEM((n_lat, 3 * nh, SSM_CHUNK), F32),
        pltpu.VMEM((n_lat, SSM_CHUNK, nh), F32),
        pltpu.VMEM((n_lat, N_DIRS, nx), F32),
        pltpu.VMEM((n_ctx, 3 * nh, SSM_CHUNK), F32),
        pltpu.VMEM((n_ctx, N_DIRS, nx), F32),
        pltpu.VMEM((s, nx), F32),
        pltpu.VMEM((N_DIRS, n_lat, ns, nx), F32),
        pltpu.VMEM((N_DIRS, n_lat, SSM_CHUNK, nx), F32),
    ]
    return pl.pallas_call(
        _ssd_kernel,
        grid=(b, g),
        in_specs=in_specs,
        out_specs=pl.BlockSpec((None, s, nx), lambda i, j: (i, 0, j)),
        out_shape=jax.ShapeDtypeStruct((b, s, SSM_WIDTH), BF16),
        scratch_shapes=scratch,
        compiler_params=pltpu.CompilerParams(
            dimension_semantics=("parallel", "parallel"), vmem_limit_bytes=VMEM_LIMIT),
        name="ssd_mixer",
    )(p3, p3, p3, p3, pc3, pc3, dtt, dttc, conv_w, conv_w, conv_w, conv_b, conv_b, conv_b,
      alog8, dtb8, skip_exp, norm_g)


def _merge_kernel(yna_ref, yssm_ref, gna_ref, gssm_ref, x_ref, gate_ref, wna_ref, wssm_ref, wout_ref, gpost_ref, o_ref):
    a = _dot(yna_ref[...], wna_ref[...])
    s = _dot(yssm_ref[...], wssm_ref[...])
    m = _sigmoid(gna_ref[...].astype(F32)) * a + _sigmoid(gssm_ref[...].astype(F32)) * s
    o = _dot(m.astype(BF16), wout_ref[...])
    r = o * lax.rsqrt(jnp.mean(o * o, axis=-1, keepdims=True) + EPS) * gpost_ref[...]
    o_ref[...] = x_ref[...] + gate_ref[...] * r


def _merge(y_na, y_ssm, p3, x3, gate, w_na, w_ssm, w_out, g_post, tm):
    b, s, d = x3.shape
    const = lambda i, j: (0, 0)
    return pl.pallas_call(
        _merge_kernel,
        grid=(b, s // tm),
        in_specs=[pl.BlockSpec((None, tm, y_na.shape[-1]), lambda i, j: (i, j, 0)),
                  pl.BlockSpec((None, tm, y_ssm.shape[-1]), lambda i, j: (i, j, 0)),
                  pl.BlockSpec((None, tm, d), lambda i, j: (i, j, COL_GNA // d)),
                  pl.BlockSpec((None, tm, d), lambda i, j: (i, j, COL_GSSM // d)),
                  pl.BlockSpec((None, tm, d), lambda i, j: (i, j, 0)),
                  pl.BlockSpec((None, 1, d), lambda i, j: (i, 0, 0)),
                  pl.BlockSpec(w_na.shape, const),
                  pl.BlockSpec(w_ssm.shape, const),
                  pl.BlockSpec(w_out.shape, const),
                  pl.BlockSpec((1, d), const)],
        out_specs=pl.BlockSpec((None, tm, d), lambda i, j: (i, j, 0)),
        out_shape=jax.ShapeDtypeStruct((b, s, d), x3.dtype),
        compiler_params=pltpu.CompilerParams(
            dimension_semantics=("parallel", "parallel"), vmem_limit_bytes=VMEM_LIMIT),
        name="merge_out",
    )(y_na, y_ssm, p3, p3, x3, gate, w_na, w_ssm, w_out, g_post)


def kernel(x, c, ctx, c_ctx, w_mod, b_mod, g_pre, g_post, w_in, conv_w, conv_b, a_log, dt_bias, d_skip,
           ssm_norm_g, rpb, w_na_out, w_ssm_out, w_out):
    assert w_mod.shape[0] == 1, "single-layer block"
    b, s, d = x.shape
    g, hpg = SSM_GROUPS, HEADS_PER_GROUP

    w_main, w_dt = _prep_w_in(w_in, COL_GNA, N_DIRS * SSM_HEADS, COL_K, NA_HEAD_DIM ** -0.5 * LOG2E)
    per_group = lambda p: jnp.transpose(p.reshape(N_DIRS, g, hpg), (1, 0, 2)).reshape(g, N_DIRS * hpg, 1)
    alog8 = per_group(a_log[0])
    dtb8 = per_group(dt_bias[0])
    skip_exp = jnp.repeat(d_skip[0], SSM_HEAD_DIM).reshape(g, 1, GROUP_X)

    cond = jnp.zeros((16, d), F32).at[:b].set(c).at[b].set(c_ctx)
    mod = _adaln(cond, w_mod[0], b_mod[0])
    shift_l, scale_l, gate_l = (mod[:b, k * d:(k + 1) * d].reshape(b, 1, d) for k in range(3))
    shift_c, scale_c = (jnp.broadcast_to(mod[b, k * d:(k + 1) * d].reshape(1, 1, d), (b, 1, d)) for k in range(2))

    p3, dtt = _projection(x, shift_l, scale_l, g_pre, w_main, w_dt, ((0, NP_COLS),), tm=256)
    pc3, dttc = _projection(ctx, shift_c, scale_c, g_pre, w_main, w_dt,
                            ((COL_K, COL_ZNA), (COL_X, COL_GNA)), tm=256)

    bias_tab = _bias_table(rpb[0])
    y_na = _neighborhood_attention(p3, pc3, bias_tab)

    y_ssm = _ssd_mixer(p3, pc3, dtt, dttc, conv_w[0], conv_b[0].reshape(1, -1), alog8, dtb8, skip_exp,
                       ssm_norm_g)

    return _merge(y_na, y_ssm, p3, x, gate_l, w_na_out[0].astype(BF16), w_ssm_out[0].astype(BF16),
                  w_out[0].astype(BF16), g_post, tm=512)
```

```python
import functools

import numpy as np
import jax
import jax.numpy as jnp
from jax import lax
from jax.experimental import pallas as pl
from jax.experimental.pallas import tpu as pltpu

F32 = jnp.float32
BF16 = jnp.bfloat16

D_MODEL = 1024
GRID_W = 64
NA_HEADS = 16
NA_HEAD_DIM = 64
WIN_ROWS = 8
WIN_COLS = 16
SSM_WIDTH = 2 * D_MODEL
SSM_HEAD_DIM = 64
SSM_HEADS = SSM_WIDTH // SSM_HEAD_DIM
SSM_GROUPS = 8
HEADS_PER_GROUP = SSM_HEADS // SSM_GROUPS
SSM_STATE = 128
SSM_CONV = 5
SSM_CHUNK = 128
CONV_HALO = 16
CONV_BLK = 64
N_DIRS = 2
GROUP_X = HEADS_PER_GROUP * SSM_HEAD_DIM
EPS = 1e-6
NEG = -1e30
LOG2E = 1.4426950408889634

COL_Q, COL_K, COL_V, COL_ZNA = 0, 1024, 2048, 3072
COL_ZSSM, COL_X, COL_B, COL_C = 4096, 6144, 8192, 9216
COL_GNA, COL_GSSM, NP_COLS = 10240, 11264, 12288
CCOL_K, CCOL_V, CCOL_X, CCOL_B, NPC_COLS = 0, 1024, 2048, 4096, 6144

Q_ROWS = 4
Q_BLK = Q_ROWS * GRID_W
KEY_ROWS = 12
KEY_BLK = KEY_ROWS * GRID_W
N_BIAS_CASES = 5
BIAS_CASE_BLOCK = (0, 1, 2, 6, 7)

VMEM_LIMIT = 56 * 1024 * 1024


def _sigmoid(v):
    return 1.0 / (1.0 + jnp.exp(-v))


def _dot(a, b):
    return jnp.dot(a, b, preferred_element_type=F32)


def _dot_nt(a, b):
    return lax.dot_general(a, b, (((1,), (1,)), ((), ())), preferred_element_type=F32)


def _adaln_kernel(cond_ref, w_ref, b_ref, o_ref):
    cnd = cond_ref[...]
    act = cnd * _sigmoid(cnd)
    o_ref[...] = jnp.dot(act, w_ref[...], preferred_element_type=F32,
                         precision=lax.Precision.HIGHEST) + b_ref[...]


def _adaln(cond, w_mod, b_mod):
    rows, d = cond.shape
    n = w_mod.shape[1]
    tn = 1024
    return pl.pallas_call(
        _adaln_kernel,
        grid=(n // tn,),
        in_specs=[pl.BlockSpec((rows, d), lambda j: (0, 0)),
                  pl.BlockSpec((d, tn), lambda j: (0, j)),
                  pl.BlockSpec((1, tn), lambda j: (0, j))],
        out_specs=pl.BlockSpec((rows, tn), lambda j: (0, j)),
        out_shape=jax.ShapeDtypeStruct((rows, n), F32),
        name="adaln",
    )(cond, w_mod, b_mod.reshape(1, n))


W_TILE = 512
DT_PAD = 128


def _wprep_kernel(a_ref, b_ref, dt_ref, w_ref, wdt_ref, *, n_aligned, shift, q_tiles, q_scale):
    j = pl.program_id(0)

    @pl.when(j < n_aligned)
    def _():
        scale = jnp.where(j < q_tiles, q_scale, 1.0)
        w_ref[...] = (a_ref[...] * scale).T.astype(BF16)

    @pl.when(j >= n_aligned)
    def _():
        w_ref[...] = jnp.concatenate([a_ref[shift:, :], b_ref[...]], axis=0).T.astype(BF16)

    @pl.when(j == 0)
    def _():
        dst = lax.broadcasted_iota(jnp.int32, (DT_PAD, DT_PAD), 1)
        src = lax.broadcasted_iota(jnp.int32, (DT_PAD, DT_PAD), 0)
        per_group = N_DIRS * HEADS_PER_GROUP
        grp, rem = dst // per_group, dst % per_group
        want = (rem // HEADS_PER_GROUP) * SSM_HEADS + grp * HEADS_PER_GROUP + rem % HEADS_PER_GROUP
        perm = jnp.where((src == want) & (dst < N_DIRS * SSM_HEADS), 1.0, 0.0).astype(BF16)
        wdt_ref[...] = _dot(dt_ref[...].T.astype(BF16), perm).astype(BF16)


def _prep_w_in(w_in_t, col_dt, n_dt, q_cols, q_scale):
    p, d = w_in_t.shape
    n_out = p - n_dt
    assert col_dt % W_TILE == 0 and n_out % W_TILE == 0 and q_cols % W_TILE == 0
    assert W_TILE % n_dt == 0 and n_dt % 8 == 0 and n_dt <= DT_PAD and col_dt % DT_PAD == 0
    return pl.pallas_call(
        functools.partial(_wprep_kernel, n_aligned=col_dt // W_TILE, shift=n_dt, q_tiles=q_cols // W_TILE,
                          q_scale=q_scale),
        grid=(n_out // W_TILE,),
        in_specs=[pl.BlockSpec((W_TILE, d), lambda j: (j, 0)),
                  pl.BlockSpec((n_dt, d), lambda j: ((j + 1) * (W_TILE // n_dt), 0)),
                  pl.BlockSpec((DT_PAD, d), lambda j: (col_dt // DT_PAD, 0))],
        out_specs=[pl.BlockSpec((d, W_TILE), lambda j: (0, j)),
                   pl.BlockSpec((d, DT_PAD), lambda j: (0, 0))],
        out_shape=[jax.ShapeDtypeStruct((d, n_out), BF16), jax.ShapeDtypeStruct((d, DT_PAD), BF16)],
        name="w_in_prep",
    )(w_in_t, w_in_t, w_in_t)


def _proj_kernel(x_ref, shift_ref, scale_ref, g_ref, w_ref, wdt_ref, p_ref, dtt_ref, *, chunks):
    x = x_ref[...]
    ms = jnp.mean(x * x, axis=-1, keepdims=True)
    h = x * lax.rsqrt(ms + EPS) * g_ref[...]
    h = h * (1.0 + scale_ref[...]) + shift_ref[...]
    hb = h.astype(BF16)
    for src, dst, width in chunks:
        p_ref[:, dst:dst + width] = _dot(hb, w_ref[:, src:src + width]).astype(BF16)
    dt = _dot(hb, wdt_ref[...])
    dtt_ref[...] = dt.T[:N_DIRS * SSM_HEADS]


def _projection(x3, shift, scale, g_pre, w_main, w_dt, col_ranges, tm):
    b, l, d = x3.shape
    chunks, dst = [], 0
    cw = 512
    for lo, hi in col_ranges:
        for src in range(lo, hi, cw):
            chunks.append((src, dst, cw))
            dst += cw
    n_out = dst
    n_dt = N_DIRS * SSM_HEADS
    return pl.pallas_call(
        functools.partial(_proj_kernel, chunks=tuple(chunks)),
        grid=(b, l // tm),
        in_specs=[pl.BlockSpec((None, tm, d), lambda i, j: (i, j, 0)),
                  pl.BlockSpec((None, 1, d), lambda i, j: (i, 0, 0)),
                  pl.BlockSpec((None, 1, d), lambda i, j: (i, 0, 0)),
                  pl.BlockSpec((1, d), lambda i, j: (0, 0)),
                  pl.BlockSpec(w_main.shape, lambda i, j: (0, 0), pipeline_mode=pl.Buffered(1)),
                  pl.BlockSpec(w_dt.shape, lambda i, j: (0, 0), pipeline_mode=pl.Buffered(1))],
        out_specs=[pl.BlockSpec((None, tm, n_out), lambda i, j: (i, j, 0)),
                   pl.BlockSpec((None, n_dt, tm), lambda i, j: (i, 0, j))],
        out_shape=[jax.ShapeDtypeStruct((b, l, n_out), BF16),
                   jax.ShapeDtypeStruct((b, n_dt, l), F32)],
        compiler_params=pltpu.CompilerParams(
            dimension_semantics=("parallel", "parallel"), vmem_limit_bytes=VMEM_LIMIT),
        name="projection",
    )(x3, shift, scale, g_pre, w_main, w_dt)


def _bias_block_plan():
    rows = 2048 // GRID_W
    plan = np.full((N_BIAS_CASES, Q_ROWS, KEY_ROWS), -1, np.int64)
    for case, blk in enumerate(BIAS_CASE_BLOCK):
        u0 = int(np.clip(Q_ROWS * blk - WIN_ROWS // 2, 0, rows - KEY_ROWS))
        for rho in range(Q_ROWS):
            r = Q_ROWS * blk + rho
            r0 = int(np.clip(r - WIN_ROWS // 2, 0, rows - WIN_ROWS))
            for a in range(KEY_ROWS):
                krow = u0 + a
                if r0 <= krow < r0 + WIN_ROWS:
                    plan[case, rho, a] = krow - r + WIN_ROWS - 1
    return plan


def _bias_kernel(rpb_ref, o_ref, toep_ref, *, plan):
    head = pl.program_id(0)
    n_dr = 2 * WIN_ROWS - 1
    n_dc = 2 * WIN_COLS - 1
    qc = lax.broadcasted_iota(jnp.int32, (GRID_W, GRID_W), 0)
    kc = lax.broadcasted_iota(jnp.int32, (GRID_W, GRID_W), 1)
    diff = kc - qc + (WIN_COLS - 1)
    c0 = jnp.clip(qc - WIN_COLS // 2, 0, GRID_W - WIN_COLS)
    in_win = (kc >= c0) & (kc < c0 + WIN_COLS)
    base = head * (n_dr * n_dc)
    for dr in range(n_dr):
        blk = jnp.zeros((GRID_W, GRID_W), F32)
        for j in range(n_dc):
            blk = jnp.where(diff == j, rpb_ref[base + dr * n_dc + j] * LOG2E, blk)
        toep_ref[dr] = jnp.where(in_win, blk, NEG)
    neg = jnp.full((GRID_W, GRID_W), NEG, F32)
    for case in range(N_BIAS_CASES):
        for rho in range(Q_ROWS):
            for a in range(KEY_ROWS):
                dr = int(plan[case, rho, a])
                val = neg if dr < 0 else toep_ref[dr]
                o_ref[case, rho * GRID_W:(rho + 1) * GRID_W, a * GRID_W:(a + 1) * GRID_W] = val.astype(BF16)


def _bias_table(rpb):
    plan = _bias_block_plan()
    return pl.pallas_call(
        functools.partial(_bias_kernel, plan=plan),
        grid=(NA_HEADS,),
        in_specs=[pl.BlockSpec(memory_space=pltpu.SMEM)],
        out_specs=pl.BlockSpec((N_BIAS_CASES, None, Q_BLK, KEY_BLK), lambda h: (0, h, 0, 0)),
        out_shape=jax.ShapeDtypeStruct((N_BIAS_CASES, NA_HEADS, Q_BLK, KEY_BLK), BF16),
        scratch_shapes=[pltpu.VMEM((2 * WIN_ROWS - 1, GRID_W, GRID_W), F32)],
        name="bias_table",
    )(rpb.reshape(-1))


def _na_kernel(q_ref, k0_ref, k1_ref, k2_ref, kc_ref, v0_ref, v1_ref, v2_ref, vc_ref, z_ref, bias_ref, o_ref):
    lane = lax.broadcasted_iota(jnp.int32, (1, 2 * NA_HEAD_DIM), 1)
    k_refs = (k0_ref, k1_ref, k2_ref, kc_ref)
    v_refs = (v0_ref, v1_ref, v2_ref, vc_ref)
    ones_tile = jnp.where(lax.broadcasted_iota(jnp.int32, (Q_BLK, 2 * NA_HEAD_DIM), 1) == 0, 1.0, 0.0).astype(BF16)
    for pair in range(NA_HEADS // 2):
        cs = slice(pair * 128, (pair + 1) * 128)
        qp = q_ref[:, cs]
        q_stack = jnp.concatenate([jnp.where(lane < NA_HEAD_DIM, qp, jnp.zeros_like(qp)),
                                   jnp.where(lane >= NA_HEAD_DIM, qp, jnp.zeros_like(qp))], axis=0)
        s_both = [_dot_nt(q_stack, r[:, cs]) for r in k_refs]
        e_both = [[], []]
        for hh in range(2):
            s = [sb[hh * Q_BLK:(hh + 1) * Q_BLK] for sb in s_both]
            for t in range(3):
                s[t] = s[t] + bias_ref[2 * pair + hh, :, t * 256:(t + 1) * 256].astype(F32)
            mx = jnp.max(jnp.maximum(jnp.maximum(s[0], s[1]), jnp.maximum(s[2], s[3])), axis=-1, keepdims=True)
            e_both[hh] = [jnp.exp2((st - mx).astype(BF16)) for st in s]
        o = None
        for t, r in enumerate(v_refs):
            p_stack = jnp.concatenate([e_both[0][t], e_both[1][t]], axis=0)
            c = _dot(p_stack, jnp.concatenate([r[:, cs], ones_tile], axis=1))
            o = c if o is None else o + c
        outs = [o[hh * Q_BLK:(hh + 1) * Q_BLK, :128] * (1.0 / o[hh * Q_BLK:(hh + 1) * Q_BLK, 128:129])
                for hh in range(2)]
        acc = jnp.where(lane < NA_HEAD_DIM, outs[0], outs[1])
        z = z_ref[:, cs].astype(F32)
        o_ref[:, cs] = (acc * (z * _sigmoid(z))).astype(BF16)


def _neighborhood_attention(p3, pc3, bias_tab):
    b, s, _ = p3.shape
    n_blk = s // Q_BLK

    def key_blk(i):
        return jnp.clip(i - 1, 0, n_blk - KEY_ROWS // Q_ROWS)

    def case_of(i):
        return jnp.minimum(i, 2) + jnp.maximum(i - (n_blk - 3), 0)

    cq, ck, cv, cz = COL_Q // 1024, COL_K // 1024, COL_V // 1024, COL_ZNA // 1024
    blk = (None, Q_BLK, 1024)
    in_specs = [pl.BlockSpec(blk, lambda i, j: (j, i, cq))]
    in_specs += [pl.BlockSpec(blk, functools.partial(lambda i, j, t: (j, key_blk(i) + t, ck), t=t)) for t in range(3)]
    in_specs += [pl.BlockSpec(blk, lambda i, j: (j, 0, CCOL_K // 1024))]
    in_specs += [pl.BlockSpec(blk, functools.partial(lambda i, j, t: (j, key_blk(i) + t, cv), t=t)) for t in range(3)]
    in_specs += [pl.BlockSpec(blk, lambda i, j: (j, 0, CCOL_V // 1024))]
    in_specs += [pl.BlockSpec(blk, lambda i, j: (j, i, cz))]
    in_specs += [pl.BlockSpec((None, NA_HEADS, Q_BLK, KEY_BLK), lambda i, j: (case_of(i), 0, 0, 0))]
    return pl.pallas_call(
        _na_kernel,
        grid=(n_blk, b),
        in_specs=in_specs,
        out_specs=pl.BlockSpec(blk, lambda i, j: (j, i, 0)),
        out_shape=jax.ShapeDtypeStruct((b, s, NA_HEADS * NA_HEAD_DIM), BF16),
        compiler_params=pltpu.CompilerParams(
            dimension_semantics=("parallel", "parallel"), vmem_limit_bytes=VMEM_LIMIT),
        name="neighborhood_attention",
    )(p3, p3, p3, p3, pc3, p3, p3, p3, pc3, p3, bias_tab)


def _softplus(v):
    return jnp.maximum(v, 0.0) + jnp.log(1.0 + jnp.exp(-jnp.abs(v)))


def _head_stack(v, head_lanes):
    return jnp.concatenate([jnp.where(m, v, jnp.zeros_like(v)) for m in head_lanes], axis=0)


def _ssd_kernel(xs_ref, bm_ref, cm_ref, z_ref, xsc_ref, bmc_ref, dtt_ref, dttc_ref,
                cwx_ref, cwb_ref, cwc_ref, cbx_ref, cbb_ref, cbc_ref,
                alog_ref, dtb_ref, skip_ref, ng_ref, o_ref,
                pad_ref, padc_ref, tap_ref, ux_ref, ubt_ref, uc_ref, uxc_ref, ubtc_ref,
                rows_ref, cols_ref, dec_ref, rowsc_ref, decc_ref, y_ref, st_ref, e_ref):
    q = SSM_CHUNK
    hpg = HEADS_PER_GROUP
    nh = N_DIRS * hpg
    s_len = xs_ref.shape[0]
    c_len = xsc_ref.shape[0]
    n_lat = s_len // q
    n_ctx = c_len // q
    halo = CONV_HALO
    blk = CONV_BLK
    win = blk + 2 * halo

    zeros_halo = jnp.zeros((halo, pad_ref.shape[1]), BF16)
    pad_ref[0:halo, :] = zeros_halo
    pad_ref[s_len + halo:s_len + 2 * halo, :] = zeros_halo
    pad_ref[halo:s_len + halo, 0:GROUP_X] = xs_ref[...]
    pad_ref[halo:s_len + halo, GROUP_X:GROUP_X + SSM_STATE] = bm_ref[...]
    pad_ref[halo:s_len + halo, GROUP_X + SSM_STATE:] = cm_ref[...]
    zeros_halo_c = jnp.zeros((halo, padc_ref.shape[1]), BF16)
    padc_ref[0:halo, :] = zeros_halo_c
    padc_ref[c_len + halo:c_len + 2 * halo, :] = zeros_halo_c
    padc_ref[halo:c_len + halo, 0:GROUP_X] = xsc_ref[...]
    padc_ref[halo:c_len + halo, GROUP_X:] = bmc_ref[...]

    cw = jnp.concatenate([cwx_ref[...], cwb_ref[...], cwc_ref[...]], axis=1)
    cb = jnp.concatenate([cbx_ref[...], cbb_ref[...], cbc_ref[...]], axis=1)
    for k in range(SSM_CONV):
        tap_ref[k] = jnp.broadcast_to(cw[k:k + 1, :], (win, cw.shape[1])).astype(BF16)
    off = lax.broadcasted_iota(jnp.int32, (blk, SSM_CONV * win), 1) - lax.broadcasted_iota(
        jnp.int32, (blk, SSM_CONV * win), 0)
    hit = off == halo - SSM_CONV // 2
    for k in range(1, SSM_CONV):
        hit = hit | (off == k * win + halo - SSM_CONV // 2 + k)
    shift = jnp.where(hit, 1.0, 0.0).astype(BF16)

    def conv_block(src_ref, start, width):
        w = src_ref[pl.ds(start, win), :]
        stack = jnp.concatenate([w * tap_ref[k, :, 0:width] for k in range(SSM_CONV)], axis=0)
        acc = _dot(shift, stack) + cb[:, :width]
        return acc * _sigmoid(acc)

    def conv_chunk(src_ref, start, width):
        return jnp.concatenate([conv_block(src_ref, start + i * blk, width) for i in range(q // blk)], axis=0)

    def conv_lat(c, carry):
        start = pl.multiple_of(c * q, q)
        u = conv_chunk(pad_ref, start, GROUP_X + 2 * SSM_STATE)
        ux_ref[pl.ds(start, q), :] = u[:, :GROUP_X]
        ubt_ref[c] = u[:, GROUP_X:GROUP_X + SSM_STATE].T.astype(BF16)
        uc_ref[pl.ds(start, q), :] = u[:, GROUP_X + SSM_STATE:].astype(BF16)
        return carry

    lax.fori_loop(0, n_lat, conv_lat, 0, unroll=4)
    for c in range(n_ctx):
        u = conv_chunk(padc_ref, c * q, GROUP_X + SSM_STATE)
        uxc_ref[c * q:(c + 1) * q, :] = u[:, :GROUP_X]
        ubtc_ref[c] = u[:, GROUP_X:].T.astype(BF16)

    ki = lax.broadcasted_iota(jnp.int32, (q, q), 0)
    ii = lax.broadcasted_iota(jnp.int32, (q, q), 1)
    tri_f = (ki <= ii).astype(F32)
    tri_b = (ki >= ii).astype(F32)
    lane_x = lax.broadcasted_iota(jnp.int32, (1, GROUP_X), 1)
    head_lanes = [(lane_x >= r * SSM_HEAD_DIM) & (lane_x < (r + 1) * SSM_HEAD_DIM) for r in range(hpg)]
    neg_a2 = -jnp.exp(alog_ref[...]) * LOG2E
    dt_bias = dtb_ref[...]

    def dt_forms(raw, n, rows_out, cols_out, dec_out):
        stack = lambda v: jnp.concatenate([v[:, c * q:(c + 1) * q] for c in range(n)], axis=0)
        dt_all = _softplus(raw + dt_bias)
        dt = stack(dt_all)
        a = stack(neg_a2 * dt_all)
        is_fwd = lax.broadcasted_iota(jnp.int32, (n * nh, 1), 0) % nh < hpg
        cum_f = jnp.dot(a, tri_f, preferred_element_type=F32, precision=lax.Precision.HIGHEST)
        cum_b = jnp.dot(a, tri_b, preferred_element_type=F32, precision=lax.Precision.HIGHEST)
        acum = jnp.where(is_fwd, cum_f, cum_b)
        atot = jnp.where(is_fwd, acum[:, q - 1:q], acum[:, 0:1])
        sw = dt * jnp.exp2(atot - acum)
        dec = jnp.exp2(atot)
        for c in range(n):
            sl = slice(c * nh, (c + 1) * nh)
            rows_out[c] = jnp.concatenate([dt[sl], acum[sl], sw[sl]], axis=0)
            if cols_out is not None:
                cols_out[c] = acum[sl].T
            dec_rows = []
            for d in range(N_DIRS):
                base = c * nh + d * hpg
                row = jnp.broadcast_to(dec[base + hpg - 1:base + hpg, :], (1, GROUP_X))
                for r in range(hpg - 2, -1, -1):
                    row = jnp.where(lane_x < (r + 1) * SSM_HEAD_DIM, dec[base + r:base + r + 1, :], row)
                dec_rows.append(row)
            dec_out[c] = jnp.concatenate(dec_rows, axis=0)

    dt_forms(dtt_ref[...], n_lat, rows_ref, cols_ref, dec_ref)
    dt_forms(dttc_ref[...], n_ctx, rowsc_ref, None, decc_ref)

    def state_term(d, rows, btf, xm):
        parts = [(btf * rows[2 * N_DIRS * hpg + d * hpg + r:2 * N_DIRS * hpg + d * hpg + r + 1, :]).astype(BF16)
                 for r in range(hpg)]
        return _dot(jnp.concatenate(parts, axis=1), xm)

    h_init = []
    for d in range(N_DIRS):
        h = jnp.zeros((SSM_STATE, GROUP_X), F32)
        for c in (range(n_ctx) if d == 0 else reversed(range(n_ctx))):
            xm = _head_stack(uxc_ref[c * q:(c + 1) * q, :].astype(BF16), head_lanes)
            h = decc_ref[c][d:d + 1, :] * h + state_term(d, rowsc_ref[c], ubtc_ref[c].astype(F32), xm)
        h_init.append(h)

    skip = skip_ref[...]
    norm_g = ng_ref[...]
    keep = (ki >= ii, ki <= ii)
    low_lanes = lax.broadcasted_iota(jnp.int32, (1, q), 1) < SSM_HEAD_DIM

    def chunk_body(c, carry):
        start = pl.multiple_of(c * q, q)
        rows = rows_ref[c]
        cols = cols_ref[c]
        xs = ux_ref[pl.ds(start, q), :]
        xm = _head_stack(xs.astype(BF16), head_lanes)
        cmat = uc_ref[pl.ds(start, q), :]
        bt = ubt_ref[c]
        cbm = _dot(cmat, bt)
        btf = bt.astype(F32)
        m_parts = []
        acols = [[], []]
        for r in range(hpg):
            t = None
            for d in range(N_DIRS):
                k = d * hpg + r
                acol = jnp.broadcast_to(cols[:, k:k + 1], (q, q))
                arow = rows[N_DIRS * hpg + k:N_DIRS * hpg + k + 1, :]
                term = jnp.where(keep[d], jnp.exp2(acol - arow) * rows[k:k + 1, :], 0.0)
                t = term if t is None else t + term
                acols[d].append(acol)
            m_parts.append((cbm * t).astype(BF16))
        y_ref[pl.ds(start, q), :] = xs * skip + _dot(jnp.concatenate(m_parts, axis=1), xm)
        for d in range(N_DIRS):
            st_ref[d, c] = state_term(d, rows, btf, xm)
            halves = [jnp.where(low_lanes, acols[d][2 * i], acols[d][2 * i + 1]) for i in range(hpg // 2)]
            e_ref[d, c] = jnp.exp2(jnp.concatenate(halves, axis=1))
        return carry

    lax.fori_loop(0, n_lat, chunk_body, 0, unroll=4)

    def scan_body(j, carry, finalize):
        new = []
        chunk_ids = (j, n_lat - 1 - j)
        for d in range(N_DIRS):
            h = carry[d]
            c = chunk_ids[d]
            start = pl.multiple_of(c * q, q)
            y_off = _dot(uc_ref[pl.ds(start, q), :], h.astype(BF16)) * e_ref[d, c]
            y_ref[pl.ds(start, q), :] = y_ref[pl.ds(start, q), :] + y_off
            new.append(dec_ref[c][d:d + 1, :] * h + st_ref[d, c])
        if finalize:
            for c in chunk_ids:
                start = pl.multiple_of(c * q, q)
                z = z_ref[pl.ds(start, q), :].astype(F32)
                u = y_ref[pl.ds(start, q), :] * (z * _sigmoid(z))
                u = u * lax.rsqrt(jnp.mean(u * u, axis=-1, keepdims=True) + EPS)
                o_ref[pl.ds(start, q), :] = (u * norm_g).astype(BF16)
        return tuple(new)

    half = n_lat // 2
    carry = lax.fori_loop(0, half, functools.partial(scan_body, finalize=False), tuple(h_init), unroll=2)
    lax.fori_loop(half, n_lat, functools.partial(scan_body, finalize=True), carry, unroll=2)


def _ssd_mixer(p3, pc3, dtt, dttc, conv_w, conv_b, alog8, dtb8, skip_exp, norm_g):
    b, s, _ = p3.shape
    c_len = pc3.shape[1]
    g = SSM_GROUPS
    n_lat, n_ctx = s // SSM_CHUNK, c_len // SSM_CHUNK
    assert n_lat % 2 == 0
    nx = GROUP_X
    ns = SSM_STATE
    nh = N_DIRS * HEADS_PER_GROUP
    in_specs = [
        pl.BlockSpec((None, s, nx), lambda i, j: (i, 0, COL_X // nx + j)),
        pl.BlockSpec((None, s, ns), lambda i, j: (i, 0, COL_B // ns + j)),
        pl.BlockSpec((None, s, ns), lambda i, j: (i, 0, COL_C // ns + j)),
        pl.BlockSpec((None, s, nx), lambda i, j: (i, 0, COL_ZSSM // nx + j)),
        pl.BlockSpec((None, c_len, nx), lambda i, j: (i, 0, CCOL_X // nx + j)),
        pl.BlockSpec((None, c_len, ns), lambda i, j: (i, 0, CCOL_B // ns + j)),
        pl.BlockSpec((None, nh, s), lambda i, j: (i, j, 0)),
        pl.BlockSpec((None, nh, c_len), lambda i, j: (i, j, 0)),
        pl.BlockSpec((SSM_CONV, nx), lambda i, j: (0, j)),
        pl.BlockSpec((SSM_CONV, ns), lambda i, j: (0, SSM_WIDTH // ns + j)),
        pl.BlockSpec((SSM_CONV, ns), lambda i, j: (0, SSM_WIDTH // ns + g + j)),
        pl.BlockSpec((1, nx), lambda i, j: (0, j)),
        pl.BlockSpec((1, ns), lambda i, j: (0, SSM_WIDTH // ns + j)),
        pl.BlockSpec((1, ns), lambda i, j: (0, SSM_WIDTH // ns + g + j)),
        pl.BlockSpec((None, nh, 1), lambda i, j: (j, 0, 0)),
        pl.BlockSpec((None, nh, 1), lambda i, j: (j, 0, 0)),
        pl.BlockSpec((None, 1, nx), lambda i, j: (j, 0, 0)),
        pl.BlockSpec((1, nx), lambda i, j: (0, j)),
    ]
    halo = CONV_HALO
    scratch = [
        pltpu.VMEM((s + 2 * halo, nx + 2 * ns), BF16),
        pltpu.VMEM((c_len + 2 * halo, nx + ns), BF16),
        pltpu.VMEM((SSM_CONV, CONV_BLK + 2 * halo, nx + 2 * ns), BF16),
        pltpu.VMEM((s, nx), F32),
        pltpu.VMEM((n_lat, ns, SSM_CHUNK), BF16),
        pltpu.VMEM((s, ns), BF16),
        pltpu.VMEM((c_len, nx), F32),
        pltpu.VMEM((n_ctx, ns, SSM_CHUNK), BF16),
        pltpu.VMEM((n_lat, 3 * nh, SSM_CHUNK), F32),
        pltpu.VMEM((n_lat, SSM_CHUNK, nh), F32),
        pltpu.VMEM((n_lat, N_DIRS, nx), F32),
        pltpu.VMEM((n_ctx, 3 * nh, SSM_CHUNK), F32),
        pltpu.VMEM((n_ctx, N_DIRS, nx), F32),
        pltpu.VMEM((s, nx), F32),
        pltpu.VMEM((N_DIRS, n_lat, ns, nx), F32),
        pltpu.VMEM((N_DIRS, n_lat, SSM_CHUNK, nx), F32),
    ]
    return pl.pallas_call(
        _ssd_kernel,
        grid=(b, g),
        in_specs=in_specs,
        out_specs=pl.BlockSpec((None, s, nx), lambda i, j: (i, 0, j)),
        out_shape=jax.ShapeDtypeStruct((b, s, SSM_WIDTH), BF16),
        scratch_shapes=scratch,
        compiler_params=pltpu.CompilerParams(
            dimension_semantics=("parallel", "parallel"), vmem_limit_bytes=VMEM_LIMIT),
        name="ssd_mixer",
    )(p3, p3, p3, p3, pc3, pc3, dtt, dttc, conv_w, conv_w, conv_w, conv_b, conv_b, conv_b,
      alog8, dtb8, skip_exp, norm_g)


def _merge_kernel(yna_ref, yssm_ref, gna_ref, gssm_ref, x_ref, gate_ref, wna_ref, wssm_ref, wout_ref, gpost_ref, o_ref):
    a = _dot(yna_ref[...], wna_ref[...])
    s = _dot(yssm_ref[...], wssm_ref[...])
    m = _sigmoid(gna_ref[...].astype(F32)) * a + _sigmoid(gssm_ref[...].astype(F32)) * s
    o = _dot(m.astype(BF16), wout_ref[...])
    r = o * lax.rsqrt(jnp.mean(o * o, axis=-1, keepdims=True) + EPS) * gpost_ref[...]
    o_ref[...] = x_ref[...] + gate_ref[...] * r


def _merge(y_na, y_ssm, p3, x3, gate, w_na, w_ssm, w_out, g_post, tm):
    b, s, d = x3.shape
    const = lambda i, j: (0, 0)
    return pl.pallas_call(
        _merge_kernel,
        grid=(b, s // tm),
        in_specs=[pl.BlockSpec((None, tm, y_na.shape[-1]), lambda i, j: (i, j, 0)),
                  pl.BlockSpec((None, tm, y_ssm.shape[-1]), lambda i, j: (i, j, 0)),
                  pl.BlockSpec((None, tm, d), lambda i, j: (i, j, COL_GNA // d)),
                  pl.BlockSpec((None, tm, d), lambda i, j: (i, j, COL_GSSM // d)),
                  pl.BlockSpec((None, tm, d), lambda i, j: (i, j, 0)),
                  pl.BlockSpec((None, 1, d), lambda i, j: (i, 0, 0)),
                  pl.BlockSpec(w_na.shape, const),
                  pl.BlockSpec(w_ssm.shape, const),
                  pl.BlockSpec(w_out.shape, const),
                  pl.BlockSpec((1, d), const)],
        out_specs=pl.BlockSpec((None, tm, d), lambda i, j: (i, j, 0)),
        out_shape=jax.ShapeDtypeStruct((b, s, d), x3.dtype),
        compiler_params=pltpu.CompilerParams(
            dimension_semantics=("parallel", "parallel"), vmem_limit_bytes=VMEM_LIMIT),
        name="merge_out",
    )(y_na, y_ssm, p3, p3, x3, gate, w_na, w_ssm, w_out, g_post)


def kernel(x, c, ctx, c_ctx, w_mod, b_mod, g_pre, g_post, w_in, conv_w, conv_b, a_log, dt_bias, d_skip,
           ssm_norm_g, rpb, w_na_out, w_ssm_out, w_out):
    assert w_mod.shape[0] == 1, "single-layer block"
    b, s, d = x.shape
    g, hpg = SSM_GROUPS, HEADS_PER_GROUP

    w_main, w_dt = _prep_w_in(jnp.transpose(w_in[0]), COL_GNA, N_DIRS * SSM_HEADS, COL_K, NA_HEAD_DIM ** -0.5 * LOG2E)
    per_group = lambda p: jnp.transpose(p.reshape(N_DIRS, g, hpg), (1, 0, 2)).reshape(g, N_DIRS * hpg, 1)
    alog8 = per_group(a_log[0])
    dtb8 = per_group(dt_bias[0])
    skip_exp = jnp.repeat(d_skip[0], SSM_HEAD_DIM).reshape(g, 1, GROUP_X)

    cond = jnp.zeros((16, d), F32).at[:b].set(c).at[b].set(c_ctx)
    mod = _adaln(cond, w_mod[0], b_mod[0])
    shift_l, scale_l, gate_l = (mod[:b, k * d:(k + 1) * d].reshape(b, 1, d) for k in range(3))
    shift_c, scale_c = (jnp.broadcast_to(mod[b, k * d:(k + 1) * d].reshape(1, 1, d), (b, 1, d)) for k in range(2))

    p3, dtt = _projection(x, shift_l, scale_l, g_pre, w_main, w_dt, ((0, NP_COLS),), tm=256)
    pc3, dttc = _projection(ctx, shift_c, scale_c, g_pre, w_main, w_dt,
                            ((COL_K, COL_ZNA), (COL_X, COL_GNA)), tm=256)

    bias_tab = _bias_table(rpb[0])
    y_na = _neighborhood_attention(p3, pc3, bias_tab)

    y_ssm = _ssd_mixer(p3, pc3, dtt, dttc, conv_w[0], conv_b[0].reshape(1, -1), alog8, dtb8, skip_exp,
                       ssm_norm_g)

    return _merge(y_na, y_ssm, p3, x, gate_l, w_na_out[0].astype(BF16), w_ssm_out[0].astype(BF16),
                  w_out[0].astype(BF16), g_post, tm=512)
```

```python
import functools

import numpy as np
import jax
import jax.numpy as jnp
from jax import lax
from jax.experimental import pallas as pl
from jax.experimental.pallas import tpu as pltpu

F32 = jnp.float32
BF16 = jnp.bfloat16

D_MODEL = 1024
GRID_W = 64
NA_HEADS = 16
NA_HEAD_DIM = 64
WIN_ROWS = 8
WIN_COLS = 16
SSM_WIDTH = 2 * D_MODEL
SSM_HEAD_DIM = 64
SSM_HEADS = SSM_WIDTH // SSM_HEAD_DIM
SSM_GROUPS = 8
HEADS_PER_GROUP = SSM_HEADS // SSM_GROUPS
SSM_STATE = 128
SSM_CONV = 5
SSM_CHUNK = 128
CONV_HALO = 16
CONV_BLK = 64
N_DIRS = 2
GROUP_X = HEADS_PER_GROUP * SSM_HEAD_DIM
EPS = 1e-6
NEG = -1e30
LOG2E = 1.4426950408889634

COL_Q, COL_K, COL_V, COL_ZNA = 0, 1024, 2048, 3072
COL_ZSSM, COL_X, COL_B, COL_C = 4096, 6144, 8192, 9216
COL_GNA, COL_GSSM, NP_COLS = 10240, 11264, 12288
CCOL_K, CCOL_V, CCOL_X, CCOL_B, NPC_COLS = 0, 1024, 2048, 4096, 6144

Q_ROWS = 4
Q_BLK = Q_ROWS * GRID_W
KEY_ROWS = 12
KEY_BLK = KEY_ROWS * GRID_W
N_BIAS_CASES = 5
BIAS_CASE_BLOCK = (0, 1, 2, 6, 7)
BIAS_LANES = 128

V7X_VMEM_BYTES = 64 * 1024 * 1024
VMEM_LIMIT = V7X_VMEM_BYTES * 7 // 8
LANES = 128
PROJ_TM = 256
PROJ_CHUNK = 512
MERGE_TM = 512
ADALN_TN = 1024
KEY_TILE = Q_BLK


def _sigmoid(v):
    return 1.0 / (1.0 + jnp.exp(-v))


def _dot(a, b):
    return jnp.dot(a, b, preferred_element_type=F32)


def _dot_nt(a, b):
    return lax.dot_general(a, b, (((1,), (1,)), ((), ())), preferred_element_type=F32)


def _adaln_kernel(cond_ref, w_ref, b_ref, o_ref):
    cnd = cond_ref[...]
    act = cnd * _sigmoid(cnd)
    o_ref[...] = jnp.dot(act, w_ref[...], preferred_element_type=F32,
                         precision=lax.Precision.HIGHEST) + b_ref[...]


def _adaln(cond, w_mod, b_mod):
    rows, d = cond.shape
    n = w_mod.shape[1]
    tn = ADALN_TN
    return pl.pallas_call(
        _adaln_kernel,
        grid=(n // tn,),
        in_specs=[pl.BlockSpec((rows, d), lambda j: (0, 0)),
                  pl.BlockSpec((d, tn), lambda j: (0, j)),
                  pl.BlockSpec((1, tn), lambda j: (0, j))],
        out_specs=pl.BlockSpec((rows, tn), lambda j: (0, j)),
        out_shape=jax.ShapeDtypeStruct((rows, n), F32),
        name="adaln",
    )(cond, w_mod, b_mod.reshape(1, n))


W_TILE = 512
DT_PAD = 128


def _wprep_kernel(a_ref, b_ref, dt_ref, w_ref, wdt_ref, *, n_aligned, shift, q_tiles, q_scale):
    j = pl.program_id(0)

    @pl.when(j < n_aligned)
    def _():
        scale = jnp.where(j < q_tiles, q_scale, 1.0)
        w_ref[...] = (a_ref[...] * scale).T.astype(BF16)

    @pl.when(j >= n_aligned)
    def _():
        w_ref[...] = jnp.concatenate([a_ref[shift:, :], b_ref[...]], axis=0).T.astype(BF16)

    @pl.when(j == 0)
    def _():
        dst = lax.broadcasted_iota(jnp.int32, (DT_PAD, DT_PAD), 1)
        src = lax.broadcasted_iota(jnp.int32, (DT_PAD, DT_PAD), 0)
        per_group = N_DIRS * HEADS_PER_GROUP
        grp, rem = dst // per_group, dst % per_group
        want = (rem // HEADS_PER_GROUP) * SSM_HEADS + grp * HEADS_PER_GROUP + rem % HEADS_PER_GROUP
        perm = jnp.where((src == want) & (dst < N_DIRS * SSM_HEADS), 1.0, 0.0).astype(BF16)
        wdt_ref[...] = _dot(dt_ref[...].T.astype(BF16), perm).astype(BF16)


def _prep_w_in(w_in_t, col_dt, n_dt, q_cols, q_scale):
    p, d = w_in_t.shape
    n_out = p - n_dt
    assert col_dt % W_TILE == 0 and n_out % W_TILE == 0 and q_cols % W_TILE == 0
    assert W_TILE % n_dt == 0 and n_dt % 8 == 0 and n_dt <= DT_PAD and col_dt % DT_PAD == 0
    return pl.pallas_call(
        functools.partial(_wprep_kernel, n_aligned=col_dt // W_TILE, shift=n_dt, q_tiles=q_cols // W_TILE,
                          q_scale=q_scale),
        grid=(n_out // W_TILE,),
        in_specs=[pl.BlockSpec((W_TILE, d), lambda j: (j, 0)),
                  pl.BlockSpec((n_dt, d), lambda j: ((j + 1) * (W_TILE // n_dt), 0)),
                  pl.BlockSpec((DT_PAD, d), lambda j: (col_dt // DT_PAD, 0))],
        out_specs=[pl.BlockSpec((d, W_TILE), lambda j: (0, j)),
                   pl.BlockSpec((d, DT_PAD), lambda j: (0, 0))],
        out_shape=[jax.ShapeDtypeStruct((d, n_out), BF16), jax.ShapeDtypeStruct((d, DT_PAD), BF16)],
        name="w_in_prep",
    )(w_in_t, w_in_t, w_in_t)


def _proj_kernel(x_ref, shift_ref, scale_ref, g_ref, w_ref, wdt_ref, p_ref, dtt_ref, *, chunks):
    x = x_ref[...]
    ms = jnp.mean(x * x, axis=-1, keepdims=True)
    h = x * lax.rsqrt(ms + EPS) * g_ref[...]
    h = h * (1.0 + scale_ref[...]) + shift_ref[...]
    hb = h.astype(BF16)
    for src, dst, width in chunks:
        p_ref[:, dst:dst + width] = _dot(hb, w_ref[:, src:src + width]).astype(BF16)
    dt = _dot(hb, wdt_ref[...])
    dtt_ref[...] = dt.T[:N_DIRS * SSM_HEADS]


def _projection(x3, shift, scale, g_pre, w_main, w_dt, col_ranges, tm):
    b, l, d = x3.shape
    chunks, dst = [], 0
    cw = PROJ_CHUNK
    for lo, hi in col_ranges:
        for src in range(lo, hi, cw):
            chunks.append((src, dst, cw))
            dst += cw
    n_out = dst
    n_dt = N_DIRS * SSM_HEADS
    return pl.pallas_call(
        functools.partial(_proj_kernel, chunks=tuple(chunks)),
        grid=(b, l // tm),
        in_specs=[pl.BlockSpec((None, tm, d), lambda i, j: (i, j, 0)),
                  pl.BlockSpec((None, 1, d), lambda i, j: (i, 0, 0)),
                  pl.BlockSpec((None, 1, d), lambda i, j: (i, 0, 0)),
                  pl.BlockSpec((1, d), lambda i, j: (0, 0)),
                  pl.BlockSpec(w_main.shape, lambda i, j: (0, 0), pipeline_mode=pl.Buffered(1)),
                  pl.BlockSpec(w_dt.shape, lambda i, j: (0, 0), pipeline_mode=pl.Buffered(1))],
        out_specs=[pl.BlockSpec((None, tm, n_out), lambda i, j: (i, j, 0)),
                   pl.BlockSpec((None, n_dt, tm), lambda i, j: (i, 0, j))],
        out_shape=[jax.ShapeDtypeStruct((b, l, n_out), BF16),
                   jax.ShapeDtypeStruct((b, n_dt, l), F32)],
        compiler_params=pltpu.CompilerParams(
            dimension_semantics=("parallel", "parallel"), vmem_limit_bytes=VMEM_LIMIT),
        name="projection",
    )(x3, shift, scale, g_pre, w_main, w_dt)


def _bias_block_plan():
    rows = 2048 // GRID_W
    plan = np.full((N_BIAS_CASES, Q_ROWS, KEY_ROWS), -1, np.int64)
    for case, blk in enumerate(BIAS_CASE_BLOCK):
        u0 = int(np.clip(Q_ROWS * blk - WIN_ROWS // 2, 0, rows - KEY_ROWS))
        for rho in range(Q_ROWS):
            r = Q_ROWS * blk + rho
            r0 = int(np.clip(r - WIN_ROWS // 2, 0, rows - WIN_ROWS))
            for a in range(KEY_ROWS):
                krow = u0 + a
                if r0 <= krow < r0 + WIN_ROWS:
                    plan[case, rho, a] = krow - r + WIN_ROWS - 1
    return plan


def _bias_kernel(rpb_ref, o_ref, toep_ref, *, plan):
    n_dr = 2 * WIN_ROWS - 1
    qc = lax.broadcasted_iota(jnp.int32, (GRID_W, GRID_W), 0)
    kc = lax.broadcasted_iota(jnp.int32, (GRID_W, GRID_W), 1)
    c0 = jnp.clip(qc - WIN_COLS // 2, 0, GRID_W - WIN_COLS)
    in_win = (kc >= c0) & (kc < c0 + WIN_COLS)
    for dr in range(n_dr):
        rows = jnp.broadcast_to(rpb_ref[dr:dr + 1, :], (GRID_W, BIAS_LANES))
        toep = pltpu.roll(rows, BIAS_LANES - (WIN_COLS - 1), 1, stride=1, stride_axis=0)[:, :GRID_W]
        toep_ref[dr] = jnp.where(in_win, toep * LOG2E, NEG)
    neg = jnp.full((GRID_W, GRID_W), NEG, F32)
    for case in range(N_BIAS_CASES):
        for rho in range(Q_ROWS):
            for a in range(KEY_ROWS):
                dr = int(plan[case, rho, a])
                val = neg if dr < 0 else toep_ref[dr]
                o_ref[case, rho * GRID_W:(rho + 1) * GRID_W, a * GRID_W:(a + 1) * GRID_W] = val.astype(BF16)


def _bias_table(rpb):
    plan = _bias_block_plan()
    heads, n_dr, n_dc = rpb.shape
    rpb_lanes = jnp.pad(rpb, ((0, 0), (0, 0), (0, BIAS_LANES - n_dc)))
    return pl.pallas_call(
        functools.partial(_bias_kernel, plan=plan),
        grid=(heads,),
        in_specs=[pl.BlockSpec((None, n_dr, BIAS_LANES), lambda h: (h, 0, 0))],
        out_specs=pl.BlockSpec((N_BIAS_CASES, None, Q_BLK, KEY_BLK), lambda h: (0, h, 0, 0)),
        out_shape=jax.ShapeDtypeStruct((N_BIAS_CASES, heads, Q_BLK, KEY_BLK), BF16),
        scratch_shapes=[pltpu.VMEM((n_dr, GRID_W, GRID_W), F32)],
        name="bias_table",
    )(rpb_lanes)


def _na_kernel(q_ref, k0_ref, k1_ref, k2_ref, kc_ref, v0_ref, v1_ref, v2_ref, vc_ref, z_ref, bias_ref, o_ref):
    lane = lax.broadcasted_iota(jnp.int32, (1, 2 * NA_HEAD_DIM), 1)
    k_refs = (k0_ref, k1_ref, k2_ref, kc_ref)
    v_refs = (v0_ref, v1_ref, v2_ref, vc_ref)
    ones_tile = jnp.where(lax.broadcasted_iota(jnp.int32, (Q_BLK, 2 * NA_HEAD_DIM), 1) == 0, 1.0, 0.0).astype(BF16)
    for pair in range(NA_HEADS // 2):
        cs = slice(pair * LANES, (pair + 1) * LANES)
        qp = q_ref[:, cs]
        q_stack = jnp.concatenate([jnp.where(lane < NA_HEAD_DIM, qp, jnp.zeros_like(qp)),
                                   jnp.where(lane >= NA_HEAD_DIM, qp, jnp.zeros_like(qp))], axis=0)
        s_both = [_dot_nt(q_stack, r[:, cs]) for r in k_refs]
        e_both = [[], []]
        for hh in range(2):
            s = [sb[hh * Q_BLK:(hh + 1) * Q_BLK] for sb in s_both]
            for t in range(KEY_ROWS // Q_ROWS):
                s[t] = s[t] + bias_ref[2 * pair + hh, :, t * KEY_TILE:(t + 1) * KEY_TILE].astype(F32)
            mx = jnp.max(jnp.maximum(jnp.maximum(s[0], s[1]), jnp.maximum(s[2], s[3])), axis=-1, keepdims=True)
            e_both[hh] = [jnp.exp2((st - mx).astype(BF16)) for st in s]
        o = None
        for t, r in enumerate(v_refs):
            p_stack = jnp.concatenate([e_both[0][t], e_both[1][t]], axis=0)
            c = _dot(p_stack, jnp.concatenate([r[:, cs], ones_tile], axis=1))
            o = c if o is None else o + c
        outs = [o[hh * Q_BLK:(hh + 1) * Q_BLK, :LANES] * (1.0 / o[hh * Q_BLK:(hh + 1) * Q_BLK, LANES:LANES + 1])
                for hh in range(2)]
        acc = jnp.where(lane < NA_HEAD_DIM, outs[0], outs[1])
        z = z_ref[:, cs].astype(F32)
        o_ref[:, cs] = (acc * (z * _sigmoid(z))).astype(BF16)


def _neighborhood_attention(p3, pc3, bias_tab):
    b, s, _ = p3.shape
    n_blk = s // Q_BLK

    def key_blk(i):
        return jnp.clip(i - 1, 0, n_blk - KEY_ROWS // Q_ROWS)

    def case_of(i):
        return jnp.minimum(i, 2) + jnp.maximum(i - (n_blk - 3), 0)

    cq, ck, cv, cz = COL_Q // 1024, COL_K // 1024, COL_V // 1024, COL_ZNA // 1024
    blk = (None, Q_BLK, 1024)
    in_specs = [pl.BlockSpec(blk, lambda i, j: (j, i, cq))]
    in_specs += [pl.BlockSpec(blk, functools.partial(lambda i, j, t: (j, key_blk(i) + t, ck), t=t)) for t in range(3)]
    in_specs += [pl.BlockSpec(blk, lambda i, j: (j, 0, CCOL_K // 1024))]
    in_specs += [pl.BlockSpec(blk, functools.partial(lambda i, j, t: (j, key_blk(i) + t, cv), t=t)) for t in range(3)]
    in_specs += [pl.BlockSpec(blk, lambda i, j: (j, 0, CCOL_V // 1024))]
    in_specs += [pl.BlockSpec(blk, lambda i, j: (j, i, cz))]
    in_specs += [pl.BlockSpec((None, NA_HEADS, Q_BLK, KEY_BLK), lambda i, j: (case_of(i), 0, 0, 0))]
    return pl.pallas_call(
        _na_kernel,
        grid=(n_blk, b),
        in_specs=in_specs,
        out_specs=pl.BlockSpec(blk, lambda i, j: (j, i, 0)),
        out_shape=jax.ShapeDtypeStruct((b, s, NA_HEADS * NA_HEAD_DIM), BF16),
        compiler_params=pltpu.CompilerParams(
            dimension_semantics=("parallel", "parallel"), vmem_limit_bytes=VMEM_LIMIT),
        name="neighborhood_attention",
    )(p3, p3, p3, p3, pc3, p3, p3, p3, pc3, p3, bias_tab)


def _softplus(v):
    return jnp.maximum(v, 0.0) + jnp.log(1.0 + jnp.exp(-jnp.abs(v)))


def _head_stack(v, head_lanes):
    return jnp.concatenate([jnp.where(m, v, jnp.zeros_like(v)) for m in head_lanes], axis=0)


def _ssd_kernel(xs_ref, bm_ref, cm_ref, z_ref, xsc_ref, bmc_ref, dtt_ref, dttc_ref,
                cwx_ref, cwb_ref, cwc_ref, cbx_ref, cbb_ref, cbc_ref,
                alog_ref, dtb_ref, skip_ref, ng_ref, o_ref,
                pad_ref, padc_ref, tap_ref, ux_ref, ubt_ref, uc_ref, uxc_ref, ubtc_ref,
                rows_ref, cols_ref, dec_ref, rowsc_ref, decc_ref, y_ref, st_ref, e_ref):
    q = SSM_CHUNK
    hpg = HEADS_PER_GROUP
    nh = N_DIRS * hpg
    s_len = xs_ref.shape[0]
    c_len = xsc_ref.shape[0]
    n_lat = s_len // q
    n_ctx = c_len // q
    halo = CONV_HALO
    blk = CONV_BLK
    win = blk + 2 * halo

    zeros_halo = jnp.zeros((halo, pad_ref.shape[1]), BF16)
    pad_ref[0:halo, :] = zeros_halo
    pad_ref[s_len + halo:s_len + 2 * halo, :] = zeros_halo
    pad_ref[halo:s_len + halo, 0:GROUP_X] = xs_ref[...]
    pad_ref[halo:s_len + halo, GROUP_X:GROUP_X + SSM_STATE] = bm_ref[...]
    pad_ref[halo:s_len + halo, GROUP_X + SSM_STATE:] = cm_ref[...]
    zeros_halo_c = jnp.zeros((halo, padc_ref.shape[1]), BF16)
    padc_ref[0:halo, :] = zeros_halo_c
    padc_ref[c_len + halo:c_len + 2 * halo, :] = zeros_halo_c
    padc_ref[halo:c_len + halo, 0:GROUP_X] = xsc_ref[...]
    padc_ref[halo:c_len + halo, GROUP_X:] = bmc_ref[...]

    cw = jnp.concatenate([cwx_ref[...], cwb_ref[...], cwc_ref[...]], axis=1)
    cb = jnp.concatenate([cbx_ref[...], cbb_ref[...], cbc_ref[...]], axis=1)
    for k in range(SSM_CONV):
        tap_ref[k] = jnp.broadcast_to(cw[k:k + 1, :], (win, cw.shape[1])).astype(BF16)
    off = lax.broadcasted_iota(jnp.int32, (blk, SSM_CONV * win), 1) - lax.broadcasted_iota(
        jnp.int32, (blk, SSM_CONV * win), 0)
    hit = off == halo - SSM_CONV // 2
    for k in range(1, SSM_CONV):
        hit = hit | (off == k * win + halo - SSM_CONV // 2 + k)
    shift = jnp.where(hit, 1.0, 0.0).astype(BF16)

    def conv_block(src_ref, start, width):
        w = src_ref[pl.ds(start, win), :]
        stack = jnp.concatenate([w * tap_ref[k, :, 0:width] for k in range(SSM_CONV)], axis=0)
        acc = _dot(shift, stack) + cb[:, :width]
        return acc * _sigmoid(acc)

    def conv_chunk(src_ref, start, width):
        return jnp.concatenate([conv_block(src_ref, start + i * blk, width) for i in range(q // blk)], axis=0)

    def conv_lat(c, carry):
        start = pl.multiple_of(c * q, q)
        u = conv_chunk(pad_ref, start, GROUP_X + 2 * SSM_STATE)
        ux_ref[pl.ds(start, q), :] = u[:, :GROUP_X]
        ubt_ref[c] = u[:, GROUP_X:GROUP_X + SSM_STATE].T.astype(BF16)
        uc_ref[pl.ds(start, q), :] = u[:, GROUP_X + SSM_STATE:].astype(BF16)
        return carry

    lax.fori_loop(0, n_lat, conv_lat, 0, unroll=4)
    for c in range(n_ctx):
        u = conv_chunk(padc_ref, c * q, GROUP_X + SSM_STATE)
        uxc_ref[c * q:(c + 1) * q, :] = u[:, :GROUP_X]
        ubtc_ref[c] = u[:, GROUP_X:].T.astype(BF16)

    ki = lax.broadcasted_iota(jnp.int32, (q, q), 0)
    ii = lax.broadcasted_iota(jnp.int32, (q, q), 1)
    tri_f = (ki <= ii).astype(F32)
    tri_b = (ki >= ii).astype(F32)
    lane_x = lax.broadcasted_iota(jnp.int32, (1, GROUP_X), 1)
    head_lanes = [(lane_x >= r * SSM_HEAD_DIM) & (lane_x < (r + 1) * SSM_HEAD_DIM) for r in range(hpg)]
    neg_a2 = -jnp.exp(alog_ref[...]) * LOG2E
    dt_bias = dtb_ref[...]

    def dt_forms(raw, n, rows_out, cols_out, dec_out):
        stack = lambda v: jnp.concatenate([v[:, c * q:(c + 1) * q] for c in range(n)], axis=0)
        dt_all = _softplus(raw + dt_bias)
        dt = stack(dt_all)
        a = stack(neg_a2 * dt_all)
        is_fwd = lax.broadcasted_iota(jnp.int32, (n * nh, 1), 0) % nh < hpg
        cum_f = jnp.dot(a, tri_f, preferred_element_type=F32, precision=lax.Precision.HIGHEST)
        cum_b = jnp.dot(a, tri_b, preferred_element_type=F32, precision=lax.Precision.HIGHEST)
        acum = jnp.where(is_fwd, cum_f, cum_b)
        atot = jnp.where(is_fwd, acum[:, q - 1:q], acum[:, 0:1])
        sw = dt * jnp.exp2(atot - acum)
        dec = jnp.exp2(atot)
        for c in range(n):
            sl = slice(c * nh, (c + 1) * nh)
            rows_out[c] = jnp.concatenate([dt[sl], acum[sl], sw[sl]], axis=0)
            if cols_out is not None:
                cols_out[c] = acum[sl].T
            dec_rows = []
            for d in range(N_DIRS):
                base = c * nh + d * hpg
                row = jnp.broadcast_to(dec[base + hpg - 1:base + hpg, :], (1, GROUP_X))
                for r in range(hpg - 2, -1, -1):
                    row = jnp.where(lane_x < (r + 1) * SSM_HEAD_DIM, dec[base + r:base + r + 1, :], row)
                dec_rows.append(row)
            dec_out[c] = jnp.concatenate(dec_rows, axis=0)

    dt_forms(dtt_ref[...], n_lat, rows_ref, cols_ref, dec_ref)
    dt_forms(dttc_ref[...], n_ctx, rowsc_ref, None, decc_ref)

    def state_term(d, rows, btf, xm):
        parts = [(btf * rows[2 * N_DIRS * hpg + d * hpg + r:2 * N_DIRS * hpg + d * hpg + r + 1, :]).astype(BF16)
                 for r in range(hpg)]
        return _dot(jnp.concatenate(parts, axis=1), xm)

    h_init = []
    for d in range(N_DIRS):
        h = jnp.zeros((SSM_STATE, GROUP_X), F32)
        for c in (range(n_ctx) if d == 0 else reversed(range(n_ctx))):
            xm = _head_stack(uxc_ref[c * q:(c + 1) * q, :].astype(BF16), head_lanes)
            h = decc_ref[c][d:d + 1, :] * h + state_term(d, rowsc_ref[c], ubtc_ref[c].astype(F32), xm)
        h_init.append(h)

    skip = skip_ref[...]
    norm_g = ng_ref[...]
    keep = (ki >= ii, ki <= ii)
    low_lanes = lax.broadcasted_iota(jnp.int32, (1, q), 1) < SSM_HEAD_DIM

    def chunk_body(c, carry):
        start = pl.multiple_of(c * q, q)
        rows = rows_ref[c]
        cols = cols_ref[c]
        xs = ux_ref[pl.ds(start, q), :]
        xm = _head_stack(xs.astype(BF16), head_lanes)
        cmat = uc_ref[pl.ds(start, q), :]
        bt = ubt_ref[c]
        cbm = _dot(cmat, bt)
        btf = bt.astype(F32)
        m_parts = []
        acols = [[], []]
        for r in range(hpg):
            t = None
            for d in range(N_DIRS):
                k = d * hpg + r
                acol = jnp.broadcast_to(cols[:, k:k + 1], (q, q))
                arow = rows[N_DIRS * hpg + k:N_DIRS * hpg + k + 1, :]
                term = jnp.where(keep[d], jnp.exp2(acol - arow) * rows[k:k + 1, :], 0.0)
                t = term if t is None else t + term
                acols[d].append(acol)
            m_parts.append((cbm * t).astype(BF16))
        y_ref[pl.ds(start, q), :] = xs * skip + _dot(jnp.concatenate(m_parts, axis=1), xm)
        for d in range(N_DIRS):
            st_ref[d, c] = state_term(d, rows, btf, xm)
            halves = [jnp.where(low_lanes, acols[d][2 * i], acols[d][2 * i + 1]) for i in range(hpg // 2)]
            e_ref[d, c] = jnp.exp2(jnp.concatenate(halves, axis=1))
        return carry

    lax.fori_loop(0, n_lat, chunk_body, 0, unroll=4)

    def scan_body(j, carry, finalize):
        new = []
        chunk_ids = (j, n_lat - 1 - j)
        for d in range(N_DIRS):
            h = carry[d]
            c = chunk_ids[d]
            start = pl.multiple_of(c * q, q)
            y_off = _dot(uc_ref[pl.ds(start, q), :], h.astype(BF16)) * e_ref[d, c]
            y_ref[pl.ds(start, q), :] = y_ref[pl.ds(start, q), :] + y_off
            new.append(dec_ref[c][d:d + 1, :] * h + st_ref[d, c])
        if finalize:
            for c in chunk_ids:
                start = pl.multiple_of(c * q, q)
                z = z_ref[pl.ds(start, q), :].astype(F32)
                u = y_ref[pl.ds(start, q), :] * (z * _sigmoid(z))
                u = u * lax.rsqrt(jnp.mean(u * u, axis=-1, keepdims=True) + EPS)
                o_ref[pl.ds(start, q), :] = (u * norm_g).astype(BF16)
        return tuple(new)

    half = n_lat // 2
    carry = lax.fori_loop(0, half, functools.partial(scan_body, finalize=False), tuple(h_init), unroll=2)
    lax.fori_loop(half, n_lat, functools.partial(scan_body, finalize=True), carry, unroll=2)


def _ssd_mixer(p3, pc3, dtt, dttc, conv_w, conv_b, alog8, dtb8, skip_exp, norm_g):
    b, s, _ = p3.shape
    c_len = pc3.shape[1]
    g = SSM_GROUPS
    n_lat, n_ctx = s // SSM_CHUNK, c_len // SSM_CHUNK
    assert n_lat % 2 == 0
    nx = GROUP_X
    ns = SSM_STATE
    nh = N_DIRS * HEADS_PER_GROUP
    in_specs = [
        pl.BlockSpec((None, s, nx), lambda i, j: (i, 0, COL_X // nx + j)),
        pl.BlockSpec((None, s, ns), lambda i, j: (i, 0, COL_B // ns + j)),
        pl.BlockSpec((None, s, ns), lambda i, j: (i, 0, COL_C // ns + j)),
        pl.BlockSpec((None, s, nx), lambda i, j: (i, 0, COL_ZSSM // nx + j)),
        pl.BlockSpec((None, c_len, nx), lambda i, j: (i, 0, CCOL_X // nx + j)),
        pl.BlockSpec((None, c_len, ns), lambda i, j: (i, 0, CCOL_B // ns + j)),
        pl.BlockSpec((None, nh, s), lambda i, j: (i, j, 0)),
        pl.BlockSpec((None, nh, c_len), lambda i, j: (i, j, 0)),
        pl.BlockSpec((SSM_CONV, nx), lambda i, j: (0, j)),
        pl.BlockSpec((SSM_CONV, ns), lambda i, j: (0, SSM_WIDTH // ns + j)),
        pl.BlockSpec((SSM_CONV, ns), lambda i, j: (0, SSM_WIDTH // ns + g + j)),
        pl.BlockSpec((1, nx), lambda i, j: (0, j)),
        pl.BlockSpec((1, ns), lambda i, j: (0, SSM_WIDTH // ns + j)),
        pl.BlockSpec((1, ns), lambda i, j: (0, SSM_WIDTH // ns + g + j)),
        pl.BlockSpec((None, nh, 1), lambda i, j: (j, 0, 0)),
        pl.BlockSpec((None, nh, 1), lambda i, j: (j, 0, 0)),
        pl.BlockSpec((None, 1, nx), lambda i, j: (j, 0, 0)),
        pl.BlockSpec((1, nx), lambda i, j: (0, j)),
    ]
    halo = CONV_HALO
    scratch = [
        pltpu.VMEM((s + 2 * halo, nx + 2 * ns), BF16),
        pltpu.VMEM((c_len + 2 * halo, nx + ns), BF16),
        pltpu.VMEM((SSM_CONV, CONV_BLK + 2 * halo, nx + 2 * ns), BF16),
        pltpu.VMEM((s, nx), F32),
        pltpu.VMEM((n_lat, ns, SSM_CHUNK), BF16),
        pltpu.VMEM((s, ns), BF16),
        pltpu.VMEM((c_len, nx), F32),
        pltpu.VMEM((n_ctx, ns, SSM_CHUNK), BF16),
        pltpu.VMEM((n_lat, 3 * nh, SSM_CHUNK), F32),
        pltpu.VMEM((n_lat, SSM_CHUNK, nh), F32),
        pltpu.VMEM((n_lat, N_DIRS, nx), F32),
        pltpu.VMEM((n_ctx, 3 * nh, SSM_CHUNK), F32),
        pltpu.VMEM((n_ctx, N_DIRS, nx), F32),
        pltpu.VMEM((s, nx), F32),
        pltpu.VMEM((N_DIRS, n_lat, ns, nx), F32),
        pltpu.VMEM((N_DIRS, n_lat, SSM_CHUNK, nx), F32),
    ]
    return pl.pallas_call(
        _ssd_kernel,
        grid=(b, g),
        in_specs=in_specs,
        out_specs=pl.BlockSpec((None, s, nx), lambda i, j: (i, 0, j)),
        out_shape=jax.ShapeDtypeStruct((b, s, SSM_WIDTH), BF16),
        scratch_shapes=scratch,
        compiler_params=pltpu.CompilerParams(
            dimension_semantics=("parallel", "parallel"), vmem_limit_bytes=VMEM_LIMIT),
        name="ssd_mixer",
    )(p3, p3, p3, p3, pc3, pc3, dtt, dttc, conv_w, conv_w, conv_w, conv_b, conv_b, conv_b,
      alog8, dtb8, skip_exp, norm_g)


def _merge_kernel(yna_ref, yssm_ref, gna_ref, gssm_ref, x_ref, gate_ref, wna_ref, wssm_ref, wout_ref, gpost_ref, o_ref):
    a = _dot(yna_ref[...], wna_ref[...])
    s = _dot(yssm_ref[...], wssm_ref[...])
    m = _sigmoid(gna_ref[...].astype(F32)) * a + _sigmoid(gssm_ref[...].astype(F32)) * s
    o = _dot(m.astype(BF16), wout_ref[...])
    r = o * lax.rsqrt(jnp.mean(o * o, axis=-1, keepdims=True) + EPS) * gpost_ref[...]
    o_ref[...] = x_ref[...] + gate_ref[...] * r


def _merge(y_na, y_ssm, p3, x3, gate, w_na, w_ssm, w_out, g_post, tm):
    b, s, d = x3.shape
    const = lambda i, j: (0, 0)
    return pl.pallas_call(
        _merge_kernel,
        grid=(b, s // tm),
        in_specs=[pl.BlockSpec((None, tm, y_na.shape[-1]), lambda i, j: (i, j, 0)),
                  pl.BlockSpec((None, tm, y_ssm.shape[-1]), lambda i, j: (i, j, 0)),
                  pl.BlockSpec((None, tm, d), lambda i, j: (i, j, COL_GNA // d)),
                  pl.BlockSpec((None, tm, d), lambda i, j: (i, j, COL_GSSM // d)),
                  pl.BlockSpec((None, tm, d), lambda i, j: (i, j, 0)),
                  pl.BlockSpec((None, 1, d), lambda i, j: (i, 0, 0)),
                  pl.BlockSpec(w_na.shape, const),
                  pl.BlockSpec(w_ssm.shape, const),
                  pl.BlockSpec(w_out.shape, const),
                  pl.BlockSpec((1, d), const)],
        out_specs=pl.BlockSpec((None, tm, d), lambda i, j: (i, j, 0)),
        out_shape=jax.ShapeDtypeStruct((b, s, d), x3.dtype),
        compiler_params=pltpu.CompilerParams(
            dimension_semantics=("parallel", "parallel"), vmem_limit_bytes=VMEM_LIMIT),
        name="merge_out",
    )(y_na, y_ssm, p3, p3, x3, gate, w_na, w_ssm, w_out, g_post)


def kernel(x, c, ctx, c_ctx, w_mod, b_mod, g_pre, g_post, w_in, conv_w, conv_b, a_log, dt_bias, d_skip,
           ssm_norm_g, rpb, w_na_out, w_ssm_out, w_out):
    assert w_mod.shape[0] == 1, "single-layer block"
    b, s, d = x.shape
    g, hpg = SSM_GROUPS, HEADS_PER_GROUP

    w_main, w_dt = _prep_w_in(jnp.transpose(w_in[0]), COL_GNA, N_DIRS * SSM_HEADS, COL_K, NA_HEAD_DIM ** -0.5 * LOG2E)
    per_group = lambda p: jnp.transpose(p.reshape(N_DIRS, g, hpg), (1, 0, 2)).reshape(g, N_DIRS * hpg, 1)
    alog8 = per_group(a_log[0])
    dtb8 = per_group(dt_bias[0])
    skip_exp = jnp.repeat(d_skip[0], SSM_HEAD_DIM).reshape(g, 1, GROUP_X)

    cond = jnp.zeros((16, d), F32).at[:b].set(c).at[b].set(c_ctx)
    mod = _adaln(cond, w_mod[0], b_mod[0])
    shift_l, scale_l, gate_l = (mod[:b, k * d:(k + 1) * d].reshape(b, 1, d) for k in range(3))
    shift_c, scale_c = (jnp.broadcast_to(mod[b, k * d:(k + 1) * d].reshape(1, 1, d), (b, 1, d)) for k in range(2))

    p3, dtt = _projection(x, shift_l, scale_l, g_pre, w_main, w_dt, ((0, NP_COLS),), tm=PROJ_TM)
    pc3, dttc = _projection(ctx, shift_c, scale_c, g_pre, w_main, w_dt,
                            ((COL_K, COL_ZNA), (COL_X, COL_GNA)), tm=PROJ_TM)

    bias_tab = _bias_table(rpb[0])
    y_na = _neighborhood_attention(p3, pc3, bias_tab)

    y_ssm = _ssd_mixer(p3, pc3, dtt, dttc, conv_w[0], conv_b[0].reshape(1, -1), alog8, dtb8, skip_exp,
                       ssm_norm_g)

    return _merge(y_na, y_ssm, p3, x, gate_l, w_na_out[0].astype(BF16), w_ssm_out[0].astype(BF16),
                  w_out[0].astype(BF16), g_post, tm=MERGE_TM)
```

```python
import functools

import numpy as np
import jax
import jax.numpy as jnp
from jax import lax
from jax.experimental import pallas as pl
from jax.experimental.pallas import tpu as pltpu

F32 = jnp.float32
BF16 = jnp.bfloat16

D_MODEL = 1024
GRID_W = 64
NA_HEADS = 16
NA_HEAD_DIM = 64
WIN_ROWS = 8
WIN_COLS = 16
SSM_WIDTH = 2 * D_MODEL
SSM_HEAD_DIM = 64
SSM_HEADS = SSM_WIDTH // SSM_HEAD_DIM
SSM_GROUPS = 8
HEADS_PER_GROUP = SSM_HEADS // SSM_GROUPS
SSM_STATE = 128
SSM_CONV = 5
SSM_CHUNK = 128
CONV_HALO = 16
CONV_BLK = 64
N_DIRS = 2
GROUP_X = HEADS_PER_GROUP * SSM_HEAD_DIM
EPS = 1e-6
NEG = -1e30
LOG2E = 1.4426950408889634

COL_Q, COL_K, COL_V, COL_ZNA = 0, 1024, 2048, 3072
COL_ZSSM, COL_X, COL_B, COL_C = 4096, 6144, 8192, 9216
COL_GNA, COL_GSSM, NP_COLS = 10240, 11264, 12288
CCOL_K, CCOL_V, CCOL_X, CCOL_B, NPC_COLS = 0, 1024, 2048, 4096, 6144

Q_ROWS = 4
Q_BLK = Q_ROWS * GRID_W
KEY_ROWS = 12
KEY_BLK = KEY_ROWS * GRID_W
N_BIAS_CASES = 5
BIAS_CASE_BLOCK = (0, 1, 2, 6, 7)
BIAS_LANES = 128

V7X_VMEM_BYTES = 64 * 1024 * 1024
VMEM_LIMIT = V7X_VMEM_BYTES * 7 // 8
LANES = 128
PROJ_TM = 256
PROJ_CHUNK = 512
MERGE_TM = 512
ADALN_TN = 1024
KEY_TILE = Q_BLK


def _sigmoid(v):
    return 1.0 / (1.0 + jnp.exp(-v))


def _dot(a, b):
    return jnp.dot(a, b, preferred_element_type=F32)


def _dot_nt(a, b):
    return lax.dot_general(a, b, (((1,), (1,)), ((), ())), preferred_element_type=F32)


def _adaln_kernel(cond_ref, w_ref, b_ref, o_ref):
    cnd = cond_ref[...]
    act = cnd * _sigmoid(cnd)
    o_ref[...] = jnp.dot(act, w_ref[...], preferred_element_type=F32,
                         precision=lax.Precision.HIGHEST) + b_ref[...]


def _adaln(cond, w_mod, b_mod):
    rows, d = cond.shape
    n = w_mod.shape[1]
    tn = ADALN_TN
    return pl.pallas_call(
        _adaln_kernel,
        grid=(n // tn,),
        in_specs=[pl.BlockSpec((rows, d), lambda j: (0, 0)),
                  pl.BlockSpec((d, tn), lambda j: (0, j)),
                  pl.BlockSpec((1, tn), lambda j: (0, j))],
        out_specs=pl.BlockSpec((rows, tn), lambda j: (0, j)),
        out_shape=jax.ShapeDtypeStruct((rows, n), F32),
        name="adaln",
    )(cond, w_mod, b_mod.reshape(1, n))


W_TILE = 512
DT_PAD = 128


def _wprep_kernel(a_ref, b_ref, dt_ref, w_ref, wdt_ref, *, n_aligned, shift, q_tiles, q_scale):
    j = pl.program_id(0)

    @pl.when(j < n_aligned)
    def _():
        scale = jnp.where(j < q_tiles, q_scale, 1.0)
        w_ref[...] = (a_ref[...] * scale).T.astype(BF16)

    @pl.when(j >= n_aligned)
    def _():
        w_ref[...] = jnp.concatenate([a_ref[shift:, :], b_ref[...]], axis=0).T.astype(BF16)

    @pl.when(j == 0)
    def _():
        dst = lax.broadcasted_iota(jnp.int32, (DT_PAD, DT_PAD), 1)
        src = lax.broadcasted_iota(jnp.int32, (DT_PAD, DT_PAD), 0)
        per_group = N_DIRS * HEADS_PER_GROUP
        grp, rem = dst // per_group, dst % per_group
        want = (rem // HEADS_PER_GROUP) * SSM_HEADS + grp * HEADS_PER_GROUP + rem % HEADS_PER_GROUP
        perm = jnp.where((src == want) & (dst < N_DIRS * SSM_HEADS), 1.0, 0.0).astype(BF16)
        wdt_ref[...] = _dot(dt_ref[...].T.astype(BF16), perm).astype(BF16)


def _prep_w_in(w_in_t, col_dt, n_dt, q_cols, q_scale):
    p, d = w_in_t.shape
    n_out = p - n_dt
    assert col_dt % W_TILE == 0 and n_out % W_TILE == 0 and q_cols % W_TILE == 0
    assert W_TILE % n_dt == 0 and n_dt % 8 == 0 and n_dt <= DT_PAD and col_dt % DT_PAD == 0
    return pl.pallas_call(
        functools.partial(_wprep_kernel, n_aligned=col_dt // W_TILE, shift=n_dt, q_tiles=q_cols // W_TILE,
                          q_scale=q_scale),
        grid=(n_out // W_TILE,),
        in_specs=[pl.BlockSpec((W_TILE, d), lambda j: (j, 0)),
                  pl.BlockSpec((n_dt, d), lambda j: ((j + 1) * (W_TILE // n_dt), 0)),
                  pl.BlockSpec((DT_PAD, d), lambda j: (col_dt // DT_PAD, 0))],
        out_specs=[pl.BlockSpec((d, W_TILE), lambda j: (0, j)),
                   pl.BlockSpec((d, DT_PAD), lambda j: (0, 0))],
        out_shape=[jax.ShapeDtypeStruct((d, n_out), BF16), jax.ShapeDtypeStruct((d, DT_PAD), BF16)],
        name="w_in_prep",
    )(w_in_t, w_in_t, w_in_t)


def _proj_kernel(x_ref, shift_ref, scale_ref, g_ref, w_ref, wdt_ref, p_ref, dtt_ref, *, chunks):
    x = x_ref[...]
    ms = jnp.mean(x * x, axis=-1, keepdims=True)
    h = x * lax.rsqrt(ms + EPS) * g_ref[...]
    h = h * (1.0 + scale_ref[...]) + shift_ref[...]
    hb = h.astype(BF16)
    for src, dst, width in chunks:
        p_ref[:, dst:dst + width] = _dot(hb, w_ref[:, src:src + width]).astype(BF16)
    dt = _dot(hb, wdt_ref[...])
    dtt_ref[...] = dt.T[:N_DIRS * SSM_HEADS]


def _projection(x3, shift, scale, g_pre, w_main, w_dt, col_ranges, tm):
    b, l, d = x3.shape
    chunks, dst = [], 0
    cw = PROJ_CHUNK
    for lo, hi in col_ranges:
        for src in range(lo, hi, cw):
            chunks.append((src, dst, cw))
            dst += cw
    n_out = dst
    n_dt = N_DIRS * SSM_HEADS
    return pl.pallas_call(
        functools.partial(_proj_kernel, chunks=tuple(chunks)),
        grid=(b, l // tm),
        in_specs=[pl.BlockSpec((None, tm, d), lambda i, j: (i, j, 0)),
                  pl.BlockSpec((None, 1, d), lambda i, j: (i, 0, 0)),
                  pl.BlockSpec((None, 1, d), lambda i, j: (i, 0, 0)),
                  pl.BlockSpec((1, d), lambda i, j: (0, 0)),
                  pl.BlockSpec(w_main.shape, lambda i, j: (0, 0), pipeline_mode=pl.Buffered(1)),
                  pl.BlockSpec(w_dt.shape, lambda i, j: (0, 0), pipeline_mode=pl.Buffered(1))],
        out_specs=[pl.BlockSpec((None, tm, n_out), lambda i, j: (i, j, 0)),
                   pl.BlockSpec((None, n_dt, tm), lambda i, j: (i, 0, j))],
        out_shape=[jax.ShapeDtypeStruct((b, l, n_out), BF16),
                   jax.ShapeDtypeStruct((b, n_dt, l), F32)],
        compiler_params=pltpu.CompilerParams(
            dimension_semantics=("parallel", "parallel"), vmem_limit_bytes=VMEM_LIMIT),
        name="projection",
    )(x3, shift, scale, g_pre, w_main, w_dt)


def _bias_block_plan():
    rows = 2048 // GRID_W
    plan = np.full((N_BIAS_CASES, Q_ROWS, KEY_ROWS), -1, np.int64)
    for case, blk in enumerate(BIAS_CASE_BLOCK):
        u0 = int(np.clip(Q_ROWS * blk - WIN_ROWS // 2, 0, rows - KEY_ROWS))
        for rho in range(Q_ROWS):
            r = Q_ROWS * blk + rho
            r0 = int(np.clip(r - WIN_ROWS // 2, 0, rows - WIN_ROWS))
            for a in range(KEY_ROWS):
                krow = u0 + a
                if r0 <= krow < r0 + WIN_ROWS:
                    plan[case, rho, a] = krow - r + WIN_ROWS - 1
    return plan


def _bias_kernel(rpb_ref, o_ref, toep_ref, *, plan):
    n_dr = 2 * WIN_ROWS - 1
    qc = lax.broadcasted_iota(jnp.int32, (GRID_W, GRID_W), 0)
    kc = lax.broadcasted_iota(jnp.int32, (GRID_W, GRID_W), 1)
    c0 = jnp.clip(qc - WIN_COLS // 2, 0, GRID_W - WIN_COLS)
    in_win = (kc >= c0) & (kc < c0 + WIN_COLS)
    for dr in range(n_dr):
        rows = jnp.broadcast_to(rpb_ref[dr:dr + 1, :], (GRID_W, BIAS_LANES))
        toep = pltpu.roll(rows, BIAS_LANES - (WIN_COLS - 1), 1, stride=1, stride_axis=0)[:, :GRID_W]
        toep_ref[dr] = jnp.where(in_win, toep * LOG2E, NEG)
    neg = jnp.full((GRID_W, GRID_W), NEG, F32)
    for case in range(N_BIAS_CASES):
        for rho in range(Q_ROWS):
            for a in range(KEY_ROWS):
                dr = int(plan[case, rho, a])
                val = neg if dr < 0 else toep_ref[dr]
                o_ref[case, rho * GRID_W:(rho + 1) * GRID_W, a * GRID_W:(a + 1) * GRID_W] = val.astype(BF16)


def _bias_table(rpb):
    plan = _bias_block_plan()
    heads, n_dr, n_dc = rpb.shape
    rpb_lanes = jnp.pad(rpb, ((0, 0), (0, 0), (0, BIAS_LANES - n_dc)))
    return pl.pallas_call(
        functools.partial(_bias_kernel, plan=plan),
        grid=(heads,),
        in_specs=[pl.BlockSpec((None, n_dr, BIAS_LANES), lambda h: (h, 0, 0))],
        out_specs=pl.BlockSpec((N_BIAS_CASES, None, Q_BLK, KEY_BLK), lambda h: (0, h, 0, 0)),
        out_shape=jax.ShapeDtypeStruct((N_BIAS_CASES, heads, Q_BLK, KEY_BLK), BF16),
        scratch_shapes=[pltpu.VMEM((n_dr, GRID_W, GRID_W), F32)],
        name="bias_table",
    )(rpb_lanes)


def _na_kernel(q_ref, k0_ref, k1_ref, k2_ref, kc_ref, v0_ref, v1_ref, v2_ref, vc_ref, z_ref, bias_ref, o_ref):
    lane = lax.broadcasted_iota(jnp.int32, (1, 2 * NA_HEAD_DIM), 1)
    k_refs = (k0_ref, k1_ref, k2_ref, kc_ref)
    v_refs = (v0_ref, v1_ref, v2_ref, vc_ref)
    ones_tile = jnp.where(lax.broadcasted_iota(jnp.int32, (Q_BLK, 2 * NA_HEAD_DIM), 1) == 0, 1.0, 0.0).astype(BF16)
    for pair in range(NA_HEADS // 2):
        cs = slice(pair * LANES, (pair + 1) * LANES)
        qp = q_ref[:, cs]
        q_stack = jnp.concatenate([jnp.where(lane < NA_HEAD_DIM, qp, jnp.zeros_like(qp)),
                                   jnp.where(lane >= NA_HEAD_DIM, qp, jnp.zeros_like(qp))], axis=0)
        s_both = [_dot_nt(q_stack, r[:, cs]) for r in k_refs]
        e_both = [[], []]
        for hh in range(2):
            s = [sb[hh * Q_BLK:(hh + 1) * Q_BLK] for sb in s_both]
            for t in range(KEY_ROWS // Q_ROWS):
                s[t] = s[t] + bias_ref[2 * pair + hh, :, t * KEY_TILE:(t + 1) * KEY_TILE].astype(F32)
            mx = jnp.max(jnp.maximum(jnp.maximum(s[0], s[1]), jnp.maximum(s[2], s[3])), axis=-1, keepdims=True)
            e_both[hh] = [jnp.exp2((st - mx).astype(BF16)) for st in s]
        o = None
        for t, r in enumerate(v_refs):
            p_stack = jnp.concatenate([e_both[0][t], e_both[1][t]], axis=0)
            c = _dot(p_stack, jnp.concatenate([r[:, cs], ones_tile], axis=1))
            o = c if o is None else o + c
        outs = [o[hh * Q_BLK:(hh + 1) * Q_BLK, :LANES] * (1.0 / o[hh * Q_BLK:(hh + 1) * Q_BLK, LANES:LANES + 1])
                for hh in range(2)]
        acc = jnp.where(lane < NA_HEAD_DIM, outs[0], outs[1])
        z = z_ref[:, cs].astype(F32)
        o_ref[:, cs] = (acc * (z * _sigmoid(z))).astype(BF16)


def _neighborhood_attention(p3, pc3, bias_tab):
    b, s, _ = p3.shape
    n_blk = s // Q_BLK

    def key_blk(i):
        return jnp.clip(i - 1, 0, n_blk - KEY_ROWS // Q_ROWS)

    def case_of(i):
        return jnp.minimum(i, 2) + jnp.maximum(i - (n_blk - 3), 0)

    cq, ck, cv, cz = COL_Q // 1024, COL_K // 1024, COL_V // 1024, COL_ZNA // 1024
    blk = (None, Q_BLK, 1024)
    in_specs = [pl.BlockSpec(blk, lambda i, j: (j, i, cq))]
    in_specs += [pl.BlockSpec(blk, functools.partial(lambda i, j, t: (j, key_blk(i) + t, ck), t=t)) for t in range(3)]
    in_specs += [pl.BlockSpec(blk, lambda i, j: (j, 0, CCOL_K // 1024))]
    in_specs += [pl.BlockSpec(blk, functools.partial(lambda i, j, t: (j, key_blk(i) + t, cv), t=t)) for t in range(3)]
    in_specs += [pl.BlockSpec(blk, lambda i, j: (j, 0, CCOL_V // 1024))]
    in_specs += [pl.BlockSpec(blk, lambda i, j: (j, i, cz))]
    in_specs += [pl.BlockSpec((None, NA_HEADS, Q_BLK, KEY_BLK), lambda i, j: (case_of(i), 0, 0, 0))]
    return pl.pallas_call(
        _na_kernel,
        grid=(n_blk, b),
        in_specs=in_specs,
        out_specs=pl.BlockSpec(blk, lambda i, j: (j, i, 0)),
        out_shape=jax.ShapeDtypeStruct((b, s, NA_HEADS * NA_HEAD_DIM), BF16),
        compiler_params=pltpu.CompilerParams(
            dimension_semantics=("parallel", "parallel"), vmem_limit_bytes=VMEM_LIMIT),
        name="neighborhood_attention",
    )(p3, p3, p3, p3, pc3, p3, p3, p3, pc3, p3, bias_tab)


def _softplus(v):
    return jnp.maximum(v, 0.0) + jnp.log(1.0 + jnp.exp(-jnp.abs(v)))


def _head_stack(v, head_lanes):
    return jnp.concatenate([jnp.where(m, v, jnp.zeros_like(v)) for m in head_lanes], axis=0)


def _ssd_kernel(xs_ref, bm_ref, cm_ref, z_ref, xsc_ref, bmc_ref, dtt_ref, dttc_ref,
                cwx_ref, cwb_ref, cwc_ref, cbx_ref, cbb_ref, cbc_ref,
                alog_ref, dtb_ref, skip_ref, ng_ref, o_ref,
                pad_ref, padc_ref, tap_ref, ux_ref, ubt_ref, uc_ref, uxc_ref, ubtc_ref,
                rows_ref, cols_ref, dec_ref, rowsc_ref, decc_ref, y_ref, st_ref, e_ref):
    q = SSM_CHUNK
    hpg = HEADS_PER_GROUP
    nh = N_DIRS * hpg
    s_len = xs_ref.shape[0]
    c_len = xsc_ref.shape[0]
    n_lat = s_len // q
    n_ctx = c_len // q
    halo = CONV_HALO
    blk = CONV_BLK
    win = blk + 2 * halo

    zeros_halo = jnp.zeros((halo, pad_ref.shape[1]), BF16)
    pad_ref[0:halo, :] = zeros_halo
    pad_ref[s_len + halo:s_len + 2 * halo, :] = zeros_halo
    pad_ref[halo:s_len + halo, 0:GROUP_X] = xs_ref[...]
    pad_ref[halo:s_len + halo, GROUP_X:GROUP_X + SSM_STATE] = bm_ref[...]
    pad_ref[halo:s_len + halo, GROUP_X + SSM_STATE:] = cm_ref[...]
    zeros_halo_c = jnp.zeros((halo, padc_ref.shape[1]), BF16)
    padc_ref[0:halo, :] = zeros_halo_c
    padc_ref[c_len + halo:c_len + 2 * halo, :] = zeros_halo_c
    padc_ref[halo:c_len + halo, 0:GROUP_X] = xsc_ref[...]
    padc_ref[halo:c_len + halo, GROUP_X:] = bmc_ref[...]

    cw = jnp.concatenate([cwx_ref[...], cwb_ref[...], cwc_ref[...]], axis=1)
    cb = jnp.concatenate([cbx_ref[...], cbb_ref[...], cbc_ref[...]], axis=1)
    for k in range(SSM_CONV):
        tap_ref[k] = jnp.broadcast_to(cw[k:k + 1, :], (win, cw.shape[1])).astype(BF16)
    off = lax.broadcasted_iota(jnp.int32, (blk, SSM_CONV * win), 1) - lax.broadcasted_iota(
        jnp.int32, (blk, SSM_CONV * win), 0)
    hit = off == halo - SSM_CONV // 2
    for k in range(1, SSM_CONV):
        hit = hit | (off == k * win + halo - SSM_CONV // 2 + k)
    shift = jnp.where(hit, 1.0, 0.0).astype(BF16)

    def conv_block(src_ref, start, width):
        w = src_ref[pl.ds(start, win), :]
        stack = jnp.concatenate([w * tap_ref[k, :, 0:width] for k in range(SSM_CONV)], axis=0)
        acc = _dot(shift, stack) + cb[:, :width]
        return acc * _sigmoid(acc)

    def conv_chunk(src_ref, start, width):
        return jnp.concatenate([conv_block(src_ref, start + i * blk, width) for i in range(q // blk)], axis=0)

    def conv_lat(c, carry):
        start = pl.multiple_of(c * q, q)
        u = conv_chunk(pad_ref, start, GROUP_X + 2 * SSM_STATE)
        ux_ref[pl.ds(start, q), :] = u[:, :GROUP_X]
        ubt_ref[c] = u[:, GROUP_X:GROUP_X + SSM_STATE].T.astype(BF16)
        uc_ref[pl.ds(start, q), :] = u[:, GROUP_X + SSM_STATE:].astype(BF16)
        return carry

    lax.fori_loop(0, n_lat, conv_lat, 0, unroll=4)
    for c in range(n_ctx):
        u = conv_chunk(padc_ref, c * q, GROUP_X + SSM_STATE)
        uxc_ref[c * q:(c + 1) * q, :] = u[:, :GROUP_X]
        ubtc_ref[c] = u[:, GROUP_X:].T.astype(BF16)

    ki = lax.broadcasted_iota(jnp.int32, (q, q), 0)
    ii = lax.broadcasted_iota(jnp.int32, (q, q), 1)
    tri_f = (ki <= ii).astype(F32)
    tri_b = (ki >= ii).astype(F32)
    lane_x = lax.broadcasted_iota(jnp.int32, (1, GROUP_X), 1)
    head_lanes = [(lane_x >= r * SSM_HEAD_DIM) & (lane_x < (r + 1) * SSM_HEAD_DIM) for r in range(hpg)]
    neg_a2 = -jnp.exp(alog_ref[...]) * LOG2E
    dt_bias = dtb_ref[...]

    def dt_forms(raw, n, rows_out, cols_out, dec_out):
        stack = lambda v: jnp.concatenate([v[:, c * q:(c + 1) * q] for c in range(n)], axis=0)
        dt_all = _softplus(raw + dt_bias)
        dt = stack(dt_all)
        a = stack(neg_a2 * dt_all)
        is_fwd = lax.broadcasted_iota(jnp.int32, (n * nh, 1), 0) % nh < hpg
        cum_f = jnp.dot(a, tri_f, preferred_element_type=F32, precision=lax.Precision.HIGHEST)
        cum_b = jnp.dot(a, tri_b, preferred_element_type=F32, precision=lax.Precision.HIGHEST)
        acum = jnp.where(is_fwd, cum_f, cum_b)
        atot = jnp.where(is_fwd, acum[:, q - 1:q], acum[:, 0:1])
        sw = dt * jnp.exp2(atot - acum)
        dec = jnp.exp2(atot)
        for c in range(n):
            sl = slice(c * nh, (c + 1) * nh)
            rows_out[c] = jnp.concatenate([dt[sl], acum[sl], sw[sl]], axis=0)
            if cols_out is not None:
                cols_out[c] = acum[sl].T
            dec_rows = []
            for d in range(N_DIRS):
                base = c * nh + d * hpg
                row = jnp.broadcast_to(dec[base + hpg - 1:base + hpg, :], (1, GROUP_X))
                for r in range(hpg - 2, -1, -1):
                    row = jnp.where(lane_x < (r + 1) * SSM_HEAD_DIM, dec[base + r:base + r + 1, :], row)
                dec_rows.append(row)
            dec_out[c] = jnp.concatenate(dec_rows, axis=0)

    dt_forms(dtt_ref[...], n_lat, rows_ref, cols_ref, dec_ref)
    dt_forms(dttc_ref[...], n_ctx, rowsc_ref, None, decc_ref)

    def state_term(d, rows_b, bt, xm):
        parts = [bt * rows_b[2 * N_DIRS * hpg + d * hpg + r:2 * N_DIRS * hpg + d * hpg + r + 1, :]
                 for r in range(hpg)]
        return _dot(jnp.concatenate(parts, axis=1), xm)

    h_init = []
    for d in range(N_DIRS):
        h = jnp.zeros((SSM_STATE, GROUP_X), F32)
        for c in (range(n_ctx) if d == 0 else reversed(range(n_ctx))):
            xm = _head_stack(uxc_ref[c * q:(c + 1) * q, :].astype(BF16), head_lanes)
            h = decc_ref[c][d:d + 1, :] * h + state_term(d, rowsc_ref[c].astype(BF16), ubtc_ref[c], xm)
        h_init.append(h)

    skip = skip_ref[...]
    norm_g = ng_ref[...]
    keep = (ki >= ii, ki <= ii)
    low_lanes = lax.broadcasted_iota(jnp.int32, (1, q), 1) < SSM_HEAD_DIM

    def chunk_body(c, carry):
        start = pl.multiple_of(c * q, q)
        rows = rows_ref[c]
        cols = cols_ref[c]
        xs = ux_ref[pl.ds(start, q), :]
        xm = _head_stack(xs.astype(BF16), head_lanes)
        cmat = uc_ref[pl.ds(start, q), :]
        bt = ubt_ref[c]
        cbm = _dot(cmat, bt)
        cbm_b = cbm.astype(BF16)
        rows_b = rows.astype(BF16)
        zero_b = jnp.zeros((q, q), BF16)
        m_parts = []
        acols = [[], []]
        for r in range(hpg):
            t = None
            for d in range(N_DIRS):
                k = d * hpg + r
                acol = jnp.broadcast_to(cols[:, k:k + 1], (q, q))
                arow = rows[N_DIRS * hpg + k:N_DIRS * hpg + k + 1, :]
                term = jnp.where(keep[d], jnp.exp2((acol - arow).astype(BF16)) * rows_b[k:k + 1, :], zero_b)
                t = term if t is None else t + term
                acols[d].append(acol)
            m_parts.append(cbm_b * t)
        y_ref[pl.ds(start, q), :] = xs * skip + _dot(jnp.concatenate(m_parts, axis=1), xm)
        for d in range(N_DIRS):
            st_ref[d, c] = state_term(d, rows_b, bt, xm)
            halves = [jnp.where(low_lanes, acols[d][2 * i], acols[d][2 * i + 1]) for i in range(hpg // 2)]
            e_ref[d, c] = jnp.exp2(jnp.concatenate(halves, axis=1))
        return carry

    lax.fori_loop(0, n_lat, chunk_body, 0, unroll=4)

    def scan_body(j, carry, finalize):
        new = []
        chunk_ids = (j, n_lat - 1 - j)
        for d in range(N_DIRS):
            h = carry[d]
            c = chunk_ids[d]
            start = pl.multiple_of(c * q, q)
            y_off = _dot(uc_ref[pl.ds(start, q), :], h.astype(BF16)) * e_ref[d, c]
            y_ref[pl.ds(start, q), :] = y_ref[pl.ds(start, q), :] + y_off
            new.append(dec_ref[c][d:d + 1, :] * h + st_ref[d, c])
        if finalize:
            for c in chunk_ids:
                start = pl.multiple_of(c * q, q)
                z = z_ref[pl.ds(start, q), :].astype(F32)
                u = y_ref[pl.ds(start, q), :] * (z * _sigmoid(z))
                u = u * lax.rsqrt(jnp.mean(u * u, axis=-1, keepdims=True) + EPS)
                o_ref[pl.ds(start, q), :] = (u * norm_g).astype(BF16)
        return tuple(new)

    half = n_lat // 2
    carry = lax.fori_loop(0, half, functools.partial(scan_body, finalize=False), tuple(h_init), unroll=2)
    lax.fori_loop(half, n_lat, functools.partial(scan_body, finalize=True), carry, unroll=2)


def _ssd_mixer(p3, pc3, dtt, dttc, conv_w, conv_b, alog8, dtb8, skip_exp, norm_g):
    b, s, _ = p3.shape
    c_len = pc3.shape[1]
    g = SSM_GROUPS
    n_lat, n_ctx = s // SSM_CHUNK, c_len // SSM_CHUNK
    assert n_lat % 2 == 0
    nx = GROUP_X
    ns = SSM_STATE
    nh = N_DIRS * HEADS_PER_GROUP
    in_specs = [
        pl.BlockSpec((None, s, nx), lambda i, j: (i, 0, COL_X // nx + j)),
        pl.BlockSpec((None, s, ns), lambda i, j: (i, 0, COL_B // ns + j)),
        pl.BlockSpec((None, s, ns), lambda i, j: (i, 0, COL_C // ns + j)),
        pl.BlockSpec((None, s, nx), lambda i, j: (i, 0, COL_ZSSM // nx + j)),
        pl.BlockSpec((None, c_len, nx), lambda i, j: (i, 0, CCOL_X // nx + j)),
        pl.BlockSpec((None, c_len, ns), lambda i, j: (i, 0, CCOL_B // ns + j)),
        pl.BlockSpec((None, nh, s), lambda i, j: (i, j, 0)),
        pl.BlockSpec((None, nh, c_len), lambda i, j: (i, j, 0)),
        pl.BlockSpec((SSM_CONV, nx), lambda i, j: (0, j)),
        pl.BlockSpec((SSM_CONV, ns), lambda i, j: (0, SSM_WIDTH // ns + j)),
        pl.BlockSpec((SSM_CONV, ns), lambda i, j: (0, SSM_WIDTH // ns + g + j)),
        pl.BlockSpec((1, nx), lambda i, j: (0, j)),
        pl.BlockSpec((1, ns), lambda i, j: (0, SSM_WIDTH // ns + j)),
        pl.BlockSpec((1, ns), lambda i, j: (0, SSM_WIDTH // ns + g + j)),
        pl.BlockSpec((None, nh, 1), lambda i, j: (j, 0, 0)),
        pl.BlockSpec((None, nh, 1), lambda i, j: (j, 0, 0)),
        pl.BlockSpec((None, 1, nx), lambda i, j: (j, 0, 0)),
        pl.BlockSpec((1, nx), lambda i, j: (0, j)),
    ]
    halo = CONV_HALO
    scratch = [
        pltpu.VMEM((s + 2 * halo, nx + 2 * ns), BF16),
        pltpu.VMEM((c_len + 2 * halo, nx + ns), BF16),
        pltpu.VMEM((SSM_CONV, CONV_BLK + 2 * halo, nx + 2 * ns), BF16),
        pltpu.VMEM((s, nx), F32),
        pltpu.VMEM((n_lat, ns, SSM_CHUNK), BF16),
        pltpu.VMEM((s, ns), BF16),
        pltpu.VMEM((c_len, nx), F32),
        pltpu.VMEM((n_ctx, ns, SSM_CHUNK), BF16),
        pltpu.VMEM((n_lat, 3 * nh, SSM_CHUNK), F32),
        pltpu.VMEM((n_lat, SSM_CHUNK, nh), F32),
        pltpu.VMEM((n_lat, N_DIRS, nx), F32),
        pltpu.VMEM((n_ctx, 3 * nh, SSM_CHUNK), F32),
        pltpu.VMEM((n_ctx, N_DIRS, nx), F32),
        pltpu.VMEM((s, nx), F32),
        pltpu.VMEM((N_DIRS, n_lat, ns, nx), F32),
        pltpu.VMEM((N_DIRS, n_lat, SSM_CHUNK, nx), F32),
    ]
    return pl.pallas_call(
        _ssd_kernel,
        grid=(b, g),
        in_specs=in_specs,
        out_specs=pl.BlockSpec((None, s, nx), lambda i, j: (i, 0, j)),
        out_shape=jax.ShapeDtypeStruct((b, s, SSM_WIDTH), BF16),
        scratch_shapes=scratch,
        compiler_params=pltpu.CompilerParams(
            dimension_semantics=("parallel", "parallel"), vmem_limit_bytes=VMEM_LIMIT),
        name="ssd_mixer",
    )(p3, p3, p3, p3, pc3, pc3, dtt, dttc, conv_w, conv_w, conv_w, conv_b, conv_b, conv_b,
      alog8, dtb8, skip_exp, norm_g)


def _merge_kernel(yna_ref, yssm_ref, gna_ref, gssm_ref, x_ref, gate_ref, wna_ref, wssm_ref, wout_ref, gpost_ref, o_ref):
    a = _dot(yna_ref[...], wna_ref[...])
    s = _dot(yssm_ref[...], wssm_ref[...])
    m = _sigmoid(gna_ref[...].astype(F32)) * a + _sigmoid(gssm_ref[...].astype(F32)) * s
    o = _dot(m.astype(BF16), wout_ref[...])
    r = o * lax.rsqrt(jnp.mean(o * o, axis=-1, keepdims=True) + EPS) * gpost_ref[...]
    o_ref[...] = x_ref[...] + gate_ref[...] * r


def _merge(y_na, y_ssm, p3, x3, gate, w_na, w_ssm, w_out, g_post, tm):
    b, s, d = x3.shape
    const = lambda i, j: (0, 0)
    return pl.pallas_call(
        _merge_kernel,
        grid=(b, s // tm),
        in_specs=[pl.BlockSpec((None, tm, y_na.shape[-1]), lambda i, j: (i, j, 0)),
                  pl.BlockSpec((None, tm, y_ssm.shape[-1]), lambda i, j: (i, j, 0)),
                  pl.BlockSpec((None, tm, d), lambda i, j: (i, j, COL_GNA // d)),
                  pl.BlockSpec((None, tm, d), lambda i, j: (i, j, COL_GSSM // d)),
                  pl.BlockSpec((None, tm, d), lambda i, j: (i, j, 0)),
                  pl.BlockSpec((None, 1, d), lambda i, j: (i, 0, 0)),
                  pl.BlockSpec(w_na.shape, const),
                  pl.BlockSpec(w_ssm.shape, const),
                  pl.BlockSpec(w_out.shape, const),
                  pl.BlockSpec((1, d), const)],
        out_specs=pl.BlockSpec((None, tm, d), lambda i, j: (i, j, 0)),
        out_shape=jax.ShapeDtypeStruct((b, s, d), x3.dtype),
        compiler_params=pltpu.CompilerParams(
            dimension_semantics=("parallel", "parallel"), vmem_limit_bytes=VMEM_LIMIT),
        name="merge_out",
    )(y_na, y_ssm, p3, p3, x3, gate, w_na, w_ssm, w_out, g_post)


def kernel(x, c, ctx, c_ctx, w_mod, b_mod, g_pre, g_post, w_in, conv_w, conv_b, a_log, dt_bias, d_skip,
           ssm_norm_g, rpb, w_na_out, w_ssm_out, w_out):
    assert w_mod.shape[0] == 1, "single-layer block"
    b, s, d = x.shape
    g, hpg = SSM_GROUPS, HEADS_PER_GROUP

    w_main, w_dt = _prep_w_in(jnp.transpose(w_in[0]), COL_GNA, N_DIRS * SSM_HEADS, COL_K, NA_HEAD_DIM ** -0.5 * LOG2E)
    per_group = lambda p: jnp.transpose(p.reshape(N_DIRS, g, hpg), (1, 0, 2)).reshape(g, N_DIRS * hpg, 1)
    alog8 = per_group(a_log[0])
    dtb8 = per_group(dt_bias[0])
    skip_exp = jnp.repeat(d_skip[0], SSM_HEAD_DIM).reshape(g, 1, GROUP_X)

    cond = jnp.zeros((16, d), F32).at[:b].set(c).at[b].set(c_ctx)
    mod = _adaln(cond, w_mod[0], b_mod[0])
    shift_l, scale_l, gate_l = (mod[:b, k * d:(k + 1) * d].reshape(b, 1, d) for k in range(3))
    shift_c, scale_c = (jnp.broadcast_to(mod[b, k * d:(k + 1) * d].reshape(1, 1, d), (b, 1, d)) for k in range(2))

    p3, dtt = _projection(x, shift_l, scale_l, g_pre, w_main, w_dt, ((0, NP_COLS),), tm=PROJ_TM)
    pc3, dttc = _projection(ctx, shift_c, scale_c, g_pre, w_main, w_dt,
                            ((COL_K, COL_ZNA), (COL_X, COL_GNA)), tm=PROJ_TM)

    bias_tab = _bias_table(rpb[0])
    y_na = _neighborhood_attention(p3, pc3, bias_tab)

    y_ssm = _ssd_mixer(p3, pc3, dtt, dttc, conv_w[0], conv_b[0].reshape(1, -1), alog8, dtb8, skip_exp,
                       ssm_norm_g)

    return _merge(y_na, y_ssm, p3, x, gate_l, w_na_out[0].astype(BF16), w_ssm_out[0].astype(BF16),
                  w_out[0].astype(BF16), g_post, tm=MERGE_TM)
```

```python
import functools

import numpy as np
import jax
import jax.numpy as jnp
from jax import lax
from jax.experimental import pallas as pl
from jax.experimental.pallas import tpu as pltpu

F32 = jnp.float32
BF16 = jnp.bfloat16

D_MODEL = 1024
GRID_W = 64
NA_HEADS = 16
NA_HEAD_DIM = 64
WIN_ROWS = 8
WIN_COLS = 16
SSM_WIDTH = 2 * D_MODEL
SSM_HEAD_DIM = 64
SSM_HEADS = SSM_WIDTH // SSM_HEAD_DIM
SSM_GROUPS = 8
HEADS_PER_GROUP = SSM_HEADS // SSM_GROUPS
SSM_STATE = 128
SSM_CONV = 5
SSM_CHUNK = 128
CONV_HALO = 16
CONV_BLK = 64
SSD_UNROLL = 8
N_DIRS = 2
GROUP_X = HEADS_PER_GROUP * SSM_HEAD_DIM
EPS = 1e-6
NEG = -1e30
LOG2E = 1.4426950408889634

COL_Q, COL_K, COL_V, COL_ZNA = 0, 1024, 2048, 3072
COL_ZSSM, COL_X, COL_B, COL_C = 4096, 6144, 8192, 9216
COL_GNA, COL_GSSM, NP_COLS = 10240, 11264, 12288
CCOL_K, CCOL_V, CCOL_X, CCOL_B, NPC_COLS = 0, 1024, 2048, 4096, 6144

Q_ROWS = 4
Q_BLK = Q_ROWS * GRID_W
KEY_ROWS = 12
KEY_BLK = KEY_ROWS * GRID_W
N_BIAS_CASES = 5
BIAS_CASE_BLOCK = (0, 1, 2, 6, 7)
BIAS_LANES = 128

V7X_VMEM_BYTES = 64 * 1024 * 1024
VMEM_LIMIT = V7X_VMEM_BYTES * 7 // 8
LANES = 128
PROJ_TM = 256
PROJ_CHUNK = 512
MERGE_TM = 512
ADALN_TN = 1024
KEY_TILE = Q_BLK


def _sigmoid(v):
    return 1.0 / (1.0 + jnp.exp(-v))


def _dot(a, b):
    return jnp.dot(a, b, preferred_element_type=F32)


def _dot_nt(a, b):
    return lax.dot_general(a, b, (((1,), (1,)), ((), ())), preferred_element_type=F32)


def _adaln_kernel(cond_ref, w_ref, b_ref, o_ref):
    cnd = cond_ref[...]
    act = cnd * _sigmoid(cnd)
    o_ref[...] = jnp.dot(act, w_ref[...], preferred_element_type=F32,
                         precision=lax.Precision.HIGHEST) + b_ref[...]


def _adaln(cond, w_mod, b_mod):
    rows, d = cond.shape
    n = w_mod.shape[1]
    tn = ADALN_TN
    return pl.pallas_call(
        _adaln_kernel,
        grid=(n // tn,),
        in_specs=[pl.BlockSpec((rows, d), lambda j: (0, 0)),
                  pl.BlockSpec((d, tn), lambda j: (0, j)),
                  pl.BlockSpec((1, tn), lambda j: (0, j))],
        out_specs=pl.BlockSpec((rows, tn), lambda j: (0, j)),
        out_shape=jax.ShapeDtypeStruct((rows, n), F32),
        name="adaln",
    )(cond, w_mod, b_mod.reshape(1, n))


W_TILE = 512
DT_PAD = 128


def _wprep_kernel(a_ref, b_ref, dt_ref, w_ref, wdt_ref, *, n_aligned, shift, q_tiles, q_scale):
    j = pl.program_id(0)

    @pl.when(j < n_aligned)
    def _():
        scale = jnp.where(j < q_tiles, q_scale, 1.0)
        w_ref[...] = (a_ref[...] * scale).T.astype(BF16)

    @pl.when(j >= n_aligned)
    def _():
        w_ref[...] = jnp.concatenate([a_ref[shift:, :], b_ref[...]], axis=0).T.astype(BF16)

    @pl.when(j == 0)
    def _():
        dst = lax.broadcasted_iota(jnp.int32, (DT_PAD, DT_PAD), 1)
        src = lax.broadcasted_iota(jnp.int32, (DT_PAD, DT_PAD), 0)
        per_group = N_DIRS * HEADS_PER_GROUP
        grp, rem = dst // per_group, dst % per_group
        want = (rem // HEADS_PER_GROUP) * SSM_HEADS + grp * HEADS_PER_GROUP + rem % HEADS_PER_GROUP
        perm = jnp.where((src == want) & (dst < N_DIRS * SSM_HEADS), 1.0, 0.0).astype(BF16)
        wdt_ref[...] = _dot(dt_ref[...].T.astype(BF16), perm).astype(BF16)


def _prep_w_in(w_in_t, col_dt, n_dt, q_cols, q_scale):
    p, d = w_in_t.shape
    n_out = p - n_dt
    assert col_dt % W_TILE == 0 and n_out % W_TILE == 0 and q_cols % W_TILE == 0
    assert W_TILE % n_dt == 0 and n_dt % 8 == 0 and n_dt <= DT_PAD and col_dt % DT_PAD == 0
    return pl.pallas_call(
        functools.partial(_wprep_kernel, n_aligned=col_dt // W_TILE, shift=n_dt, q_tiles=q_cols // W_TILE,
                          q_scale=q_scale),
        grid=(n_out // W_TILE,),
        in_specs=[pl.BlockSpec((W_TILE, d), lambda j: (j, 0)),
                  pl.BlockSpec((n_dt, d), lambda j: ((j + 1) * (W_TILE // n_dt), 0)),
                  pl.BlockSpec((DT_PAD, d), lambda j: (col_dt // DT_PAD, 0))],
        out_specs=[pl.BlockSpec((d, W_TILE), lambda j: (0, j)),
                   pl.BlockSpec((d, DT_PAD), lambda j: (0, 0))],
        out_shape=[jax.ShapeDtypeStruct((d, n_out), BF16), jax.ShapeDtypeStruct((d, DT_PAD), BF16)],
        name="w_in_prep",
    )(w_in_t, w_in_t, w_in_t)


def _proj_kernel(x_ref, shift_ref, scale_ref, g_ref, w_ref, wdt_ref, p_ref, dtt_ref, *, chunks):
    x = x_ref[...]
    ms = jnp.mean(x * x, axis=-1, keepdims=True)
    h = x * lax.rsqrt(ms + EPS) * g_ref[...]
    h = h * (1.0 + scale_ref[...]) + shift_ref[...]
    hb = h.astype(BF16)
    for src, dst, width in chunks:
        p_ref[:, dst:dst + width] = _dot(hb, w_ref[:, src:src + width]).astype(BF16)
    dt = _dot(hb, wdt_ref[...])
    dtt_ref[...] = dt.T[:N_DIRS * SSM_HEADS]


def _projection(x3, shift, scale, g_pre, w_main, w_dt, col_ranges, tm):
    b, l, d = x3.shape
    chunks, dst = [], 0
    cw = PROJ_CHUNK
    for lo, hi in col_ranges:
        for src in range(lo, hi, cw):
            chunks.append((src, dst, cw))
            dst += cw
    n_out = dst
    n_dt = N_DIRS * SSM_HEADS
    return pl.pallas_call(
        functools.partial(_proj_kernel, chunks=tuple(chunks)),
        grid=(b, l // tm),
        in_specs=[pl.BlockSpec((None, tm, d), lambda i, j: (i, j, 0)),
                  pl.BlockSpec((None, 1, d), lambda i, j: (i, 0, 0)),
                  pl.BlockSpec((None, 1, d), lambda i, j: (i, 0, 0)),
                  pl.BlockSpec((1, d), lambda i, j: (0, 0)),
                  pl.BlockSpec(w_main.shape, lambda i, j: (0, 0), pipeline_mode=pl.Buffered(1)),
                  pl.BlockSpec(w_dt.shape, lambda i, j: (0, 0), pipeline_mode=pl.Buffered(1))],
        out_specs=[pl.BlockSpec((None, tm, n_out), lambda i, j: (i, j, 0)),
                   pl.BlockSpec((None, n_dt, tm), lambda i, j: (i, 0, j))],
        out_shape=[jax.ShapeDtypeStruct((b, l, n_out), BF16),
                   jax.ShapeDtypeStruct((b, n_dt, l), F32)],
        compiler_params=pltpu.CompilerParams(
            dimension_semantics=("parallel", "parallel"), vmem_limit_bytes=VMEM_LIMIT),
        name="projection",
    )(x3, shift, scale, g_pre, w_main, w_dt)


def _bias_block_plan():
    rows = 2048 // GRID_W
    plan = np.full((N_BIAS_CASES, Q_ROWS, KEY_ROWS), -1, np.int64)
    for case, blk in enumerate(BIAS_CASE_BLOCK):
        u0 = int(np.clip(Q_ROWS * blk - WIN_ROWS // 2, 0, rows - KEY_ROWS))
        for rho in range(Q_ROWS):
            r = Q_ROWS * blk + rho
            r0 = int(np.clip(r - WIN_ROWS // 2, 0, rows - WIN_ROWS))
            for a in range(KEY_ROWS):
                krow = u0 + a
                if r0 <= krow < r0 + WIN_ROWS:
                    plan[case, rho, a] = krow - r + WIN_ROWS - 1
    return plan


def _bias_kernel(rpb_ref, o_ref, toep_ref, *, plan):
    n_dr = 2 * WIN_ROWS - 1
    qc = lax.broadcasted_iota(jnp.int32, (GRID_W, GRID_W), 0)
    kc = lax.broadcasted_iota(jnp.int32, (GRID_W, GRID_W), 1)
    c0 = jnp.clip(qc - WIN_COLS // 2, 0, GRID_W - WIN_COLS)
    in_win = (kc >= c0) & (kc < c0 + WIN_COLS)
    for dr in range(n_dr):
        rows = jnp.broadcast_to(rpb_ref[dr:dr + 1, :], (GRID_W, BIAS_LANES))
        toep = pltpu.roll(rows, BIAS_LANES - (WIN_COLS - 1), 1, stride=1, stride_axis=0)[:, :GRID_W]
        toep_ref[dr] = jnp.where(in_win, toep * LOG2E, NEG)
    neg = jnp.full((GRID_W, GRID_W), NEG, F32)
    for case in range(N_BIAS_CASES):
        for rho in range(Q_ROWS):
            for a in range(KEY_ROWS):
                dr = int(plan[case, rho, a])
                val = neg if dr < 0 else toep_ref[dr]
                o_ref[case, rho * GRID_W:(rho + 1) * GRID_W, a * GRID_W:(a + 1) * GRID_W] = val.astype(BF16)


def _bias_table(rpb):
    plan = _bias_block_plan()
    heads, n_dr, n_dc = rpb.shape
    rpb_lanes = jnp.pad(rpb, ((0, 0), (0, 0), (0, BIAS_LANES - n_dc)))
    return pl.pallas_call(
        functools.partial(_bias_kernel, plan=plan),
        grid=(heads,),
        in_specs=[pl.BlockSpec((None, n_dr, BIAS_LANES), lambda h: (h, 0, 0))],
        out_specs=pl.BlockSpec((N_BIAS_CASES, None, Q_BLK, KEY_BLK), lambda h: (0, h, 0, 0)),
        out_shape=jax.ShapeDtypeStruct((N_BIAS_CASES, heads, Q_BLK, KEY_BLK), BF16),
        scratch_shapes=[pltpu.VMEM((n_dr, GRID_W, GRID_W), F32)],
        name="bias_table",
    )(rpb_lanes)


def _na_kernel(q_ref, k0_ref, k1_ref, k2_ref, kc_ref, v0_ref, v1_ref, v2_ref, vc_ref, z_ref, bias_ref, o_ref):
    lane = lax.broadcasted_iota(jnp.int32, (1, 2 * NA_HEAD_DIM), 1)
    k_refs = (k0_ref, k1_ref, k2_ref, kc_ref)
    v_refs = (v0_ref, v1_ref, v2_ref, vc_ref)
    ones_tile = jnp.where(lax.broadcasted_iota(jnp.int32, (Q_BLK, 2 * NA_HEAD_DIM), 1) == 0, 1.0, 0.0).astype(BF16)
    for pair in range(NA_HEADS // 2):
        cs = slice(pair * LANES, (pair + 1) * LANES)
        qp = q_ref[:, cs]
        q_stack = jnp.concatenate([jnp.where(lane < NA_HEAD_DIM, qp, jnp.zeros_like(qp)),
                                   jnp.where(lane >= NA_HEAD_DIM, qp, jnp.zeros_like(qp))], axis=0)
        s_both = [_dot_nt(q_stack, r[:, cs]) for r in k_refs]
        e_both = [[], []]
        for hh in range(2):
            s = [sb[hh * Q_BLK:(hh + 1) * Q_BLK] for sb in s_both]
            for t in range(KEY_ROWS // Q_ROWS):
                s[t] = s[t] + bias_ref[2 * pair + hh, :, t * KEY_TILE:(t + 1) * KEY_TILE].astype(F32)
            mx = jnp.max(jnp.maximum(jnp.maximum(s[0], s[1]), jnp.maximum(s[2], s[3])), axis=-1, keepdims=True)
            e_both[hh] = [jnp.exp2((st - mx).astype(BF16)) for st in s]
        o = None
        for t, r in enumerate(v_refs):
            p_stack = jnp.concatenate([e_both[0][t], e_both[1][t]], axis=0)
            c = _dot(p_stack, jnp.concatenate([r[:, cs], ones_tile], axis=1))
            o = c if o is None else o + c
        outs = [o[hh * Q_BLK:(hh + 1) * Q_BLK, :LANES] * (1.0 / o[hh * Q_BLK:(hh + 1) * Q_BLK, LANES:LANES + 1])
                for hh in range(2)]
        acc = jnp.where(lane < NA_HEAD_DIM, outs[0], outs[1])
        z = z_ref[:, cs].astype(F32)
        o_ref[:, cs] = (acc * (z * _sigmoid(z))).astype(BF16)


def _neighborhood_attention(p3, pc3, bias_tab):
    b, s, _ = p3.shape
    n_blk = s // Q_BLK

    def key_blk(i):
        return jnp.clip(i - 1, 0, n_blk - KEY_ROWS // Q_ROWS)

    def case_of(i):
        return jnp.minimum(i, 2) + jnp.maximum(i - (n_blk - 3), 0)

    cq, ck, cv, cz = COL_Q // 1024, COL_K // 1024, COL_V // 1024, COL_ZNA // 1024
    blk = (None, Q_BLK, 1024)
    in_specs = [pl.BlockSpec(blk, lambda i, j: (j, i, cq))]
    in_specs += [pl.BlockSpec(blk, functools.partial(lambda i, j, t: (j, key_blk(i) + t, ck), t=t)) for t in range(3)]
    in_specs += [pl.BlockSpec(blk, lambda i, j: (j, 0, CCOL_K // 1024))]
    in_specs += [pl.BlockSpec(blk, functools.partial(lambda i, j, t: (j, key_blk(i) + t, cv), t=t)) for t in range(3)]
    in_specs += [pl.BlockSpec(blk, lambda i, j: (j, 0, CCOL_V // 1024))]
    in_specs += [pl.BlockSpec(blk, lambda i, j: (j, i, cz))]
    in_specs += [pl.BlockSpec((None, NA_HEADS, Q_BLK, KEY_BLK), lambda i, j: (case_of(i), 0, 0, 0))]
    return pl.pallas_call(
        _na_kernel,
        grid=(n_blk, b),
        in_specs=in_specs,
        out_specs=pl.BlockSpec(blk, lambda i, j: (j, i, 0)),
        out_shape=jax.ShapeDtypeStruct((b, s, NA_HEADS * NA_HEAD_DIM), BF16),
        compiler_params=pltpu.CompilerParams(
            dimension_semantics=("parallel", "parallel"), vmem_limit_bytes=VMEM_LIMIT),
        name="neighborhood_attention",
    )(p3, p3, p3, p3, pc3, p3, p3, p3, pc3, p3, bias_tab)


def _softplus(v):
    return jnp.maximum(v, 0.0) + jnp.log(1.0 + jnp.exp(-jnp.abs(v)))


def _head_stack(v, head_lanes):
    return jnp.concatenate([jnp.where(m, v, jnp.zeros_like(v)) for m in head_lanes], axis=0)


def _ssd_kernel(xs_ref, bm_ref, cm_ref, z_ref, xsc_ref, bmc_ref, dtt_ref, dttc_ref,
                cwx_ref, cwb_ref, cwc_ref, cbx_ref, cbb_ref, cbc_ref,
                alog_ref, dtb_ref, skip_ref, ng_ref, o_ref,
                pad_ref, padc_ref, tap_ref, ux_ref, ubt_ref, uc_ref, uxc_ref, ubtc_ref,
                rows_ref, cols_ref, dec_ref, rowsc_ref, decc_ref, y_ref, st_ref, e_ref):
    q = SSM_CHUNK
    hpg = HEADS_PER_GROUP
    nh = N_DIRS * hpg
    s_len = xs_ref.shape[0]
    c_len = xsc_ref.shape[0]
    n_lat = s_len // q
    n_ctx = c_len // q
    halo = CONV_HALO
    blk = CONV_BLK
    win = blk + 2 * halo

    zeros_halo = jnp.zeros((halo, pad_ref.shape[1]), BF16)
    pad_ref[0:halo, :] = zeros_halo
    pad_ref[s_len + halo:s_len + 2 * halo, :] = zeros_halo
    pad_ref[halo:s_len + halo, 0:GROUP_X] = xs_ref[...]
    pad_ref[halo:s_len + halo, GROUP_X:GROUP_X + SSM_STATE] = bm_ref[...]
    pad_ref[halo:s_len + halo, GROUP_X + SSM_STATE:] = cm_ref[...]
    zeros_halo_c = jnp.zeros((halo, padc_ref.shape[1]), BF16)
    padc_ref[0:halo, :] = zeros_halo_c
    padc_ref[c_len + halo:c_len + 2 * halo, :] = zeros_halo_c
    padc_ref[halo:c_len + halo, 0:GROUP_X] = xsc_ref[...]
    padc_ref[halo:c_len + halo, GROUP_X:] = bmc_ref[...]

    cw = jnp.concatenate([cwx_ref[...], cwb_ref[...], cwc_ref[...]], axis=1)
    cb = jnp.concatenate([cbx_ref[...], cbb_ref[...], cbc_ref[...]], axis=1)
    for k in range(SSM_CONV):
        tap_ref[k] = jnp.broadcast_to(cw[k:k + 1, :], (win, cw.shape[1])).astype(BF16)
    off = lax.broadcasted_iota(jnp.int32, (blk, SSM_CONV * win), 1) - lax.broadcasted_iota(
        jnp.int32, (blk, SSM_CONV * win), 0)
    hit = off == halo - SSM_CONV // 2
    for k in range(1, SSM_CONV):
        hit = hit | (off == k * win + halo - SSM_CONV // 2 + k)
    shift = jnp.where(hit, 1.0, 0.0).astype(BF16)

    def conv_block(src_ref, start, width):
        w = src_ref[pl.ds(start, win), :]
        stack = jnp.concatenate([w * tap_ref[k, :, 0:width] for k in range(SSM_CONV)], axis=0)
        acc = _dot(shift, stack) + cb[:, :width]
        return acc * _sigmoid(acc)

    def conv_chunk(src_ref, start, width):
        return jnp.concatenate([conv_block(src_ref, start + i * blk, width) for i in range(q // blk)], axis=0)

    def conv_lat(c, carry):
        start = pl.multiple_of(c * q, q)
        u = conv_chunk(pad_ref, start, GROUP_X + 2 * SSM_STATE)
        ux_ref[pl.ds(start, q), :] = u[:, :GROUP_X]
        ubt_ref[c] = u[:, GROUP_X:GROUP_X + SSM_STATE].T.astype(BF16)
        uc_ref[pl.ds(start, q), :] = u[:, GROUP_X + SSM_STATE:].astype(BF16)
        return carry

    lax.fori_loop(0, n_lat, conv_lat, 0, unroll=SSD_UNROLL)
    for c in range(n_ctx):
        u = conv_chunk(padc_ref, c * q, GROUP_X + SSM_STATE)
        uxc_ref[c * q:(c + 1) * q, :] = u[:, :GROUP_X]
        ubtc_ref[c] = u[:, GROUP_X:].T.astype(BF16)

    ki = lax.broadcasted_iota(jnp.int32, (q, q), 0)
    ii = lax.broadcasted_iota(jnp.int32, (q, q), 1)
    tri_f = (ki <= ii).astype(F32)
    tri_b = (ki >= ii).astype(F32)
    lane_x = lax.broadcasted_iota(jnp.int32, (1, GROUP_X), 1)
    head_lanes = [(lane_x >= r * SSM_HEAD_DIM) & (lane_x < (r + 1) * SSM_HEAD_DIM) for r in range(hpg)]
    neg_a2 = -jnp.exp(alog_ref[...]) * LOG2E
    dt_bias = dtb_ref[...]

    def dt_forms(raw, n, rows_out, cols_out, dec_out):
        stack = lambda v: jnp.concatenate([v[:, c * q:(c + 1) * q] for c in range(n)], axis=0)
        dt_all = _softplus(raw + dt_bias)
        dt = stack(dt_all)
        a = stack(neg_a2 * dt_all)
        is_fwd = lax.broadcasted_iota(jnp.int32, (n * nh, 1), 0) % nh < hpg
        cum_f = jnp.dot(a, tri_f, preferred_element_type=F32, precision=lax.Precision.HIGHEST)
        cum_b = jnp.dot(a, tri_b, preferred_element_type=F32, precision=lax.Precision.HIGHEST)
        acum = jnp.where(is_fwd, cum_f, cum_b)
        atot = jnp.where(is_fwd, acum[:, q - 1:q], acum[:, 0:1])
        sw = dt * jnp.exp2(atot - acum)
        dec = jnp.exp2(atot)
        for c in range(n):
            sl = slice(c * nh, (c + 1) * nh)
            rows_out[c] = jnp.concatenate([dt[sl], acum[sl], sw[sl]], axis=0)
            if cols_out is not None:
                cols_out[c] = acum[sl].T
            dec_rows = []
            for d in range(N_DIRS):
                base = c * nh + d * hpg
                row = jnp.broadcast_to(dec[base + hpg - 1:base + hpg, :], (1, GROUP_X))
                for r in range(hpg - 2, -1, -1):
                    row = jnp.where(lane_x < (r + 1) * SSM_HEAD_DIM, dec[base + r:base + r + 1, :], row)
                dec_rows.append(row)
            dec_out[c] = jnp.concatenate(dec_rows, axis=0)

    dt_forms(dtt_ref[...], n_lat, rows_ref, cols_ref, dec_ref)
    dt_forms(dttc_ref[...], n_ctx, rowsc_ref, None, decc_ref)

    def state_term(d, rows_b, bt, xm):
        parts = [bt * rows_b[2 * N_DIRS * hpg + d * hpg + r:2 * N_DIRS * hpg + d * hpg + r + 1, :]
                 for r in range(hpg)]
        return _dot(jnp.concatenate(parts, axis=1), xm)

    h_init = []
    for d in range(N_DIRS):
        h = jnp.zeros((SSM_STATE, GROUP_X), F32)
        for c in (range(n_ctx) if d == 0 else reversed(range(n_ctx))):
            xm = _head_stack(uxc_ref[c * q:(c + 1) * q, :].astype(BF16), head_lanes)
            h = decc_ref[c][d:d + 1, :] * h + state_term(d, rowsc_ref[c].astype(BF16), ubtc_ref[c], xm)
        h_init.append(h)

    skip = skip_ref[...]
    norm_g = ng_ref[...]
    keep = (ki >= ii, ki <= ii)
    low_lanes = lax.broadcasted_iota(jnp.int32, (1, q), 1) < SSM_HEAD_DIM

    def chunk_body(c, carry):
        start = pl.multiple_of(c * q, q)
        rows = rows_ref[c]
        cols = cols_ref[c]
        xs = ux_ref[pl.ds(start, q), :]
        xm = _head_stack(xs.astype(BF16), head_lanes)
        cmat = uc_ref[pl.ds(start, q), :]
        bt = ubt_ref[c]
        cbm = _dot(cmat, bt)
        cbm_b = cbm.astype(BF16)
        rows_b = rows.astype(BF16)
        zero_b = jnp.zeros((q, q), BF16)
        m_parts = []
        acols = [[], []]
        for r in range(hpg):
            t = None
            for d in range(N_DIRS):
                k = d * hpg + r
                acol = jnp.broadcast_to(cols[:, k:k + 1], (q, q))
                arow = rows[N_DIRS * hpg + k:N_DIRS * hpg + k + 1, :]
                term = jnp.where(keep[d], jnp.exp2((acol - arow).astype(BF16)) * rows_b[k:k + 1, :], zero_b)
                t = term if t is None else t + term
                acols[d].append(acol)
            m_parts.append(cbm_b * t)
        y_ref[pl.ds(start, q), :] = xs * skip + _dot(jnp.concatenate(m_parts, axis=1), xm)
        for d in range(N_DIRS):
            st_ref[d, c] = state_term(d, rows_b, bt, xm)
            halves = [jnp.where(low_lanes, acols[d][2 * i], acols[d][2 * i + 1]) for i in range(hpg // 2)]
            e_ref[d, c] = jnp.exp2(jnp.concatenate(halves, axis=1))
        return carry

    lax.fori_loop(0, n_lat, chunk_body, 0, unroll=SSD_UNROLL)

    def scan_body(j, carry, finalize):
        new = []
        chunk_ids = (j, n_lat - 1 - j)
        for d in range(N_DIRS):
            h = carry[d]
            c = chunk_ids[d]
            start = pl.multiple_of(c * q, q)
            y_off = _dot(uc_ref[pl.ds(start, q), :], h.astype(BF16)) * e_ref[d, c]
            y_ref[pl.ds(start, q), :] = y_ref[pl.ds(start, q), :] + y_off
            new.append(dec_ref[c][d:d + 1, :] * h + st_ref[d, c])
        if finalize:
            for c in chunk_ids:
                start = pl.multiple_of(c * q, q)
                z = z_ref[pl.ds(start, q), :].astype(F32)
                u = y_ref[pl.ds(start, q), :] * (z * _sigmoid(z))
                u = u * lax.rsqrt(jnp.mean(u * u, axis=-1, keepdims=True) + EPS)
                o_ref[pl.ds(start, q), :] = (u * norm_g).astype(BF16)
        return tuple(new)

    half = n_lat // 2
    carry = lax.fori_loop(0, half, functools.partial(scan_body, finalize=False), tuple(h_init), unroll=SSD_UNROLL)
    lax.fori_loop(half, n_lat, functools.partial(scan_body, finalize=True), carry, unroll=SSD_UNROLL)


def _ssd_mixer(p3, pc3, dtt, dttc, conv_w, conv_b, alog8, dtb8, skip_exp, norm_g):
    b, s, _ = p3.shape
    c_len = pc3.shape[1]
    g = SSM_GROUPS
    n_lat, n_ctx = s // SSM_CHUNK, c_len // SSM_CHUNK
    assert n_lat % 2 == 0
    nx = GROUP_X
    ns = SSM_STATE
    nh = N_DIRS * HEADS_PER_GROUP
    in_specs = [
        pl.BlockSpec((None, s, nx), lambda i, j: (i, 0, COL_X // nx + j)),
        pl.BlockSpec((None, s, ns), lambda i, j: (i, 0, COL_B // ns + j)),
        pl.BlockSpec((None, s, ns), lambda i, j: (i, 0, COL_C // ns + j)),
        pl.BlockSpec((None, s, nx), lambda i, j: (i, 0, COL_ZSSM // nx + j)),
        pl.BlockSpec((None, c_len, nx), lambda i, j: (i, 0, CCOL_X // nx + j)),
        pl.BlockSpec((None, c_len, ns), lambda i, j: (i, 0, CCOL_B // ns + j)),
        pl.BlockSpec((None, nh, s), lambda i, j: (i, j, 0)),
        pl.BlockSpec((None, nh, c_len), lambda i, j: (i, j, 0)),
        pl.BlockSpec((SSM_CONV, nx), lambda i, j: (0, j)),
        pl.BlockSpec((SSM_CONV, ns), lambda i, j: (0, SSM_WIDTH // ns + j)),
        pl.BlockSpec((SSM_CONV, ns), lambda i, j: (0, SSM_WIDTH // ns + g + j)),
        pl.BlockSpec((1, nx), lambda i, j: (0, j)),
        pl.BlockSpec((1, ns), lambda i, j: (0, SSM_WIDTH // ns + j)),
        pl.BlockSpec((1, ns), lambda i, j: (0, SSM_WIDTH // ns + g + j)),
        pl.BlockSpec((None, nh, 1), lambda i, j: (j, 0, 0)),
        pl.BlockSpec((None, nh, 1), lambda i, j: (j, 0, 0)),
        pl.BlockSpec((None, 1, nx), lambda i, j: (j, 0, 0)),
        pl.BlockSpec((1, nx), lambda i, j: (0, j)),
    ]
    halo = CONV_HALO
    scratch = [
        pltpu.VMEM((s + 2 * halo, nx + 2 * ns), BF16),
        pltpu.VMEM((c_len + 2 * halo, nx + ns), BF16),
        pltpu.VMEM((SSM_CONV, CONV_BLK + 2 * halo, nx + 2 * ns), BF16),
        pltpu.VMEM((s, nx), F32),
        pltpu.VMEM((n_lat, ns, SSM_CHUNK), BF16),
        pltpu.VMEM((s, ns), BF16),
        pltpu.VMEM((c_len, nx), F32),
        pltpu.VMEM((n_ctx, ns, SSM_CHUNK), BF16),
        pltpu.VMEM((n_lat, 3 * nh, SSM_CHUNK), F32),
        pltpu.VMEM((n_lat, SSM_CHUNK, nh), F32),
        pltpu.VMEM((n_lat, N_DIRS, nx), F32),
        pltpu.VMEM((n_ctx, 3 * nh, SSM_CHUNK), F32),
        pltpu.VMEM((n_ctx, N_DIRS, nx), F32),
        pltpu.VMEM((s, nx), F32),
        pltpu.VMEM((N_DIRS, n_lat, ns, nx), F32),
        pltpu.VMEM((N_DIRS, n_lat, SSM_CHUNK, nx), F32),
    ]
    return pl.pallas_call(
        _ssd_kernel,
        grid=(b, g),
        in_specs=in_specs,
        out_specs=pl.BlockSpec((None, s, nx), lambda i, j: (i, 0, j)),
        out_shape=jax.ShapeDtypeStruct((b, s, SSM_WIDTH), BF16),
        scratch_shapes=scratch,
        compiler_params=pltpu.CompilerParams(
            dimension_semantics=("parallel", "parallel"), vmem_limit_bytes=VMEM_LIMIT),
        name="ssd_mixer",
    )(p3, p3, p3, p3, pc3, pc3, dtt, dttc, conv_w, conv_w, conv_w, conv_b, conv_b, conv_b,
      alog8, dtb8, skip_exp, norm_g)


def _merge_kernel(yna_ref, yssm_ref, gna_ref, gssm_ref, x_ref, gate_ref, wna_ref, wssm_ref, wout_ref, gpost_ref, o_ref):
    a = _dot(yna_ref[...], wna_ref[...])
    s = _dot(yssm_ref[...], wssm_ref[...])
    m = _sigmoid(gna_ref[...].astype(F32)) * a + _sigmoid(gssm_ref[...].astype(F32)) * s
    o = _dot(m.astype(BF16), wout_ref[...])
    r = o * lax.rsqrt(jnp.mean(o * o, axis=-1, keepdims=True) + EPS) * gpost_ref[...]
    o_ref[...] = x_ref[...] + gate_ref[...] * r


def _merge(y_na, y_ssm, p3, x3, gate, w_na, w_ssm, w_out, g_post, tm):
    b, s, d = x3.shape
    const = lambda i, j: (0, 0)
    return pl.pallas_call(
        _merge_kernel,
        grid=(b, s // tm),
        in_specs=[pl.BlockSpec((None, tm, y_na.shape[-1]), lambda i, j: (i, j, 0)),
                  pl.BlockSpec((None, tm, y_ssm.shape[-1]), lambda i, j: (i, j, 0)),
                  pl.BlockSpec((None, tm, d), lambda i, j: (i, j, COL_GNA // d)),
                  pl.BlockSpec((None, tm, d), lambda i, j: (i, j, COL_GSSM // d)),
                  pl.BlockSpec((None, tm, d), lambda i, j: (i, j, 0)),
                  pl.BlockSpec((None, 1, d), lambda i, j: (i, 0, 0)),
                  pl.BlockSpec(w_na.shape, const),
                  pl.BlockSpec(w_ssm.shape, const),
                  pl.BlockSpec(w_out.shape, const),
                  pl.BlockSpec((1, d), const)],
        out_specs=pl.BlockSpec((None, tm, d), lambda i, j: (i, j, 0)),
        out_shape=jax.ShapeDtypeStruct((b, s, d), x3.dtype),
        compiler_params=pltpu.CompilerParams(
            dimension_semantics=("parallel", "parallel"), vmem_limit_bytes=VMEM_LIMIT),
        name="merge_out",
    )(y_na, y_ssm, p3, p3, x3, gate, w_na, w_ssm, w_out, g_post)


def kernel(x, c, ctx, c_ctx, w_mod, b_mod, g_pre, g_post, w_in, conv_w, conv_b, a_log, dt_bias, d_skip,
           ssm_norm_g, rpb, w_na_out, w_ssm_out, w_out):
    assert w_mod.shape[0] == 1, "single-layer block"
    b, s, d = x.shape
    g, hpg = SSM_GROUPS, HEADS_PER_GROUP

    w_main, w_dt = _prep_w_in(jnp.transpose(w_in[0]), COL_GNA, N_DIRS * SSM_HEADS, COL_K, NA_HEAD_DIM ** -0.5 * LOG2E)
    per_group = lambda p: jnp.transpose(p.reshape(N_DIRS, g, hpg), (1, 0, 2)).reshape(g, N_DIRS * hpg, 1)
    alog8 = per_group(a_log[0])
    dtb8 = per_group(dt_bias[0])
    skip_exp = jnp.repeat(d_skip[0], SSM_HEAD_DIM).reshape(g, 1, GROUP_X)

    cond = jnp.zeros((16, d), F32).at[:b].set(c).at[b].set(c_ctx)
    mod = _adaln(cond, w_mod[0], b_mod[0])
    shift_l, scale_l, gate_l = (mod[:b, k * d:(k + 1) * d].reshape(b, 1, d) for k in range(3))
    shift_c, scale_c = (jnp.broadcast_to(mod[b, k * d:(k + 1) * d].reshape(1, 1, d), (b, 1, d)) for k in range(2))

    p3, dtt = _projection(x, shift_l, scale_l, g_pre, w_main, w_dt, ((0, NP_COLS),), tm=PROJ_TM)
    pc3, dttc = _projection(ctx, shift_c, scale_c, g_pre, w_main, w_dt,
                            ((COL_K, COL_ZNA), (COL_X, COL_GNA)), tm=PROJ_TM)

    bias_tab = _bias_table(rpb[0])
    y_na = _neighborhood_attention(p3, pc3, bias_tab)

    y_ssm = _ssd_mixer(p3, pc3, dtt, dttc, conv_w[0], conv_b[0].reshape(1, -1), alog8, dtb8, skip_exp,
                       ssm_norm_g)

    return _merge(y_na, y_ssm, p3, x, gate_l, w_na_out[0].astype(BF16), w_ssm_out[0].astype(BF16),
                  w_out[0].astype(BF16), g_post, tm=MERGE_TM)
```

```python
import functools

import numpy as np
import jax
import jax.numpy as jnp
from jax import lax
from jax.experimental import pallas as pl
from jax.experimental.pallas import tpu as pltpu

F32 = jnp.float32
BF16 = jnp.bfloat16

D_MODEL = 1024
GRID_W = 64
NA_HEADS = 16
NA_HEAD_DIM = 64
WIN_ROWS = 8
WIN_COLS = 16
SSM_WIDTH = 2 * D_MODEL
SSM_HEAD_DIM = 64
SSM_HEADS = SSM_WIDTH // SSM_HEAD_DIM
SSM_GROUPS = 8
HEADS_PER_GROUP = SSM_HEADS // SSM_GROUPS
SSM_STATE = 128
SSM_CONV = 5
SSM_CHUNK = 128
CONV_HALO = 16
CONV_BLK = 64
SSD_UNROLL = 8
N_DIRS = 2
GROUP_X = HEADS_PER_GROUP * SSM_HEAD_DIM
EPS = 1e-6
NEG = -1e30
LOG2E = 1.4426950408889634

COL_Q, COL_K, COL_V, COL_ZNA = 0, 1024, 2048, 3072
COL_ZSSM, COL_X, COL_B, COL_C = 4096, 6144, 8192, 9216
COL_GNA, COL_GSSM, NP_COLS = 10240, 11264, 12288
CCOL_K, CCOL_V, CCOL_X, CCOL_B, NPC_COLS = 0, 1024, 2048, 4096, 6144

Q_ROWS = 4
Q_BLK = Q_ROWS * GRID_W
KEY_ROWS = 12
KEY_BLK = KEY_ROWS * GRID_W
N_BIAS_CASES = 5
BIAS_CASE_BLOCK = (0, 1, 2, 6, 7)
BIAS_LANES = 128

V7X_VMEM_BYTES = 64 * 1024 * 1024
VMEM_LIMIT = V7X_VMEM_BYTES * 7 // 8
LANES = 128
PROJ_TM = 256
PROJ_CHUNK = 512
MERGE_TM = 512
ADALN_TN = 1024
KEY_TILE = Q_BLK
NA_BATCH = 2


def _sigmoid(v):
    return 1.0 / (1.0 + jnp.exp(-v))


def _dot(a, b):
    return jnp.dot(a, b, preferred_element_type=F32)


def _dot_nt(a, b):
    return lax.dot_general(a, b, (((1,), (1,)), ((), ())), preferred_element_type=F32)


def _adaln_kernel(cond_ref, w_ref, b_ref, o_ref):
    cnd = cond_ref[...]
    act = cnd * _sigmoid(cnd)
    o_ref[...] = jnp.dot(act, w_ref[...], preferred_element_type=F32,
                         precision=lax.Precision.HIGHEST) + b_ref[...]


def _adaln(cond, w_mod, b_mod):
    rows, d = cond.shape
    n = w_mod.shape[1]
    tn = ADALN_TN
    return pl.pallas_call(
        _adaln_kernel,
        grid=(n // tn,),
        in_specs=[pl.BlockSpec((rows, d), lambda j: (0, 0)),
                  pl.BlockSpec((d, tn), lambda j: (0, j)),
                  pl.BlockSpec((1, tn), lambda j: (0, j))],
        out_specs=pl.BlockSpec((rows, tn), lambda j: (0, j)),
        out_shape=jax.ShapeDtypeStruct((rows, n), F32),
        name="adaln",
    )(cond, w_mod, b_mod.reshape(1, n))


W_TILE = 512
DT_PAD = 128


def _wprep_kernel(a_ref, b_ref, dt_ref, w_ref, wdt_ref, *, n_aligned, shift, q_tiles, q_scale):
    j = pl.program_id(0)

    @pl.when(j < n_aligned)
    def _():
        scale = jnp.where(j < q_tiles, q_scale, 1.0)
        w_ref[...] = (a_ref[...] * scale).T.astype(BF16)

    @pl.when(j >= n_aligned)
    def _():
        w_ref[...] = jnp.concatenate([a_ref[shift:, :], b_ref[...]], axis=0).T.astype(BF16)

    @pl.when(j == 0)
    def _():
        dst = lax.broadcasted_iota(jnp.int32, (DT_PAD, DT_PAD), 1)
        src = lax.broadcasted_iota(jnp.int32, (DT_PAD, DT_PAD), 0)
        per_group = N_DIRS * HEADS_PER_GROUP
        grp, rem = dst // per_group, dst % per_group
        want = (rem // HEADS_PER_GROUP) * SSM_HEADS + grp * HEADS_PER_GROUP + rem % HEADS_PER_GROUP
        perm = jnp.where((src == want) & (dst < N_DIRS * SSM_HEADS), 1.0, 0.0).astype(BF16)
        wdt_ref[...] = _dot(dt_ref[...].T.astype(BF16), perm).astype(BF16)


def _prep_w_in(w_in_t, col_dt, n_dt, q_cols, q_scale):
    p, d = w_in_t.shape
    n_out = p - n_dt
    assert col_dt % W_TILE == 0 and n_out % W_TILE == 0 and q_cols % W_TILE == 0
    assert W_TILE % n_dt == 0 and n_dt % 8 == 0 and n_dt <= DT_PAD and col_dt % DT_PAD == 0
    return pl.pallas_call(
        functools.partial(_wprep_kernel, n_aligned=col_dt // W_TILE, shift=n_dt, q_tiles=q_cols // W_TILE,
                          q_scale=q_scale),
        grid=(n_out // W_TILE,),
        in_specs=[pl.BlockSpec((W_TILE, d), lambda j: (j, 0)),
                  pl.BlockSpec((n_dt, d), lambda j: ((j + 1) * (W_TILE // n_dt), 0)),
                  pl.BlockSpec((DT_PAD, d), lambda j: (col_dt // DT_PAD, 0))],
        out_specs=[pl.BlockSpec((d, W_TILE), lambda j: (0, j)),
                   pl.BlockSpec((d, DT_PAD), lambda j: (0, 0))],
        out_shape=[jax.ShapeDtypeStruct((d, n_out), BF16), jax.ShapeDtypeStruct((d, DT_PAD), BF16)],
        name="w_in_prep",
    )(w_in_t, w_in_t, w_in_t)


def _proj_kernel(x_ref, shift_ref, scale_ref, g_ref, w_ref, wdt_ref, p_ref, dtt_ref, *, chunks):
    x = x_ref[...]
    ms = jnp.mean(x * x, axis=-1, keepdims=True)
    h = x * lax.rsqrt(ms + EPS) * g_ref[...]
    h = h * (1.0 + scale_ref[...]) + shift_ref[...]
    hb = h.astype(BF16)
    for src, dst, width in chunks:
        p_ref[:, dst:dst + width] = _dot(hb, w_ref[:, src:src + width]).astype(BF16)
    dt = _dot(hb, wdt_ref[...])
    dtt_ref[...] = dt.T[:N_DIRS * SSM_HEADS]


def _projection(x3, shift, scale, g_pre, w_main, w_dt, col_ranges, tm):
    b, l, d = x3.shape
    chunks, dst = [], 0
    cw = PROJ_CHUNK
    for lo, hi in col_ranges:
        for src in range(lo, hi, cw):
            chunks.append((src, dst, cw))
            dst += cw
    n_out = dst
    n_dt = N_DIRS * SSM_HEADS
    return pl.pallas_call(
        functools.partial(_proj_kernel, chunks=tuple(chunks)),
        grid=(b, l // tm),
        in_specs=[pl.BlockSpec((None, tm, d), lambda i, j: (i, j, 0)),
                  pl.BlockSpec((None, 1, d), lambda i, j: (i, 0, 0)),
                  pl.BlockSpec((None, 1, d), lambda i, j: (i, 0, 0)),
                  pl.BlockSpec((1, d), lambda i, j: (0, 0)),
                  pl.BlockSpec(w_main.shape, lambda i, j: (0, 0), pipeline_mode=pl.Buffered(1)),
                  pl.BlockSpec(w_dt.shape, lambda i, j: (0, 0), pipeline_mode=pl.Buffered(1))],
        out_specs=[pl.BlockSpec((None, tm, n_out), lambda i, j: (i, j, 0)),
                   pl.BlockSpec((None, n_dt, tm), lambda i, j: (i, 0, j))],
        out_shape=[jax.ShapeDtypeStruct((b, l, n_out), BF16),
                   jax.ShapeDtypeStruct((b, n_dt, l), F32)],
        compiler_params=pltpu.CompilerParams(
            dimension_semantics=("parallel", "parallel"), vmem_limit_bytes=VMEM_LIMIT),
        name="projection",
    )(x3, shift, scale, g_pre, w_main, w_dt)


def _bias_block_plan():
    rows = 2048 // GRID_W
    plan = np.full((N_BIAS_CASES, Q_ROWS, KEY_ROWS), -1, np.int64)
    for case, blk in enumerate(BIAS_CASE_BLOCK):
        u0 = int(np.clip(Q_ROWS * blk - WIN_ROWS // 2, 0, rows - KEY_ROWS))
        for rho in range(Q_ROWS):
            r = Q_ROWS * blk + rho
            r0 = int(np.clip(r - WIN_ROWS // 2, 0, rows - WIN_ROWS))
            for a in range(KEY_ROWS):
                krow = u0 + a
                if r0 <= krow < r0 + WIN_ROWS:
                    plan[case, rho, a] = krow - r + WIN_ROWS - 1
    return plan


def _bias_kernel(rpb_ref, o_ref, toep_ref, *, plan):
    n_dr = 2 * WIN_ROWS - 1
    qc = lax.broadcasted_iota(jnp.int32, (GRID_W, GRID_W), 0)
    kc = lax.broadcasted_iota(jnp.int32, (GRID_W, GRID_W), 1)
    c0 = jnp.clip(qc - WIN_COLS // 2, 0, GRID_W - WIN_COLS)
    in_win = (kc >= c0) & (kc < c0 + WIN_COLS)
    for dr in range(n_dr):
        rows = jnp.broadcast_to(rpb_ref[dr:dr + 1, :], (GRID_W, BIAS_LANES))
        toep = pltpu.roll(rows, BIAS_LANES - (WIN_COLS - 1), 1, stride=1, stride_axis=0)[:, :GRID_W]
        toep_ref[dr] = jnp.where(in_win, toep * LOG2E, NEG)
    neg = jnp.full((GRID_W, GRID_W), NEG, F32)
    for case in range(N_BIAS_CASES):
        for rho in range(Q_ROWS):
            for a in range(KEY_ROWS):
                dr = int(plan[case, rho, a])
                val = neg if dr < 0 else toep_ref[dr]
                o_ref[case, rho * GRID_W:(rho + 1) * GRID_W, a * GRID_W:(a + 1) * GRID_W] = val.astype(BF16)


def _bias_table(rpb):
    plan = _bias_block_plan()
    heads, n_dr, n_dc = rpb.shape
    rpb_lanes = jnp.pad(rpb, ((0, 0), (0, 0), (0, BIAS_LANES - n_dc)))
    return pl.pallas_call(
        functools.partial(_bias_kernel, plan=plan),
        grid=(heads,),
        in_specs=[pl.BlockSpec((None, n_dr, BIAS_LANES), lambda h: (h, 0, 0))],
        out_specs=pl.BlockSpec((N_BIAS_CASES, None, Q_BLK, KEY_BLK), lambda h: (0, h, 0, 0)),
        out_shape=jax.ShapeDtypeStruct((N_BIAS_CASES, heads, Q_BLK, KEY_BLK), BF16),
        scratch_shapes=[pltpu.VMEM((n_dr, GRID_W, GRID_W), F32)],
        name="bias_table",
    )(rpb_lanes)


def _na_kernel(q_ref, k0_ref, k1_ref, k2_ref, kc_ref, v0_ref, v1_ref, v2_ref, vc_ref, z_ref, bias_ref, o_ref):
    lane = lax.broadcasted_iota(jnp.int32, (1, 2 * NA_HEAD_DIM), 1)
    k_refs = (k0_ref, k1_ref, k2_ref, kc_ref)
    v_refs = (v0_ref, v1_ref, v2_ref, vc_ref)
    ones_tile = jnp.where(lax.broadcasted_iota(jnp.int32, (Q_BLK, 2 * NA_HEAD_DIM), 1) == 0, 1.0, 0.0).astype(BF16)
    for bb, pair in ((bb, pair) for bb in range(q_ref.shape[0]) for pair in range(NA_HEADS // 2)):
        cs = slice(pair * LANES, (pair + 1) * LANES)
        qp = q_ref[bb, :, cs]
        q_stack = jnp.concatenate([jnp.where(lane < NA_HEAD_DIM, qp, jnp.zeros_like(qp)),
                                   jnp.where(lane >= NA_HEAD_DIM, qp, jnp.zeros_like(qp))], axis=0)
        s_both = [_dot_nt(q_stack, r[bb, :, cs]) for r in k_refs]
        e_both = [[], []]
        for hh in range(2):
            s = [sb[hh * Q_BLK:(hh + 1) * Q_BLK] for sb in s_both]
            for t in range(KEY_ROWS // Q_ROWS):
                s[t] = s[t] + bias_ref[2 * pair + hh, :, t * KEY_TILE:(t + 1) * KEY_TILE].astype(F32)
            mx = jnp.max(jnp.maximum(jnp.maximum(s[0], s[1]), jnp.maximum(s[2], s[3])), axis=-1, keepdims=True)
            e_both[hh] = [jnp.exp2((st - mx).astype(BF16)) for st in s]
        o = None
        for t, r in enumerate(v_refs):
            p_stack = jnp.concatenate([e_both[0][t], e_both[1][t]], axis=0)
            c = _dot(p_stack, jnp.concatenate([r[bb, :, cs], ones_tile], axis=1))
            o = c if o is None else o + c
        outs = [o[hh * Q_BLK:(hh + 1) * Q_BLK, :LANES] * (1.0 / o[hh * Q_BLK:(hh + 1) * Q_BLK, LANES:LANES + 1])
                for hh in range(2)]
        acc = jnp.where(lane < NA_HEAD_DIM, outs[0], outs[1])
        z = z_ref[bb, :, cs].astype(F32)
        o_ref[bb, :, cs] = (acc * (z * _sigmoid(z))).astype(BF16)


def _neighborhood_attention(p3, pc3, bias_tab):
    b, s, _ = p3.shape
    assert b % NA_BATCH == 0
    n_blk = s // Q_BLK

    def key_blk(i):
        return jnp.clip(i - 1, 0, n_blk - KEY_ROWS // Q_ROWS)

    def case_of(i):
        return jnp.minimum(i, 2) + jnp.maximum(i - (n_blk - 3), 0)

    cq, ck, cv, cz = COL_Q // 1024, COL_K // 1024, COL_V // 1024, COL_ZNA // 1024
    blk = (NA_BATCH, Q_BLK, 1024)
    in_specs = [pl.BlockSpec(blk, lambda i, j: (j, i, cq))]
    in_specs += [pl.BlockSpec(blk, functools.partial(lambda i, j, t: (j, key_blk(i) + t, ck), t=t)) for t in range(3)]
    in_specs += [pl.BlockSpec(blk, lambda i, j: (j, 0, CCOL_K // 1024))]
    in_specs += [pl.BlockSpec(blk, functools.partial(lambda i, j, t: (j, key_blk(i) + t, cv), t=t)) for t in range(3)]
    in_specs += [pl.BlockSpec(blk, lambda i, j: (j, 0, CCOL_V // 1024))]
    in_specs += [pl.BlockSpec(blk, lambda i, j: (j, i, cz))]
    in_specs += [pl.BlockSpec((None, NA_HEADS, Q_BLK, KEY_BLK), lambda i, j: (case_of(i), 0, 0, 0))]
    return pl.pallas_call(
        _na_kernel,
        grid=(n_blk, b // NA_BATCH),
        in_specs=in_specs,
        out_specs=pl.BlockSpec(blk, lambda i, j: (j, i, 0)),
        out_shape=jax.ShapeDtypeStruct((b, s, NA_HEADS * NA_HEAD_DIM), BF16),
        compiler_params=pltpu.CompilerParams(
            dimension_semantics=("parallel", "parallel"), vmem_limit_bytes=VMEM_LIMIT),
        name="neighborhood_attention",
    )(p3, p3, p3, p3, pc3, p3, p3, p3, pc3, p3, bias_tab)


def _softplus(v):
    return jnp.maximum(v, 0.0) + jnp.log(1.0 + jnp.exp(-jnp.abs(v)))


def _head_stack(v, head_lanes):
    return jnp.concatenate([jnp.where(m, v, jnp.zeros_like(v)) for m in head_lanes], axis=0)


def _ssd_kernel(xs_ref, bm_ref, cm_ref, z_ref, xsc_ref, bmc_ref, dtt_ref, dttc_ref,
                cwx_ref, cwb_ref, cwc_ref, cbx_ref, cbb_ref, cbc_ref,
                alog_ref, dtb_ref, skip_ref, ng_ref, o_ref,
                pad_ref, padc_ref, tap_ref, ux_ref, ubt_ref, uc_ref, uxc_ref, ubtc_ref,
                rows_ref, cols_ref, dec_ref, rowsc_ref, decc_ref, y_ref, st_ref, e_ref):
    q = SSM_CHUNK
    hpg = HEADS_PER_GROUP
    nh = N_DIRS * hpg
    s_len = xs_ref.shape[0]
    c_len = xsc_ref.shape[0]
    n_lat = s_len // q
    n_ctx = c_len // q
    halo = CONV_HALO
    blk = CONV_BLK
    win = blk + 2 * halo

    zeros_halo = jnp.zeros((halo, pad_ref.shape[1]), BF16)
    pad_ref[0:halo, :] = zeros_halo
    pad_ref[s_len + halo:s_len + 2 * halo, :] = zeros_halo
    pad_ref[halo:s_len + halo, 0:GROUP_X] = xs_ref[...]
    pad_ref[halo:s_len + halo, GROUP_X:GROUP_X + SSM_STATE] = bm_ref[...]
    pad_ref[halo:s_len + halo, GROUP_X + SSM_STATE:] = cm_ref[...]
    zeros_halo_c = jnp.zeros((halo, padc_ref.shape[1]), BF16)
    padc_ref[0:halo, :] = zeros_halo_c
    padc_ref[c_len + halo:c_len + 2 * halo, :] = zeros_halo_c
    padc_ref[halo:c_len + halo, 0:GROUP_X] = xsc_ref[...]
    padc_ref[halo:c_len + halo, GROUP_X:] = bmc_ref[...]

    cw = jnp.concatenate([cwx_ref[...], cwb_ref[...], cwc_ref[...]], axis=1)
    cb = jnp.concatenate([cbx_ref[...], cbb_ref[...], cbc_ref[...]], axis=1)
    for k in range(SSM_CONV):
        tap_ref[k] = jnp.broadcast_to(cw[k:k + 1, :], (win, cw.shape[1])).astype(BF16)
    off = lax.broadcasted_iota(jnp.int32, (blk, SSM_CONV * win), 1) - lax.broadcasted_iota(
        jnp.int32, (blk, SSM_CONV * win), 0)
    hit = off == halo - SSM_CONV // 2
    for k in range(1, SSM_CONV):
        hit = hit | (off == k * win + halo - SSM_CONV // 2 + k)
    shift = jnp.where(hit, 1.0, 0.0).astype(BF16)

    def conv_block(src_ref, start, width):
        w = src_ref[pl.ds(start, win), :]
        stack = jnp.concatenate([w * tap_ref[k, :, 0:width] for k in range(SSM_CONV)], axis=0)
        acc = _dot(shift, stack) + cb[:, :width]
        return acc * _sigmoid(acc)

    def conv_chunk(src_ref, start, width):
        return jnp.concatenate([conv_block(src_ref, start + i * blk, width) for i in range(q // blk)], axis=0)

    def conv_lat(c, carry):
        start = pl.multiple_of(c * q, q)
        u = conv_chunk(pad_ref, start, GROUP_X + 2 * SSM_STATE)
        ux_ref[pl.ds(start, q), :] = u[:, :GROUP_X]
        ubt_ref[c] = u[:, GROUP_X:GROUP_X + SSM_STATE].T.astype(BF16)
        uc_ref[pl.ds(start, q), :] = u[:, GROUP_X + SSM_STATE:].astype(BF16)
        return carry

    lax.fori_loop(0, n_lat, conv_lat, 0, unroll=SSD_UNROLL)
    for c in range(n_ctx):
        u = conv_chunk(padc_ref, c * q, GROUP_X + SSM_STATE)
        uxc_ref[c * q:(c + 1) * q, :] = u[:, :GROUP_X]
        ubtc_ref[c] = u[:, GROUP_X:].T.astype(BF16)

    ki = lax.broadcasted_iota(jnp.int32, (q, q), 0)
    ii = lax.broadcasted_iota(jnp.int32, (q, q), 1)
    tri_f = (ki <= ii).astype(F32)
    tri_b = (ki >= ii).astype(F32)
    lane_x = lax.broadcasted_iota(jnp.int32, (1, GROUP_X), 1)
    head_lanes = [(lane_x >= r * SSM_HEAD_DIM) & (lane_x < (r + 1) * SSM_HEAD_DIM) for r in range(hpg)]
    neg_a2 = -jnp.exp(alog_ref[...]) * LOG2E
    dt_bias = dtb_ref[...]

    def dt_forms(raw, n, rows_out, cols_out, dec_out):
        stack = lambda v: jnp.concatenate([v[:, c * q:(c + 1) * q] for c in range(n)], axis=0)
        dt_all = _softplus(raw + dt_bias)
        dt = stack(dt_all)
        a = stack(neg_a2 * dt_all)
        is_fwd = lax.broadcasted_iota(jnp.int32, (n * nh, 1), 0) % nh < hpg
        cum_f = jnp.dot(a, tri_f, preferred_element_type=F32, precision=lax.Precision.HIGHEST)
        cum_b = jnp.dot(a, tri_b, preferred_element_type=F32, precision=lax.Precision.HIGHEST)
        acum = jnp.where(is_fwd, cum_f, cum_b)
        atot = jnp.where(is_fwd, acum[:, q - 1:q], acum[:, 0:1])
        sw = dt * jnp.exp2(atot - acum)
        dec = jnp.exp2(atot)
        for c in range(n):
            sl = slice(c * nh, (c + 1) * nh)
            rows_out[c] = jnp.concatenate([dt[sl], acum[sl], sw[sl]], axis=0)
            if cols_out is not None:
                cols_out[c] = acum[sl].T
            dec_rows = []
            for d in range(N_DIRS):
                base = c * nh + d * hpg
                row = jnp.broadcast_to(dec[base + hpg - 1:base + hpg, :], (1, GROUP_X))
                for r in range(hpg - 2, -1, -1):
                    row = jnp.where(lane_x < (r + 1) * SSM_HEAD_DIM, dec[base + r:base + r + 1, :], row)
                dec_rows.append(row)
            dec_out[c] = jnp.concatenate(dec_rows, axis=0)

    dt_forms(dtt_ref[...], n_lat, rows_ref, cols_ref, dec_ref)
    dt_forms(dttc_ref[...], n_ctx, rowsc_ref, None, decc_ref)

    def state_term(d, rows_b, bt, xm):
        parts = [bt * rows_b[2 * N_DIRS * hpg + d * hpg + r:2 * N_DIRS * hpg + d * hpg + r + 1, :]
                 for r in range(hpg)]
        return _dot(jnp.concatenate(parts, axis=1), xm)

    h_init = []
    for d in range(N_DIRS):
        h = jnp.zeros((SSM_STATE, GROUP_X), F32)
        for c in (range(n_ctx) if d == 0 else reversed(range(n_ctx))):
            xm = _head_stack(uxc_ref[c * q:(c + 1) * q, :].astype(BF16), head_lanes)
            h = decc_ref[c][d:d + 1, :] * h + state_term(d, rowsc_ref[c].astype(BF16), ubtc_ref[c], xm)
        h_init.append(h)

    skip = skip_ref[...]
    norm_g = ng_ref[...]
    keep = (ki >= ii, ki <= ii)
    low_lanes = lax.broadcasted_iota(jnp.int32, (1, q), 1) < SSM_HEAD_DIM

    def chunk_body(c, carry):
        start = pl.multiple_of(c * q, q)
        rows = rows_ref[c]
        cols = cols_ref[c]
        xs = ux_ref[pl.ds(start, q), :]
        xm = _head_stack(xs.astype(BF16), head_lanes)
        cmat = uc_ref[pl.ds(start, q), :]
        bt = ubt_ref[c]
        cbm = _dot(cmat, bt)
        cbm_b = cbm.astype(BF16)
        rows_b = rows.astype(BF16)
        zero_b = jnp.zeros((q, q), BF16)
        m_parts = []
        acols = [[], []]
        for r in range(hpg):
            t = None
            for d in range(N_DIRS):
                k = d * hpg + r
                acol = jnp.broadcast_to(cols[:, k:k + 1], (q, q))
                arow = rows[N_DIRS * hpg + k:N_DIRS * hpg + k + 1, :]
                term = jnp.where(keep[d], jnp.exp2((acol - arow).astype(BF16)) * rows_b[k:k + 1, :], zero_b)
                t = term if t is None else t + term
                acols[d].append(acol)
            m_parts.append(cbm_b * t)
        y_ref[pl.ds(start, q), :] = xs * skip + _dot(jnp.concatenate(m_parts, axis=1), xm)
        for d in range(N_DIRS):
            st_ref[d, c] = state_term(d, rows_b, bt, xm)
            halves = [jnp.where(low_lanes, acols[d][2 * i], acols[d][2 * i + 1]) for i in range(hpg // 2)]
            e_ref[d, c] = jnp.exp2(jnp.concatenate(halves, axis=1))
        return carry

    lax.fori_loop(0, n_lat, chunk_body, 0, unroll=SSD_UNROLL)

    def scan_body(j, carry, finalize):
        new = []
        chunk_ids = (j, n_lat - 1 - j)
        for d in range(N_DIRS):
            h = carry[d]
            c = chunk_ids[d]
            start = pl.multiple_of(c * q, q)
            y_off = _dot(uc_ref[pl.ds(start, q), :], h.astype(BF16)) * e_ref[d, c]
            y_ref[pl.ds(start, q), :] = y_ref[pl.ds(start, q), :] + y_off
            new.append(dec_ref[c][d:d + 1, :] * h + st_ref[d, c])
        if finalize:
            for c in chunk_ids:
                start = pl.multiple_of(c * q, q)
                z = z_ref[pl.ds(start, q), :].astype(F32)
                u = y_ref[pl.ds(start, q), :] * (z * _sigmoid(z))
                u = u * lax.rsqrt(jnp.mean(u * u, axis=-1, keepdims=True) + EPS)
                o_ref[pl.ds(start, q), :] = (u * norm_g).astype(BF16)
        return tuple(new)

    half = n_lat // 2
    carry = lax.fori_loop(0, half, functools.partial(scan_body, finalize=False), tuple(h_init), unroll=SSD_UNROLL)
    lax.fori_loop(half, n_lat, functools.partial(scan_body, finalize=True), carry, unroll=SSD_UNROLL)


def _ssd_mixer(p3, pc3, dtt, dttc, conv_w, conv_b, alog8, dtb8, skip_exp, norm_g):
    b, s, _ = p3.shape
    c_len = pc3.shape[1]
    g = SSM_GROUPS
    n_lat, n_ctx = s // SSM_CHUNK, c_len // SSM_CHUNK
    assert n_lat % 2 == 0
    nx = GROUP_X
    ns = SSM_STATE
    nh = N_DIRS * HEADS_PER_GROUP
    in_specs = [
        pl.BlockSpec((None, s, nx), lambda i, j: (i, 0, COL_X // nx + j)),
        pl.BlockSpec((None, s, ns), lambda i, j: (i, 0, COL_B // ns + j)),
        pl.BlockSpec((None, s, ns), lambda i, j: (i, 0, COL_C // ns + j)),
        pl.BlockSpec((None, s, nx), lambda i, j: (i, 0, COL_ZSSM // nx + j)),
        pl.BlockSpec((None, c_len, nx), lambda i, j: (i, 0, CCOL_X // nx + j)),
        pl.BlockSpec((None, c_len, ns), lambda i, j: (i, 0, CCOL_B // ns + j)),
        pl.BlockSpec((None, nh, s), lambda i, j: (i, j, 0)),
        pl.BlockSpec((None, nh, c_len), lambda i, j: (i, j, 0)),
        pl.BlockSpec((SSM_CONV, nx), lambda i, j: (0, j)),
        pl.BlockSpec((SSM_CONV, ns), lambda i, j: (0, SSM_WIDTH // ns + j)),
        pl.BlockSpec((SSM_CONV, ns), lambda i, j: (0, SSM_WIDTH // ns + g + j)),
        pl.BlockSpec((1, nx), lambda i, j: (0, j)),
        pl.BlockSpec((1, ns), lambda i, j: (0, SSM_WIDTH // ns + j)),
        pl.BlockSpec((1, ns), lambda i, j: (0, SSM_WIDTH // ns + g + j)),
        pl.BlockSpec((None, nh, 1), lambda i, j: (j, 0, 0)),
        pl.BlockSpec((None, nh, 1), lambda i, j: (j, 0, 0)),
        pl.BlockSpec((None, 1, nx), lambda i, j: (j, 0, 0)),
        pl.BlockSpec((1, nx), lambda i, j: (0, j)),
    ]
    halo = CONV_HALO
    scratch = [
        pltpu.VMEM((s + 2 * halo, nx + 2 * ns), BF16),
        pltpu.VMEM((c_len + 2 * halo, nx + ns), BF16),
        pltpu.VMEM((SSM_CONV, CONV_BLK + 2 * halo, nx + 2 * ns), BF16),
        pltpu.VMEM((s, nx), F32),
        pltpu.VMEM((n_lat, ns, SSM_CHUNK), BF16),
        pltpu.VMEM((s, ns), BF16),
        pltpu.VMEM((c_len, nx), F32),
        pltpu.VMEM((n_ctx, ns, SSM_CHUNK), BF16),
        pltpu.VMEM((n_lat, 3 * nh, SSM_CHUNK), F32),
        pltpu.VMEM((n_lat, SSM_CHUNK, nh), F32),
        pltpu.VMEM((n_lat, N_DIRS, nx), F32),
        pltpu.VMEM((n_ctx, 3 * nh, SSM_CHUNK), F32),
        pltpu.VMEM((n_ctx, N_DIRS, nx), F32),
        pltpu.VMEM((s, nx), F32),
        pltpu.VMEM((N_DIRS, n_lat, ns, nx), F32),
        pltpu.VMEM((N_DIRS, n_lat, SSM_CHUNK, nx), F32),
    ]
    return pl.pallas_call(
        _ssd_kernel,
        grid=(b, g),
        in_specs=in_specs,
        out_specs=pl.BlockSpec((None, s, nx), lambda i, j: (i, 0, j)),
        out_shape=jax.ShapeDtypeStruct((b, s, SSM_WIDTH), BF16),
        scratch_shapes=scratch,
        compiler_params=pltpu.CompilerParams(
            dimension_semantics=("parallel", "parallel"), vmem_limit_bytes=VMEM_LIMIT),
        name="ssd_mixer",
    )(p3, p3, p3, p3, pc3, pc3, dtt, dttc, conv_w, conv_w, conv_w, conv_b, conv_b, conv_b,
      alog8, dtb8, skip_exp, norm_g)


def _merge_kernel(yna_ref, yssm_ref, gna_ref, gssm_ref, x_ref, gate_ref, wna_ref, wssm_ref, wout_ref, gpost_ref, o_ref):
    a = _dot(yna_ref[...], wna_ref[...])
    s = _dot(yssm_ref[...], wssm_ref[...])
    m = _sigmoid(gna_ref[...].astype(F32)) * a + _sigmoid(gssm_ref[...].astype(F32)) * s
    o = _dot(m.astype(BF16), wout_ref[...])
    r = o * lax.rsqrt(jnp.mean(o * o, axis=-1, keepdims=True) + EPS) * gpost_ref[...]
    o_ref[...] = x_ref[...] + gate_ref[...] * r


def _merge(y_na, y_ssm, p3, x3, gate, w_na, w_ssm, w_out, g_post, tm):
    b, s, d = x3.shape
    const = lambda i, j: (0, 0)
    return pl.pallas_call(
        _merge_kernel,
        grid=(b, s // tm),
        in_specs=[pl.BlockSpec((None, tm, y_na.shape[-1]), lambda i, j: (i, j, 0)),
                  pl.BlockSpec((None, tm, y_ssm.shape[-1]), lambda i, j: (i, j, 0)),
                  pl.BlockSpec((None, tm, d), lambda i, j: (i, j, COL_GNA // d)),
                  pl.BlockSpec((None, tm, d), lambda i, j: (i, j, COL_GSSM // d)),
                  pl.BlockSpec((None, tm, d), lambda i, j: (i, j, 0)),
                  pl.BlockSpec((None, 1, d), lambda i, j: (i, 0, 0)),
                  pl.BlockSpec(w_na.shape, const),
                  pl.BlockSpec(w_ssm.shape, const),
                  pl.BlockSpec(w_out.shape, const),
                  pl.BlockSpec((1, d), const)],
        out_specs=pl.BlockSpec((None, tm, d), lambda i, j: (i, j, 0)),
        out_shape=jax.ShapeDtypeStruct((b, s, d), x3.dtype),
        compiler_params=pltpu.CompilerParams(
            dimension_semantics=("parallel", "parallel"), vmem_limit_bytes=VMEM_LIMIT),
        name="merge_out",
    )(y_na, y_ssm, p3, p3, x3, gate, w_na, w_ssm, w_out, g_post)


def kernel(x, c, ctx, c_ctx, w_mod, b_mod, g_pre, g_post, w_in, conv_w, conv_b, a_log, dt_bias, d_skip,
           ssm_norm_g, rpb, w_na_out, w_ssm_out, w_out):
    assert w_mod.shape[0] == 1, "single-layer block"
    b, s, d = x.shape
    g, hpg = SSM_GROUPS, HEADS_PER_GROUP

    w_main, w_dt = _prep_w_in(jnp.transpose(w_in[0]), COL_GNA, N_DIRS * SSM_HEADS, COL_K, NA_HEAD_DIM ** -0.5 * LOG2E)
    per_group = lambda p: jnp.transpose(p.reshape(N_DIRS, g, hpg), (1, 0, 2)).reshape(g, N_DIRS * hpg, 1)
    alog8 = per_group(a_log[0])
    dtb8 = per_group(dt_bias[0])
    skip_exp = jnp.repeat(d_skip[0], SSM_HEAD_DIM).reshape(g, 1, GROUP_X)

    cond = jnp.zeros((16, d), F32).at[:b].set(c).at[b].set(c_ctx)
    mod = _adaln(cond, w_mod[0], b_mod[0])
    shift_l, scale_l, gate_l = (mod[:b, k * d:(k + 1) * d].reshape(b, 1, d) for k in range(3))
    shift_c, scale_c = (jnp.broadcast_to(mod[b, k * d:(k + 1) * d].reshape(1, 1, d), (b, 1, d)) for k in range(2))

    p3, dtt = _projection(x, shift_l, scale_l, g_pre, w_main, w_dt, ((0, NP_COLS),), tm=PROJ_TM)
    pc3, dttc = _projection(ctx, shift_c, scale_c, g_pre, w_main, w_dt,
                            ((COL_K, COL_ZNA), (COL_X, COL_GNA)), tm=PROJ_TM)

    bias_tab = _bias_table(rpb[0])
    y_na = _neighborhood_attention(p3, pc3, bias_tab)

    y_ssm = _ssd_mixer(p3, pc3, dtt, dttc, conv_w[0], conv_b[0].reshape(1, -1), alog8, dtb8, skip_exp,
                       ssm_norm_g)

    return _merge(y_na, y_ssm, p3, x, gate_l, w_na_out[0].astype(BF16), w_ssm_out[0].astype(BF16),
                  w_out[0].astype(BF16), g_post, tm=MERGE_TM)
```

```python
import functools

import numpy as np
import jax
import jax.numpy as jnp
from jax import lax
from jax.experimental import pallas as pl
from jax.experimental.pallas import tpu as pltpu

F32 = jnp.float32
BF16 = jnp.bfloat16

D_MODEL = 1024
GRID_W = 64
NA_HEADS = 16
NA_HEAD_DIM = 64
WIN_ROWS = 8
WIN_COLS = 16
SSM_WIDTH = 2 * D_MODEL
SSM_HEAD_DIM = 64
SSM_HEADS = SSM_WIDTH // SSM_HEAD_DIM
SSM_GROUPS = 8
HEADS_PER_GROUP = SSM_HEADS // SSM_GROUPS
SSM_STATE = 128
SSM_CONV = 5
SSM_CHUNK = 128
CONV_HALO = 16
CONV_BLK = 64
SSD_UNROLL = 8
N_DIRS = 2
GROUP_X = HEADS_PER_GROUP * SSM_HEAD_DIM
EPS = 1e-6
NEG = -1e30
LOG2E = 1.4426950408889634

COL_Q, COL_K, COL_V, COL_ZNA = 0, 1024, 2048, 3072
COL_ZSSM, COL_X, COL_B, COL_C = 4096, 6144, 8192, 9216
COL_GNA, COL_GSSM, NP_COLS = 10240, 11264, 12288
CCOL_K, CCOL_V, CCOL_X, CCOL_B, NPC_COLS = 0, 1024, 2048, 4096, 6144

Q_ROWS = 4
Q_BLK = Q_ROWS * GRID_W
KEY_ROWS = 12
KEY_BLK = KEY_ROWS * GRID_W
N_BIAS_CASES = 5
BIAS_CASE_BLOCK = (0, 1, 2, 6, 7)
BIAS_LANES = 128

V7X_VMEM_BYTES = 64 * 1024 * 1024
VMEM_LIMIT = V7X_VMEM_BYTES * 7 // 8
LANES = 128
PROJ_TM = 256
PROJ_CHUNK = 512
MERGE_TM = 1024
ADALN_TN = 1024
KEY_TILE = Q_BLK
NA_BATCH = 2


def _sigmoid(v):
    return 1.0 / (1.0 + jnp.exp(-v))


def _dot(a, b):
    return jnp.dot(a, b, preferred_element_type=F32)


def _dot_nt(a, b):
    return lax.dot_general(a, b, (((1,), (1,)), ((), ())), preferred_element_type=F32)


def _adaln_kernel(cond_ref, w_ref, b_ref, o_ref):
    cnd = cond_ref[...]
    act = cnd * _sigmoid(cnd)
    o_ref[...] = jnp.dot(act, w_ref[...], preferred_element_type=F32,
                         precision=lax.Precision.HIGHEST) + b_ref[...]


def _adaln(cond, w_mod, b_mod):
    rows, d = cond.shape
    n = w_mod.shape[1]
    tn = ADALN_TN
    return pl.pallas_call(
        _adaln_kernel,
        grid=(n // tn,),
        in_specs=[pl.BlockSpec((rows, d), lambda j: (0, 0)),
                  pl.BlockSpec((d, tn), lambda j: (0, j)),
                  pl.BlockSpec((1, tn), lambda j: (0, j))],
        out_specs=pl.BlockSpec((rows, tn), lambda j: (0, j)),
        out_shape=jax.ShapeDtypeStruct((rows, n), F32),
        name="adaln",
    )(cond, w_mod, b_mod.reshape(1, n))


W_TILE = 512
DT_PAD = 128


def _wprep_kernel(a_ref, b_ref, dt_ref, w_ref, wdt_ref, *, n_aligned, shift, q_tiles, q_scale):
    j = pl.program_id(0)

    @pl.when(j < n_aligned)
    def _():
        scale = jnp.where(j < q_tiles, q_scale, 1.0)
        w_ref[...] = (a_ref[...] * scale).T.astype(BF16)

    @pl.when(j >= n_aligned)
    def _():
        w_ref[...] = jnp.concatenate([a_ref[shift:, :], b_ref[...]], axis=0).T.astype(BF16)

    @pl.when(j == 0)
    def _():
        dst = lax.broadcasted_iota(jnp.int32, (DT_PAD, DT_PAD), 1)
        src = lax.broadcasted_iota(jnp.int32, (DT_PAD, DT_PAD), 0)
        per_group = N_DIRS * HEADS_PER_GROUP
        grp, rem = dst // per_group, dst % per_group
        want = (rem // HEADS_PER_GROUP) * SSM_HEADS + grp * HEADS_PER_GROUP + rem % HEADS_PER_GROUP
        perm = jnp.where((src == want) & (dst < N_DIRS * SSM_HEADS), 1.0, 0.0).astype(BF16)
        wdt_ref[...] = _dot(dt_ref[...].T.astype(BF16), perm).astype(BF16)


def _prep_w_in(w_in_t, col_dt, n_dt, q_cols, q_scale):
    p, d = w_in_t.shape
    n_out = p - n_dt
    assert col_dt % W_TILE == 0 and n_out % W_TILE == 0 and q_cols % W_TILE == 0
    assert W_TILE % n_dt == 0 and n_dt % 8 == 0 and n_dt <= DT_PAD and col_dt % DT_PAD == 0
    return pl.pallas_call(
        functools.partial(_wprep_kernel, n_aligned=col_dt // W_TILE, shift=n_dt, q_tiles=q_cols // W_TILE,
                          q_scale=q_scale),
        grid=(n_out // W_TILE,),
        in_specs=[pl.BlockSpec((W_TILE, d), lambda j: (j, 0)),
                  pl.BlockSpec((n_dt, d), lambda j: ((j + 1) * (W_TILE // n_dt), 0)),
                  pl.BlockSpec((DT_PAD, d), lambda j: (col_dt // DT_PAD, 0))],
        out_specs=[pl.BlockSpec((d, W_TILE), lambda j: (0, j)),
                   pl.BlockSpec((d, DT_PAD), lambda j: (0, 0))],
        out_shape=[jax.ShapeDtypeStruct((d, n_out), BF16), jax.ShapeDtypeStruct((d, DT_PAD), BF16)],
        name="w_in_prep",
    )(w_in_t, w_in_t, w_in_t)


def _proj_kernel(x_ref, shift_ref, scale_ref, g_ref, w_ref, wdt_ref, p_ref, dtt_ref, *, chunks):
    x = x_ref[...]
    ms = jnp.mean(x * x, axis=-1, keepdims=True)
    h = x * lax.rsqrt(ms + EPS) * g_ref[...]
    h = h * (1.0 + scale_ref[...]) + shift_ref[...]
    hb = h.astype(BF16)
    for src, dst, width in chunks:
        p_ref[:, dst:dst + width] = _dot(hb, w_ref[:, src:src + width]).astype(BF16)
    dt = _dot(hb, wdt_ref[...])
    dtt_ref[...] = dt.T[:N_DIRS * SSM_HEADS]


def _projection(x3, shift, scale, g_pre, w_main, w_dt, col_ranges, tm):
    b, l, d = x3.shape
    chunks, dst = [], 0
    cw = PROJ_CHUNK
    for lo, hi in col_ranges:
        for src in range(lo, hi, cw):
            chunks.append((src, dst, cw))
            dst += cw
    n_out = dst
    n_dt = N_DIRS * SSM_HEADS
    return pl.pallas_call(
        functools.partial(_proj_kernel, chunks=tuple(chunks)),
        grid=(b, l // tm),
        in_specs=[pl.BlockSpec((None, tm, d), lambda i, j: (i, j, 0)),
                  pl.BlockSpec((None, 1, d), lambda i, j: (i, 0, 0)),
                  pl.BlockSpec((None, 1, d), lambda i, j: (i, 0, 0)),
                  pl.BlockSpec((1, d), lambda i, j: (0, 0)),
                  pl.BlockSpec(w_main.shape, lambda i, j: (0, 0), pipeline_mode=pl.Buffered(1)),
                  pl.BlockSpec(w_dt.shape, lambda i, j: (0, 0), pipeline_mode=pl.Buffered(1))],
        out_specs=[pl.BlockSpec((None, tm, n_out), lambda i, j: (i, j, 0)),
                   pl.BlockSpec((None, n_dt, tm), lambda i, j: (i, 0, j))],
        out_shape=[jax.ShapeDtypeStruct((b, l, n_out), BF16),
                   jax.ShapeDtypeStruct((b, n_dt, l), F32)],
        compiler_params=pltpu.CompilerParams(
            dimension_semantics=("parallel", "parallel"), vmem_limit_bytes=VMEM_LIMIT),
        name="projection",
    )(x3, shift, scale, g_pre, w_main, w_dt)


def _bias_block_plan():
    rows = 2048 // GRID_W
    plan = np.full((N_BIAS_CASES, Q_ROWS, KEY_ROWS), -1, np.int64)
    for case, blk in enumerate(BIAS_CASE_BLOCK):
        u0 = int(np.clip(Q_ROWS * blk - WIN_ROWS // 2, 0, rows - KEY_ROWS))
        for rho in range(Q_ROWS):
            r = Q_ROWS * blk + rho
            r0 = int(np.clip(r - WIN_ROWS // 2, 0, rows - WIN_ROWS))
            for a in range(KEY_ROWS):
                krow = u0 + a
                if r0 <= krow < r0 + WIN_ROWS:
                    plan[case, rho, a] = krow - r + WIN_ROWS - 1
    return plan


def _bias_kernel(rpb_ref, o_ref, toep_ref, *, plan):
    n_dr = 2 * WIN_ROWS - 1
    qc = lax.broadcasted_iota(jnp.int32, (GRID_W, GRID_W), 0)
    kc = lax.broadcasted_iota(jnp.int32, (GRID_W, GRID_W), 1)
    c0 = jnp.clip(qc - WIN_COLS // 2, 0, GRID_W - WIN_COLS)
    in_win = (kc >= c0) & (kc < c0 + WIN_COLS)
    for dr in range(n_dr):
        rows = jnp.broadcast_to(rpb_ref[dr:dr + 1, :], (GRID_W, BIAS_LANES))
        toep = pltpu.roll(rows, BIAS_LANES - (WIN_COLS - 1), 1, stride=1, stride_axis=0)[:, :GRID_W]
        toep_ref[dr] = jnp.where(in_win, toep * LOG2E, NEG)
    neg = jnp.full((GRID_W, GRID_W), NEG, F32)
    for case in range(N_BIAS_CASES):
        for rho in range(Q_ROWS):
            for a in range(KEY_ROWS):
                dr = int(plan[case, rho, a])
                val = neg if dr < 0 else toep_ref[dr]
                o_ref[case, rho * GRID_W:(rho + 1) * GRID_W, a * GRID_W:(a + 1) * GRID_W] = val.astype(BF16)


def _bias_table(rpb):
    plan = _bias_block_plan()
    heads, n_dr, n_dc = rpb.shape
    rpb_lanes = jnp.pad(rpb, ((0, 0), (0, 0), (0, BIAS_LANES - n_dc)))
    return pl.pallas_call(
        functools.partial(_bias_kernel, plan=plan),
        grid=(heads,),
        in_specs=[pl.BlockSpec((None, n_dr, BIAS_LANES), lambda h: (h, 0, 0))],
        out_specs=pl.BlockSpec((N_BIAS_CASES, None, Q_BLK, KEY_BLK), lambda h: (0, h, 0, 0)),
        out_shape=jax.ShapeDtypeStruct((N_BIAS_CASES, heads, Q_BLK, KEY_BLK), BF16),
        scratch_shapes=[pltpu.VMEM((n_dr, GRID_W, GRID_W), F32)],
        name="bias_table",
    )(rpb_lanes)


def _na_kernel(q_ref, k0_ref, k1_ref, k2_ref, kc_ref, v0_ref, v1_ref, v2_ref, vc_ref, z_ref, bias_ref, o_ref):
    lane = lax.broadcasted_iota(jnp.int32, (1, 2 * NA_HEAD_DIM), 1)
    k_refs = (k0_ref, k1_ref, k2_ref, kc_ref)
    v_refs = (v0_ref, v1_ref, v2_ref, vc_ref)
    ones_tile = jnp.where(lax.broadcasted_iota(jnp.int32, (Q_BLK, 2 * NA_HEAD_DIM), 1) == 0, 1.0, 0.0).astype(BF16)
    for bb, pair in ((bb, pair) for bb in range(q_ref.shape[0]) for pair in range(NA_HEADS // 2)):
        cs = slice(pair * LANES, (pair + 1) * LANES)
        qp = q_ref[bb, :, cs]
        q_stack = jnp.concatenate([jnp.where(lane < NA_HEAD_DIM, qp, jnp.zeros_like(qp)),
                                   jnp.where(lane >= NA_HEAD_DIM, qp, jnp.zeros_like(qp))], axis=0)
        s_both = [_dot_nt(q_stack, r[bb, :, cs]) for r in k_refs]
        e_both = [[], []]
        for hh in range(2):
            s = [sb[hh * Q_BLK:(hh + 1) * Q_BLK] for sb in s_both]
            for t in range(KEY_ROWS // Q_ROWS):
                s[t] = s[t] + bias_ref[2 * pair + hh, :, t * KEY_TILE:(t + 1) * KEY_TILE].astype(F32)
            mx = jnp.max(jnp.maximum(jnp.maximum(s[0], s[1]), jnp.maximum(s[2], s[3])), axis=-1, keepdims=True)
            e_both[hh] = [jnp.exp2((st - mx).astype(BF16)) for st in s]
        o = None
        for t, r in enumerate(v_refs):
            p_stack = jnp.concatenate([e_both[0][t], e_both[1][t]], axis=0)
            c = _dot(p_stack, jnp.concatenate([r[bb, :, cs], ones_tile], axis=1))
            o = c if o is None else o + c
        outs = [o[hh * Q_BLK:(hh + 1) * Q_BLK, :LANES] * (1.0 / o[hh * Q_BLK:(hh + 1) * Q_BLK, LANES:LANES + 1])
                for hh in range(2)]
        acc = jnp.where(lane < NA_HEAD_DIM, outs[0], outs[1])
        z = z_ref[bb, :, cs].astype(F32)
        o_ref[bb, :, cs] = (acc * (z * _sigmoid(z))).astype(BF16)


def _neighborhood_attention(p3, pc3, bias_tab):
    b, s, _ = p3.shape
    assert b % NA_BATCH == 0
    n_blk = s // Q_BLK

    def key_blk(i):
        return jnp.clip(i - 1, 0, n_blk - KEY_ROWS // Q_ROWS)

    def case_of(i):
        return jnp.minimum(i, 2) + jnp.maximum(i - (n_blk - 3), 0)

    cq, ck, cv, cz = COL_Q // 1024, COL_K // 1024, COL_V // 1024, COL_ZNA // 1024
    blk = (NA_BATCH, Q_BLK, 1024)
    in_specs = [pl.BlockSpec(blk, lambda i, j: (j, i, cq))]
    in_specs += [pl.BlockSpec(blk, functools.partial(lambda i, j, t: (j, key_blk(i) + t, ck), t=t)) for t in range(3)]
    in_specs += [pl.BlockSpec(blk, lambda i, j: (j, 0, CCOL_K // 1024))]
    in_specs += [pl.BlockSpec(blk, functools.partial(lambda i, j, t: (j, key_blk(i) + t, cv), t=t)) for t in range(3)]
    in_specs += [pl.BlockSpec(blk, lambda i, j: (j, 0, CCOL_V // 1024))]
    in_specs += [pl.BlockSpec(blk, lambda i, j: (j, i, cz))]
    in_specs += [pl.BlockSpec((None, NA_HEADS, Q_BLK, KEY_BLK), lambda i, j: (case_of(i), 0, 0, 0))]
    return pl.pallas_call(
        _na_kernel,
        grid=(n_blk, b // NA_BATCH),
        in_specs=in_specs,
        out_specs=pl.BlockSpec(blk, lambda i, j: (j, i, 0)),
        out_shape=jax.ShapeDtypeStruct((b, s, NA_HEADS * NA_HEAD_DIM), BF16),
        compiler_params=pltpu.CompilerParams(
            dimension_semantics=("parallel", "parallel"), vmem_limit_bytes=VMEM_LIMIT),
        name="neighborhood_attention",
    )(p3, p3, p3, p3, pc3, p3, p3, p3, pc3, p3, bias_tab)


def _softplus(v):
    return jnp.maximum(v, 0.0) + jnp.log(1.0 + jnp.exp(-jnp.abs(v)))


def _head_stack(v, head_lanes):
    return jnp.concatenate([jnp.where(m, v, jnp.zeros_like(v)) for m in head_lanes], axis=0)


def _ssd_kernel(xs_ref, bm_ref, cm_ref, z_ref, xsc_ref, bmc_ref, dtt_ref, dttc_ref,
                cwx_ref, cwb_ref, cwc_ref, cbx_ref, cbb_ref, cbc_ref,
                alog_ref, dtb_ref, skip_ref, ng_ref, o_ref,
                pad_ref, padc_ref, tap_ref, ux_ref, ubt_ref, uc_ref, uxc_ref, ubtc_ref,
                rows_ref, cols_ref, dec_ref, rowsc_ref, decc_ref, y_ref, st_ref, e_ref):
    q = SSM_CHUNK
    hpg = HEADS_PER_GROUP
    nh = N_DIRS * hpg
    s_len = xs_ref.shape[0]
    c_len = xsc_ref.shape[0]
    n_lat = s_len // q
    n_ctx = c_len // q
    halo = CONV_HALO
    blk = CONV_BLK
    win = blk + 2 * halo

    zeros_halo = jnp.zeros((halo, pad_ref.shape[1]), BF16)
    pad_ref[0:halo, :] = zeros_halo
    pad_ref[s_len + halo:s_len + 2 * halo, :] = zeros_halo
    pad_ref[halo:s_len + halo, 0:GROUP_X] = xs_ref[...]
    pad_ref[halo:s_len + halo, GROUP_X:GROUP_X + SSM_STATE] = bm_ref[...]
    pad_ref[halo:s_len + halo, GROUP_X + SSM_STATE:] = cm_ref[...]
    zeros_halo_c = jnp.zeros((halo, padc_ref.shape[1]), BF16)
    padc_ref[0:halo, :] = zeros_halo_c
    padc_ref[c_len + halo:c_len + 2 * halo, :] = zeros_halo_c
    padc_ref[halo:c_len + halo, 0:GROUP_X] = xsc_ref[...]
    padc_ref[halo:c_len + halo, GROUP_X:] = bmc_ref[...]

    cw = jnp.concatenate([cwx_ref[...], cwb_ref[...], cwc_ref[...]], axis=1)
    cb = jnp.concatenate([cbx_ref[...], cbb_ref[...], cbc_ref[...]], axis=1)
    for k in range(SSM_CONV):
        tap_ref[k] = jnp.broadcast_to(cw[k:k + 1, :], (win, cw.shape[1])).astype(BF16)
    off = lax.broadcasted_iota(jnp.int32, (blk, SSM_CONV * win), 1) - lax.broadcasted_iota(
        jnp.int32, (blk, SSM_CONV * win), 0)
    hit = off == halo - SSM_CONV // 2
    for k in range(1, SSM_CONV):
        hit = hit | (off == k * win + halo - SSM_CONV // 2 + k)
    shift = jnp.where(hit, 1.0, 0.0).astype(BF16)

    def conv_block(src_ref, start, width):
        w = src_ref[pl.ds(start, win), :]
        stack = jnp.concatenate([w * tap_ref[k, :, 0:width] for k in range(SSM_CONV)], axis=0)
        acc = _dot(shift, stack) + cb[:, :width]
        return acc * _sigmoid(acc)

    def conv_chunk(src_ref, start, width):
        return jnp.concatenate([conv_block(src_ref, start + i * blk, width) for i in range(q // blk)], axis=0)

    def conv_lat(c, carry):
        start = pl.multiple_of(c * q, q)
        u = conv_chunk(pad_ref, start, GROUP_X + 2 * SSM_STATE)
        ux_ref[pl.ds(start, q), :] = u[:, :GROUP_X]
        ubt_ref[c] = u[:, GROUP_X:GROUP_X + SSM_STATE].T.astype(BF16)
        uc_ref[pl.ds(start, q), :] = u[:, GROUP_X + SSM_STATE:].astype(BF16)
        return carry

    lax.fori_loop(0, n_lat, conv_lat, 0, unroll=True)
    for c in range(n_ctx):
        u = conv_chunk(padc_ref, c * q, GROUP_X + SSM_STATE)
        uxc_ref[c * q:(c + 1) * q, :] = u[:, :GROUP_X]
        ubtc_ref[c] = u[:, GROUP_X:].T.astype(BF16)

    ki = lax.broadcasted_iota(jnp.int32, (q, q), 0)
    ii = lax.broadcasted_iota(jnp.int32, (q, q), 1)
    tri_f = (ki <= ii).astype(F32)
    tri_b = (ki >= ii).astype(F32)
    lane_x = lax.broadcasted_iota(jnp.int32, (1, GROUP_X), 1)
    head_lanes = [(lane_x >= r * SSM_HEAD_DIM) & (lane_x < (r + 1) * SSM_HEAD_DIM) for r in range(hpg)]
    neg_a2 = -jnp.exp(alog_ref[...]) * LOG2E
    dt_bias = dtb_ref[...]

    def dt_forms(raw, n, rows_out, cols_out, dec_out):
        stack = lambda v: jnp.concatenate([v[:, c * q:(c + 1) * q] for c in range(n)], axis=0)
        dt_all = _softplus(raw + dt_bias)
        dt = stack(dt_all)
        a = stack(neg_a2 * dt_all)
        is_fwd = lax.broadcasted_iota(jnp.int32, (n * nh, 1), 0) % nh < hpg
        cum_f = jnp.dot(a, tri_f, preferred_element_type=F32, precision=lax.Precision.HIGHEST)
        cum_b = jnp.dot(a, tri_b, preferred_element_type=F32, precision=lax.Precision.HIGHEST)
        acum = jnp.where(is_fwd, cum_f, cum_b)
        atot = jnp.where(is_fwd, acum[:, q - 1:q], acum[:, 0:1])
        sw = dt * jnp.exp2(atot - acum)
        dec = jnp.exp2(atot)
        for c in range(n):
            sl = slice(c * nh, (c + 1) * nh)
            rows_out[c] = jnp.concatenate([dt[sl], acum[sl], sw[sl]], axis=0)
            if cols_out is not None:
                cols_out[c] = acum[sl].T
            dec_rows = []
            for d in range(N_DIRS):
                base = c * nh + d * hpg
                row = jnp.broadcast_to(dec[base + hpg - 1:base + hpg, :], (1, GROUP_X))
                for r in range(hpg - 2, -1, -1):
                    row = jnp.where(lane_x < (r + 1) * SSM_HEAD_DIM, dec[base + r:base + r + 1, :], row)
                dec_rows.append(row)
            dec_out[c] = jnp.concatenate(dec_rows, axis=0)

    dt_forms(dtt_ref[...], n_lat, rows_ref, cols_ref, dec_ref)
    dt_forms(dttc_ref[...], n_ctx, rowsc_ref, None, decc_ref)

    def state_term(d, rows_b, bt, xm):
        parts = [bt * rows_b[2 * N_DIRS * hpg + d * hpg + r:2 * N_DIRS * hpg + d * hpg + r + 1, :]
                 for r in range(hpg)]
        return _dot(jnp.concatenate(parts, axis=1), xm)

    h_init = []
    for d in range(N_DIRS):
        h = jnp.zeros((SSM_STATE, GROUP_X), F32)
        for c in (range(n_ctx) if d == 0 else reversed(range(n_ctx))):
            xm = _head_stack(uxc_ref[c * q:(c + 1) * q, :].astype(BF16), head_lanes)
            h = decc_ref[c][d:d + 1, :] * h + state_term(d, rowsc_ref[c].astype(BF16), ubtc_ref[c], xm)
        h_init.append(h)

    skip = skip_ref[...]
    norm_g = ng_ref[...]
    keep = (ki >= ii, ki <= ii)
    low_lanes = lax.broadcasted_iota(jnp.int32, (1, q), 1) < SSM_HEAD_DIM

    def chunk_body(c, carry):
        start = pl.multiple_of(c * q, q)
        rows = rows_ref[c]
        cols = cols_ref[c]
        xs = ux_ref[pl.ds(start, q), :]
        xm = _head_stack(xs.astype(BF16), head_lanes)
        cmat = uc_ref[pl.ds(start, q), :]
        bt = ubt_ref[c]
        cbm = _dot(cmat, bt)
        cbm_b = cbm.astype(BF16)
        rows_b = rows.astype(BF16)
        zero_b = jnp.zeros((q, q), BF16)
        m_parts = []
        acols = [[], []]
        for r in range(hpg):
            t = None
            for d in range(N_DIRS):
                k = d * hpg + r
                acol = jnp.broadcast_to(cols[:, k:k + 1], (q, q))
                arow = rows[N_DIRS * hpg + k:N_DIRS * hpg + k + 1, :]
                term = jnp.where(keep[d], jnp.exp2((acol - arow).astype(BF16)) * rows_b[k:k + 1, :], zero_b)
                t = term if t is None else t + term
                acols[d].append(acol)
            m_parts.append(cbm_b * t)
        y_ref[pl.ds(start, q), :] = xs * skip + _dot(jnp.concatenate(m_parts, axis=1), xm)
        for d in range(N_DIRS):
            st_ref[d, c] = state_term(d, rows_b, bt, xm)
            halves = [jnp.where(low_lanes, acols[d][2 * i], acols[d][2 * i + 1]) for i in range(hpg // 2)]
            e_ref[d, c] = jnp.exp2(jnp.concatenate(halves, axis=1))
        return carry

    lax.fori_loop(0, n_lat, chunk_body, 0, unroll=SSD_UNROLL)

    def scan_body(j, carry, finalize):
        new = []
        chunk_ids = (j, n_lat - 1 - j)
        for d in range(N_DIRS):
            h = carry[d]
            c = chunk_ids[d]
            start = pl.multiple_of(c * q, q)
            y_off = _dot(uc_ref[pl.ds(start, q), :], h.astype(BF16)) * e_ref[d, c]
            y_ref[pl.ds(start, q), :] = y_ref[pl.ds(start, q), :] + y_off
            new.append(dec_ref[c][d:d + 1, :] * h + st_ref[d, c])
        if finalize:
            for c in chunk_ids:
                start = pl.multiple_of(c * q, q)
                z = z_ref[pl.ds(start, q), :].astype(F32)
                u = y_ref[pl.ds(start, q), :] * (z * _sigmoid(z))
                u = u * lax.rsqrt(jnp.mean(u * u, axis=-1, keepdims=True) + EPS)
                o_ref[pl.ds(start, q), :] = (u * norm_g).astype(BF16)
        return tuple(new)

    half = n_lat // 2
    carry = lax.fori_loop(0, half, functools.partial(scan_body, finalize=False), tuple(h_init), unroll=SSD_UNROLL)
    lax.fori_loop(half, n_lat, functools.partial(scan_body, finalize=True), carry, unroll=SSD_UNROLL)


def _ssd_mixer(p3, pc3, dtt, dttc, conv_w, conv_b, alog8, dtb8, skip_exp, norm_g):
    b, s, _ = p3.shape
    c_len = pc3.shape[1]
    g = SSM_GROUPS
    n_lat, n_ctx = s // SSM_CHUNK, c_len // SSM_CHUNK
    assert n_lat % 2 == 0
    nx = GROUP_X
    ns = SSM_STATE
    nh = N_DIRS * HEADS_PER_GROUP
    in_specs = [
        pl.BlockSpec((None, s, nx), lambda i, j: (i, 0, COL_X // nx + j)),
        pl.BlockSpec((None, s, ns), lambda i, j: (i, 0, COL_B // ns + j)),
        pl.BlockSpec((None, s, ns), lambda i, j: (i, 0, COL_C // ns + j)),
        pl.BlockSpec((None, s, nx), lambda i, j: (i, 0, COL_ZSSM // nx + j)),
        pl.BlockSpec((None, c_len, nx), lambda i, j: (i, 0, CCOL_X // nx + j)),
        pl.BlockSpec((None, c_len, ns), lambda i, j: (i, 0, CCOL_B // ns + j)),
        pl.BlockSpec((None, nh, s), lambda i, j: (i, j, 0)),
        pl.BlockSpec((None, nh, c_len), lambda i, j: (i, j, 0)),
        pl.BlockSpec((SSM_CONV, nx), lambda i, j: (0, j)),
        pl.BlockSpec((SSM_CONV, ns), lambda i, j: (0, SSM_WIDTH // ns + j)),
        pl.BlockSpec((SSM_CONV, ns), lambda i, j: (0, SSM_WIDTH // ns + g + j)),
        pl.BlockSpec((1, nx), lambda i, j: (0, j)),
        pl.BlockSpec((1, ns), lambda i, j: (0, SSM_WIDTH // ns + j)),
        pl.BlockSpec((1, ns), lambda i, j: (0, SSM_WIDTH // ns + g + j)),
        pl.BlockSpec((None, nh, 1), lambda i, j: (j, 0, 0)),
        pl.BlockSpec((None, nh, 1), lambda i, j: (j, 0, 0)),
        pl.BlockSpec((None, 1, nx), lambda i, j: (j, 0, 0)),
        pl.BlockSpec((1, nx), lambda i, j: (0, j)),
    ]
    halo = CONV_HALO
    scratch = [
        pltpu.VMEM((s + 2 * halo, nx + 2 * ns), BF16),
        pltpu.VMEM((c_len + 2 * halo, nx + ns), BF16),
        pltpu.VMEM((SSM_CONV, CONV_BLK + 2 * halo, nx + 2 * ns), BF16),
        pltpu.VMEM((s, nx), F32),
        pltpu.VMEM((n_lat, ns, SSM_CHUNK), BF16),
        pltpu.VMEM((s, ns), BF16),
        pltpu.VMEM((c_len, nx), F32),
        pltpu.VMEM((n_ctx, ns, SSM_CHUNK), BF16),
        pltpu.VMEM((n_lat, 3 * nh, SSM_CHUNK), F32),
        pltpu.VMEM((n_lat, SSM_CHUNK, nh), F32),
        pltpu.VMEM((n_lat, N_DIRS, nx), F32),
        pltpu.VMEM((n_ctx, 3 * nh, SSM_CHUNK), F32),
        pltpu.VMEM((n_ctx, N_DIRS, nx), F32),
        pltpu.VMEM((s, nx), F32),
        pltpu.VMEM((N_DIRS, n_lat, ns, nx), F32),
        pltpu.VMEM((N_DIRS, n_lat, SSM_CHUNK, nx), F32),
    ]
    return pl.pallas_call(
        _ssd_kernel,
        grid=(b, g),
        in_specs=in_specs,
        out_specs=pl.BlockSpec((None, s, nx), lambda i, j: (i, 0, j)),
        out_shape=jax.ShapeDtypeStruct((b, s, SSM_WIDTH), BF16),
        scratch_shapes=scratch,
        compiler_params=pltpu.CompilerParams(
            dimension_semantics=("parallel", "parallel"), vmem_limit_bytes=VMEM_LIMIT),
        name="ssd_mixer",
    )(p3, p3, p3, p3, pc3, pc3, dtt, dttc, conv_w, conv_w, conv_w, conv_b, conv_b, conv_b,
      alog8, dtb8, skip_exp, norm_g)


def _merge_kernel(yna_ref, yssm_ref, gna_ref, gssm_ref, x_ref, gate_ref, wna_ref, wssm_ref, wout_ref, gpost_ref, o_ref):
    a = _dot(yna_ref[...], wna_ref[...])
    s = _dot(yssm_ref[...], wssm_ref[...])
    m = _sigmoid(gna_ref[...].astype(F32)) * a + _sigmoid(gssm_ref[...].astype(F32)) * s
    o = _dot(m.astype(BF16), wout_ref[...])
    r = o * lax.rsqrt(jnp.mean(o * o, axis=-1, keepdims=True) + EPS) * gpost_ref[...]
    o_ref[...] = x_ref[...] + gate_ref[...] * r


def _merge(y_na, y_ssm, p3, x3, gate, w_na, w_ssm, w_out, g_post, tm):
    b, s, d = x3.shape
    const = lambda i, j: (0, 0)
    return pl.pallas_call(
        _merge_kernel,
        grid=(b, s // tm),
        in_specs=[pl.BlockSpec((None, tm, y_na.shape[-1]), lambda i, j: (i, j, 0)),
                  pl.BlockSpec((None, tm, y_ssm.shape[-1]), lambda i, j: (i, j, 0)),
                  pl.BlockSpec((None, tm, d), lambda i, j: (i, j, COL_GNA // d)),
                  pl.BlockSpec((None, tm, d), lambda i, j: (i, j, COL_GSSM // d)),
                  pl.BlockSpec((None, tm, d), lambda i, j: (i, j, 0)),
                  pl.BlockSpec((None, 1, d), lambda i, j: (i, 0, 0)),
                  pl.BlockSpec(w_na.shape, const, pipeline_mode=pl.Buffered(1)),
                  pl.BlockSpec(w_ssm.shape, const, pipeline_mode=pl.Buffered(1)),
                  pl.BlockSpec(w_out.shape, const, pipeline_mode=pl.Buffered(1)),
                  pl.BlockSpec((1, d), const)],
        out_specs=pl.BlockSpec((None, tm, d), lambda i, j: (i, j, 0)),
        out_shape=jax.ShapeDtypeStruct((b, s, d), x3.dtype),
        compiler_params=pltpu.CompilerParams(
            dimension_semantics=("parallel", "parallel"), vmem_limit_bytes=VMEM_LIMIT),
        name="merge_out",
    )(y_na, y_ssm, p3, p3, x3, gate, w_na, w_ssm, w_out, g_post)


def kernel(x, c, ctx, c_ctx, w_mod, b_mod, g_pre, g_post, w_in, conv_w, conv_b, a_log, dt_bias, d_skip,
           ssm_norm_g, rpb, w_na_out, w_ssm_out, w_out):
    assert w_mod.shape[0] == 1, "single-layer block"
    b, s, d = x.shape
    g, hpg = SSM_GROUPS, HEADS_PER_GROUP

    w_main, w_dt = _prep_w_in(jnp.transpose(w_in[0]), COL_GNA, N_DIRS * SSM_HEADS, COL_K, NA_HEAD_DIM ** -0.5 * LOG2E)
    per_group = lambda p: jnp.transpose(p.reshape(N_DIRS, g, hpg), (1, 0, 2)).reshape(g, N_DIRS * hpg, 1)
    alog8 = per_group(a_log[0])
    dtb8 = per_group(dt_bias[0])
    skip_exp = jnp.repeat(d_skip[0], SSM_HEAD_DIM).reshape(g, 1, GROUP_X)

    cond = jnp.zeros((16, d), F32).at[:b].set(c).at[b].set(c_ctx)
    mod = _adaln(cond, w_mod[0], b_mod[0])
    shift_l, scale_l, gate_l = (mod[:b, k * d:(k + 1) * d].reshape(b, 1, d) for k in range(3))
    shift_c, scale_c = (jnp.broadcast_to(mod[b, k * d:(k + 1) * d].reshape(1, 1, d), (b, 1, d)) for k in range(2))

    p3, dtt = _projection(x, shift_l, scale_l, g_pre, w_main, w_dt, ((0, NP_COLS),), tm=PROJ_TM)
    pc3, dttc = _projection(ctx, shift_c, scale_c, g_pre, w_main, w_dt,
                            ((COL_K, COL_ZNA), (COL_X, COL_GNA)), tm=PROJ_TM)

    bias_tab = _bias_table(rpb[0])
    y_na = _neighborhood_attention(p3, pc3, bias_tab)

    y_ssm = _ssd_mixer(p3, pc3, dtt, dttc, conv_w[0], conv_b[0].reshape(1, -1), alog8, dtb8, skip_exp,
                       ssm_norm_g)

    return _merge(y_na, y_ssm, p3, x, gate_l, w_na_out[0].astype(BF16), w_ssm_out[0].astype(BF16),
                  w_out[0].astype(BF16), g_post, tm=MERGE_TM)
```

```python
import functools

import numpy as np
import jax
import jax.numpy as jnp
from jax import lax
from jax.experimental import pallas as pl
from jax.experimental.pallas import tpu as pltpu

F32 = jnp.float32
BF16 = jnp.bfloat16

D_MODEL = 1024
GRID_W = 64
NA_HEADS = 16
NA_HEAD_DIM = 64
WIN_ROWS = 8
WIN_COLS = 16
SSM_WIDTH = 2 * D_MODEL
SSM_HEAD_DIM = 64
SSM_HEADS = SSM_WIDTH // SSM_HEAD_DIM
SSM_GROUPS = 8
HEADS_PER_GROUP = SSM_HEADS // SSM_GROUPS
SSM_STATE = 128
SSM_CONV = 5
SSM_CHUNK = 128
CONV_HALO = 16
CONV_BLK = 64
SSD_UNROLL = 8
N_DIRS = 2
GROUP_X = HEADS_PER_GROUP * SSM_HEAD_DIM
EPS = 1e-6
NEG = -1e30
LOG2E = 1.4426950408889634

COL_Q, COL_K, COL_V, COL_ZNA = 0, 1024, 2048, 3072
COL_ZSSM, COL_X, COL_B, COL_C = 4096, 6144, 8192, 9216
COL_GNA, COL_GSSM, NP_COLS = 10240, 11264, 12288
CCOL_K, CCOL_V, CCOL_X, CCOL_B, NPC_COLS = 0, 1024, 2048, 4096, 6144

Q_ROWS = 4
Q_BLK = Q_ROWS * GRID_W
KEY_ROWS = 12
KEY_BLK = KEY_ROWS * GRID_W
N_BIAS_CASES = 5
BIAS_CASE_BLOCK = (0, 1, 2, 6, 7)
BIAS_LANES = 128

V7X_VMEM_BYTES = 64 * 1024 * 1024
VMEM_LIMIT = V7X_VMEM_BYTES * 7 // 8
LANES = 128
PROJ_TM = 512
PROJ_CHUNK = 512
MERGE_TM = 1024
ADALN_TN = 1024
KEY_TILE = Q_BLK
NA_BATCH = 2


def _sigmoid(v):
    return 1.0 / (1.0 + jnp.exp(-v))


def _dot(a, b):
    return jnp.dot(a, b, preferred_element_type=F32)


def _dot_nt(a, b):
    return lax.dot_general(a, b, (((1,), (1,)), ((), ())), preferred_element_type=F32)


def _adaln_kernel(cond_ref, w_ref, b_ref, o_ref):
    cnd = cond_ref[...]
    act = cnd * _sigmoid(cnd)
    o_ref[...] = jnp.dot(act, w_ref[...], preferred_element_type=F32,
                         precision=lax.Precision.HIGHEST) + b_ref[...]


def _adaln(cond, w_mod, b_mod):
    rows, d = cond.shape
    n = w_mod.shape[1]
    tn = ADALN_TN
    return pl.pallas_call(
        _adaln_kernel,
        grid=(n // tn,),
        in_specs=[pl.BlockSpec((rows, d), lambda j: (0, 0)),
                  pl.BlockSpec((d, tn), lambda j: (0, j)),
                  pl.BlockSpec((1, tn), lambda j: (0, j))],
        out_specs=pl.BlockSpec((rows, tn), lambda j: (0, j)),
        out_shape=jax.ShapeDtypeStruct((rows, n), F32),
        name="adaln",
    )(cond, w_mod, b_mod.reshape(1, n))


W_TILE = 512
DT_PAD = 128


def _wprep_kernel(a_ref, b_ref, dt_ref, w_ref, wdt_ref, *, n_aligned, shift, q_tiles, q_scale):
    j = pl.program_id(0)

    @pl.when(j < n_aligned)
    def _():
        scale = jnp.where(j < q_tiles, q_scale, 1.0)
        w_ref[...] = (a_ref[...] * scale).T.astype(BF16)

    @pl.when(j >= n_aligned)
    def _():
        w_ref[...] = jnp.concatenate([a_ref[shift:, :], b_ref[...]], axis=0).T.astype(BF16)

    @pl.when(j == 0)
    def _():
        dst = lax.broadcasted_iota(jnp.int32, (DT_PAD, DT_PAD), 1)
        src = lax.broadcasted_iota(jnp.int32, (DT_PAD, DT_PAD), 0)
        per_group = N_DIRS * HEADS_PER_GROUP
        grp, rem = dst // per_group, dst % per_group
        want = (rem // HEADS_PER_GROUP) * SSM_HEADS + grp * HEADS_PER_GROUP + rem % HEADS_PER_GROUP
        perm = jnp.where((src == want) & (dst < N_DIRS * SSM_HEADS), 1.0, 0.0).astype(BF16)
        wdt_ref[...] = _dot(dt_ref[...].T.astype(BF16), perm).astype(BF16)


def _prep_w_in(w_in_t, col_dt, n_dt, q_cols, q_scale):
    p, d = w_in_t.shape
    n_out = p - n_dt
    assert col_dt % W_TILE == 0 and n_out % W_TILE == 0 and q_cols % W_TILE == 0
    assert W_TILE % n_dt == 0 and n_dt % 8 == 0 and n_dt <= DT_PAD and col_dt % DT_PAD == 0
    return pl.pallas_call(
        functools.partial(_wprep_kernel, n_aligned=col_dt // W_TILE, shift=n_dt, q_tiles=q_cols // W_TILE,
                          q_scale=q_scale),
        grid=(n_out // W_TILE,),
        in_specs=[pl.BlockSpec((W_TILE, d), lambda j: (j, 0)),
                  pl.BlockSpec((n_dt, d), lambda j: ((j + 1) * (W_TILE // n_dt), 0)),
                  pl.BlockSpec((DT_PAD, d), lambda j: (col_dt // DT_PAD, 0))],
        out_specs=[pl.BlockSpec((d, W_TILE), lambda j: (0, j)),
                   pl.BlockSpec((d, DT_PAD), lambda j: (0, 0))],
        out_shape=[jax.ShapeDtypeStruct((d, n_out), BF16), jax.ShapeDtypeStruct((d, DT_PAD), BF16)],
        name="w_in_prep",
    )(w_in_t, w_in_t, w_in_t)


def _proj_kernel(x_ref, shift_ref, scale_ref, g_ref, w_ref, wdt_ref, p_ref, dtt_ref, *, chunks):
    x = x_ref[...]
    ms = jnp.mean(x * x, axis=-1, keepdims=True)
    h = x * lax.rsqrt(ms + EPS) * g_ref[...]
    h = h * (1.0 + scale_ref[...]) + shift_ref[...]
    hb = h.astype(BF16)
    for src, dst, width in chunks:
        p_ref[:, dst:dst + width] = _dot(hb, w_ref[:, src:src + width]).astype(BF16)
    dt = _dot(hb, wdt_ref[...])
    dtt_ref[...] = dt.T[:N_DIRS * SSM_HEADS]


def _projection(x3, shift, scale, g_pre, w_main, w_dt, col_ranges, tm):
    b, l, d = x3.shape
    tm = min(tm, l)
    chunks, dst = [], 0
    cw = PROJ_CHUNK
    for lo, hi in col_ranges:
        for src in range(lo, hi, cw):
            chunks.append((src, dst, cw))
            dst += cw
    n_out = dst
    n_dt = N_DIRS * SSM_HEADS
    return pl.pallas_call(
        functools.partial(_proj_kernel, chunks=tuple(chunks)),
        grid=(b, l // tm),
        in_specs=[pl.BlockSpec((None, tm, d), lambda i, j: (i, j, 0)),
                  pl.BlockSpec((None, 1, d), lambda i, j: (i, 0, 0)),
                  pl.BlockSpec((None, 1, d), lambda i, j: (i, 0, 0)),
                  pl.BlockSpec((1, d), lambda i, j: (0, 0)),
                  pl.BlockSpec(w_main.shape, lambda i, j: (0, 0), pipeline_mode=pl.Buffered(1)),
                  pl.BlockSpec(w_dt.shape, lambda i, j: (0, 0), pipeline_mode=pl.Buffered(1))],
        out_specs=[pl.BlockSpec((None, tm, n_out), lambda i, j: (i, j, 0)),
                   pl.BlockSpec((None, n_dt, tm), lambda i, j: (i, 0, j))],
        out_shape=[jax.ShapeDtypeStruct((b, l, n_out), BF16),
                   jax.ShapeDtypeStruct((b, n_dt, l), F32)],
        compiler_params=pltpu.CompilerParams(
            dimension_semantics=("parallel", "parallel"), vmem_limit_bytes=VMEM_LIMIT),
        name="projection",
    )(x3, shift, scale, g_pre, w_main, w_dt)


def _bias_block_plan():
    rows = 2048 // GRID_W
    plan = np.full((N_BIAS_CASES, Q_ROWS, KEY_ROWS), -1, np.int64)
    for case, blk in enumerate(BIAS_CASE_BLOCK):
        u0 = int(np.clip(Q_ROWS * blk - WIN_ROWS // 2, 0, rows - KEY_ROWS))
        for rho in range(Q_ROWS):
            r = Q_ROWS * blk + rho
            r0 = int(np.clip(r - WIN_ROWS // 2, 0, rows - WIN_ROWS))
            for a in range(KEY_ROWS):
                krow = u0 + a
                if r0 <= krow < r0 + WIN_ROWS:
                    plan[case, rho, a] = krow - r + WIN_ROWS - 1
    return plan


def _bias_kernel(rpb_ref, o_ref, toep_ref, *, plan):
    n_dr = 2 * WIN_ROWS - 1
    qc = lax.broadcasted_iota(jnp.int32, (GRID_W, GRID_W), 0)
    kc = lax.broadcasted_iota(jnp.int32, (GRID_W, GRID_W), 1)
    c0 = jnp.clip(qc - WIN_COLS // 2, 0, GRID_W - WIN_COLS)
    in_win = (kc >= c0) & (kc < c0 + WIN_COLS)
    for dr in range(n_dr):
        rows = jnp.broadcast_to(rpb_ref[dr:dr + 1, :], (GRID_W, BIAS_LANES))
        toep = pltpu.roll(rows, BIAS_LANES - (WIN_COLS - 1), 1, stride=1, stride_axis=0)[:, :GRID_W]
        toep_ref[dr] = jnp.where(in_win, toep * LOG2E, NEG)
    neg = jnp.full((GRID_W, GRID_W), NEG, F32)
    for case in range(N_BIAS_CASES):
        for rho in range(Q_ROWS):
            for a in range(KEY_ROWS):
                dr = int(plan[case, rho, a])
                val = neg if dr < 0 else toep_ref[dr]
                o_ref[case, rho * GRID_W:(rho + 1) * GRID_W, a * GRID_W:(a + 1) * GRID_W] = val.astype(BF16)


def _bias_table(rpb):
    plan = _bias_block_plan()
    heads, n_dr, n_dc = rpb.shape
    rpb_lanes = jnp.pad(rpb, ((0, 0), (0, 0), (0, BIAS_LANES - n_dc)))
    return pl.pallas_call(
        functools.partial(_bias_kernel, plan=plan),
        grid=(heads,),
        in_specs=[pl.BlockSpec((None, n_dr, BIAS_LANES), lambda h: (h, 0, 0))],
        out_specs=pl.BlockSpec((N_BIAS_CASES, None, Q_BLK, KEY_BLK), lambda h: (0, h, 0, 0)),
        out_shape=jax.ShapeDtypeStruct((N_BIAS_CASES, heads, Q_BLK, KEY_BLK), BF16),
        scratch_shapes=[pltpu.VMEM((n_dr, GRID_W, GRID_W), F32)],
        name="bias_table",
    )(rpb_lanes)


def _na_kernel(q_ref, k0_ref, k1_ref, k2_ref, kc_ref, v0_ref, v1_ref, v2_ref, vc_ref, z_ref, bias_ref, o_ref):
    lane = lax.broadcasted_iota(jnp.int32, (1, 2 * NA_HEAD_DIM), 1)
    k_refs = (k0_ref, k1_ref, k2_ref, kc_ref)
    v_refs = (v0_ref, v1_ref, v2_ref, vc_ref)
    ones_tile = jnp.where(lax.broadcasted_iota(jnp.int32, (Q_BLK, 2 * NA_HEAD_DIM), 1) == 0, 1.0, 0.0).astype(BF16)
    for bb, pair in ((bb, pair) for bb in range(q_ref.shape[0]) for pair in range(NA_HEADS // 2)):
        cs = slice(pair * LANES, (pair + 1) * LANES)
        qp = q_ref[bb, :, cs]
        q_stack = jnp.concatenate([jnp.where(lane < NA_HEAD_DIM, qp, jnp.zeros_like(qp)),
                                   jnp.where(lane >= NA_HEAD_DIM, qp, jnp.zeros_like(qp))], axis=0)
        s_both = [_dot_nt(q_stack, r[bb, :, cs]) for r in k_refs]
        e_both = [[], []]
        for hh in range(2):
            s = [sb[hh * Q_BLK:(hh + 1) * Q_BLK] for sb in s_both]
            for t in range(KEY_ROWS // Q_ROWS):
                s[t] = s[t] + bias_ref[2 * pair + hh, :, t * KEY_TILE:(t + 1) * KEY_TILE].astype(F32)
            mx = jnp.max(jnp.maximum(jnp.maximum(s[0], s[1]), jnp.maximum(s[2], s[3])), axis=-1, keepdims=True)
            e_both[hh] = [jnp.exp2((st - mx).astype(BF16)) for st in s]
        o = None
        for t, r in enumerate(v_refs):
            p_stack = jnp.concatenate([e_both[0][t], e_both[1][t]], axis=0)
            c = _dot(p_stack, jnp.concatenate([r[bb, :, cs], ones_tile], axis=1))
            o = c if o is None else o + c
        outs = [o[hh * Q_BLK:(hh + 1) * Q_BLK, :LANES] * (1.0 / o[hh * Q_BLK:(hh + 1) * Q_BLK, LANES:LANES + 1])
                for hh in range(2)]
        acc = jnp.where(lane < NA_HEAD_DIM, outs[0], outs[1])
        z = z_ref[bb, :, cs].astype(F32)
        o_ref[bb, :, cs] = (acc * (z * _sigmoid(z))).astype(BF16)


def _neighborhood_attention(p3, pc3, bias_tab):
    b, s, _ = p3.shape
    assert b % NA_BATCH == 0
    n_blk = s // Q_BLK

    def key_blk(i):
        return jnp.clip(i - 1, 0, n_blk - KEY_ROWS // Q_ROWS)

    def case_of(i):
        return jnp.minimum(i, 2) + jnp.maximum(i - (n_blk - 3), 0)

    cq, ck, cv, cz = COL_Q // 1024, COL_K // 1024, COL_V // 1024, COL_ZNA // 1024
    blk = (NA_BATCH, Q_BLK, 1024)
    in_specs = [pl.BlockSpec(blk, lambda i, j: (j, i, cq))]
    in_specs += [pl.BlockSpec(blk, functools.partial(lambda i, j, t: (j, key_blk(i) + t, ck), t=t)) for t in range(3)]
    in_specs += [pl.BlockSpec(blk, lambda i, j: (j, 0, CCOL_K // 1024))]
    in_specs += [pl.BlockSpec(blk, functools.partial(lambda i, j, t: (j, key_blk(i) + t, cv), t=t)) for t in range(3)]
    in_specs += [pl.BlockSpec(blk, lambda i, j: (j, 0, CCOL_V // 1024))]
    in_specs += [pl.BlockSpec(blk, lambda i, j: (j, i, cz))]
    in_specs += [pl.BlockSpec((None, NA_HEADS, Q_BLK, KEY_BLK), lambda i, j: (case_of(i), 0, 0, 0))]
    return pl.pallas_call(
        _na_kernel,
        grid=(n_blk, b // NA_BATCH),
        in_specs=in_specs,
        out_specs=pl.BlockSpec(blk, lambda i, j: (j, i, 0)),
        out_shape=jax.ShapeDtypeStruct((b, s, NA_HEADS * NA_HEAD_DIM), BF16),
        compiler_params=pltpu.CompilerParams(
            dimension_semantics=("parallel", "parallel"), vmem_limit_bytes=VMEM_LIMIT),
        name="neighborhood_attention",
    )(p3, p3, p3, p3, pc3, p3, p3, p3, pc3, p3, bias_tab)


def _softplus(v):
    return jnp.maximum(v, 0.0) + jnp.log(1.0 + jnp.exp(-jnp.abs(v)))


def _head_stack(v, head_lanes):
    return jnp.concatenate([jnp.where(m, v, jnp.zeros_like(v)) for m in head_lanes], axis=0)


def _ssd_kernel(xs_ref, bm_ref, cm_ref, z_ref, xsc_ref, bmc_ref, dtt_ref, dttc_ref,
                cwx_ref, cwb_ref, cwc_ref, cbx_ref, cbb_ref, cbc_ref,
                alog_ref, dtb_ref, skip_ref, ng_ref, o_ref,
                pad_ref, padc_ref, tap_ref, ux_ref, ubt_ref, uc_ref, uxc_ref, ubtc_ref,
                rows_ref, cols_ref, dec_ref, rowsc_ref, decc_ref, y_ref, st_ref, e_ref):
    q = SSM_CHUNK
    hpg = HEADS_PER_GROUP
    nh = N_DIRS * hpg
    s_len = xs_ref.shape[0]
    c_len = xsc_ref.shape[0]
    n_lat = s_len // q
    n_ctx = c_len // q
    halo = CONV_HALO
    blk = CONV_BLK
    win = blk + 2 * halo

    zeros_halo = jnp.zeros((halo, pad_ref.shape[1]), BF16)
    pad_ref[0:halo, :] = zeros_halo
    pad_ref[s_len + halo:s_len + 2 * halo, :] = zeros_halo
    pad_ref[halo:s_len + halo, 0:GROUP_X] = xs_ref[...]
    pad_ref[halo:s_len + halo, GROUP_X:GROUP_X + SSM_STATE] = bm_ref[...]
    pad_ref[halo:s_len + halo, GROUP_X + SSM_STATE:] = cm_ref[...]
    zeros_halo_c = jnp.zeros((halo, padc_ref.shape[1]), BF16)
    padc_ref[0:halo, :] = zeros_halo_c
    padc_ref[c_len + halo:c_len + 2 * halo, :] = zeros_halo_c
    padc_ref[halo:c_len + halo, 0:GROUP_X] = xsc_ref[...]
    padc_ref[halo:c_len + halo, GROUP_X:] = bmc_ref[...]

    cw = jnp.concatenate([cwx_ref[...], cwb_ref[...], cwc_ref[...]], axis=1)
    cb = jnp.concatenate([cbx_ref[...], cbb_ref[...], cbc_ref[...]], axis=1)
    for k in range(SSM_CONV):
        tap_ref[k] = jnp.broadcast_to(cw[k:k + 1, :], (win, cw.shape[1])).astype(BF16)
    off = lax.broadcasted_iota(jnp.int32, (blk, SSM_CONV * win), 1) - lax.broadcasted_iota(
        jnp.int32, (blk, SSM_CONV * win), 0)
    hit = off == halo - SSM_CONV // 2
    for k in range(1, SSM_CONV):
        hit = hit | (off == k * win + halo - SSM_CONV // 2 + k)
    shift = jnp.where(hit, 1.0, 0.0).astype(BF16)

    def conv_block(src_ref, start, width):
        w = src_ref[pl.ds(start, win), :]
        stack = jnp.concatenate([w * tap_ref[k, :, 0:width] for k in range(SSM_CONV)], axis=0)
        acc = _dot(shift, stack) + cb[:, :width]
        return acc * _sigmoid(acc)

    def conv_chunk(src_ref, start, width):
        return jnp.concatenate([conv_block(src_ref, start + i * blk, width) for i in range(q // blk)], axis=0)

    def conv_lat(c, carry):
        start = pl.multiple_of(c * q, q)
        u = conv_chunk(pad_ref, start, GROUP_X + 2 * SSM_STATE)
        ux_ref[pl.ds(start, q), :] = u[:, :GROUP_X]
        ubt_ref[c] = u[:, GROUP_X:GROUP_X + SSM_STATE].T.astype(BF16)
        uc_ref[pl.ds(start, q), :] = u[:, GROUP_X + SSM_STATE:].astype(BF16)
        return carry

    lax.fori_loop(0, n_lat, conv_lat, 0, unroll=True)
    for c in range(n_ctx):
        u = conv_chunk(padc_ref, c * q, GROUP_X + SSM_STATE)
        uxc_ref[c * q:(c + 1) * q, :] = u[:, :GROUP_X]
        ubtc_ref[c] = u[:, GROUP_X:].T.astype(BF16)

    ki = lax.broadcasted_iota(jnp.int32, (q, q), 0)
    ii = lax.broadcasted_iota(jnp.int32, (q, q), 1)
    tri_f = (ki <= ii).astype(F32)
    tri_b = (ki >= ii).astype(F32)
    lane_x = lax.broadcasted_iota(jnp.int32, (1, GROUP_X), 1)
    head_lanes = [(lane_x >= r * SSM_HEAD_DIM) & (lane_x < (r + 1) * SSM_HEAD_DIM) for r in range(hpg)]
    neg_a2 = -jnp.exp(alog_ref[...]) * LOG2E
    dt_bias = dtb_ref[...]

    def dt_forms(raw, n, rows_out, cols_out, dec_out):
        stack = lambda v: jnp.concatenate([v[:, c * q:(c + 1) * q] for c in range(n)], axis=0)
        dt_all = _softplus(raw + dt_bias)
        dt = stack(dt_all)
        a = stack(neg_a2 * dt_all)
        is_fwd = lax.broadcasted_iota(jnp.int32, (n * nh, 1), 0) % nh < hpg
        cum_f = jnp.dot(a, tri_f, preferred_element_type=F32, precision=lax.Precision.HIGHEST)
        cum_b = jnp.dot(a, tri_b, preferred_element_type=F32, precision=lax.Precision.HIGHEST)
        acum = jnp.where(is_fwd, cum_f, cum_b)
        atot = jnp.where(is_fwd, acum[:, q - 1:q], acum[:, 0:1])
        sw = dt * jnp.exp2(atot - acum)
        dec = jnp.exp2(atot)
        for c in range(n):
            sl = slice(c * nh, (c + 1) * nh)
            rows_out[c] = jnp.concatenate([dt[sl], acum[sl], sw[sl]], axis=0)
            if cols_out is not None:
                cols_out[c] = acum[sl].T
            dec_rows = []
            for d in range(N_DIRS):
                base = c * nh + d * hpg
                row = jnp.broadcast_to(dec[base + hpg - 1:base + hpg, :], (1, GROUP_X))
                for r in range(hpg - 2, -1, -1):
                    row = jnp.where(lane_x < (r + 1) * SSM_HEAD_DIM, dec[base + r:base + r + 1, :], row)
                dec_rows.append(row)
            dec_out[c] = jnp.concatenate(dec_rows, axis=0)

    dt_forms(dtt_ref[...], n_lat, rows_ref, cols_ref, dec_ref)
    dt_forms(dttc_ref[...], n_ctx, rowsc_ref, None, decc_ref)

    def state_term(d, rows_b, bt, xm):
        parts = [bt * rows_b[2 * N_DIRS * hpg + d * hpg + r:2 * N_DIRS * hpg + d * hpg + r + 1, :]
                 for r in range(hpg)]
        return _dot(jnp.concatenate(parts, axis=1), xm)

    h_init = []
    for d in range(N_DIRS):
        h = jnp.zeros((SSM_STATE, GROUP_X), F32)
        for c in (range(n_ctx) if d == 0 else reversed(range(n_ctx))):
            xm = _head_stack(uxc_ref[c * q:(c + 1) * q, :].astype(BF16), head_lanes)
            h = decc_ref[c][d:d + 1, :] * h + state_term(d, rowsc_ref[c].astype(BF16), ubtc_ref[c], xm)
        h_init.append(h)

    skip = skip_ref[...]
    norm_g = ng_ref[...]
    keep = (ki >= ii, ki <= ii)
    low_lanes = lax.broadcasted_iota(jnp.int32, (1, q), 1) < SSM_HEAD_DIM

    def chunk_body(c, carry):
        start = pl.multiple_of(c * q, q)
        rows = rows_ref[c]
        cols = cols_ref[c]
        xs = ux_ref[pl.ds(start, q), :]
        xm = _head_stack(xs.astype(BF16), head_lanes)
        cmat = uc_ref[pl.ds(start, q), :]
        bt = ubt_ref[c]
        cbm = _dot(cmat, bt)
        cbm_b = cbm.astype(BF16)
        rows_b = rows.astype(BF16)
        zero_b = jnp.zeros((q, q), BF16)
        m_parts = []
        acols = [[], []]
        for r in range(hpg):
            t = None
            for d in range(N_DIRS):
                k = d * hpg + r
                acol = jnp.broadcast_to(cols[:, k:k + 1], (q, q))
                arow = rows[N_DIRS * hpg + k:N_DIRS * hpg + k + 1, :]
                term = jnp.where(keep[d], jnp.exp2((acol - arow).astype(BF16)) * rows_b[k:k + 1, :], zero_b)
                t = term if t is None else t + term
                acols[d].append(acol)
            m_parts.append(cbm_b * t)
        y_ref[pl.ds(start, q), :] = xs * skip + _dot(jnp.concatenate(m_parts, axis=1), xm)
        for d in range(N_DIRS):
            st_ref[d, c] = state_term(d, rows_b, bt, xm)
            halves = [jnp.where(low_lanes, acols[d][2 * i], acols[d][2 * i + 1]) for i in range(hpg // 2)]
            e_ref[d, c] = jnp.exp2(jnp.concatenate(halves, axis=1))
        return carry

    lax.fori_loop(0, n_lat, chunk_body, 0, unroll=SSD_UNROLL)

    def scan_body(j, carry, finalize):
        new = []
        chunk_ids = (j, n_lat - 1 - j)
        for d in range(N_DIRS):
            h = carry[d]
            c = chunk_ids[d]
            start = pl.multiple_of(c * q, q)
            y_off = _dot(uc_ref[pl.ds(start, q), :], h.astype(BF16)) * e_ref[d, c]
            y_ref[pl.ds(start, q), :] = y_ref[pl.ds(start, q), :] + y_off
            new.append(dec_ref[c][d:d + 1, :] * h + st_ref[d, c])
        if finalize:
            for c in chunk_ids:
                start = pl.multiple_of(c * q, q)
                z = z_ref[pl.ds(start, q), :].astype(F32)
                u = y_ref[pl.ds(start, q), :] * (z * _sigmoid(z))
                u = u * lax.rsqrt(jnp.mean(u * u, axis=-1, keepdims=True) + EPS)
                o_ref[pl.ds(start, q), :] = (u * norm_g).astype(BF16)
        return tuple(new)

    half = n_lat // 2
    carry = lax.fori_loop(0, half, functools.partial(scan_body, finalize=False), tuple(h_init), unroll=SSD_UNROLL)
    lax.fori_loop(half, n_lat, functools.partial(scan_body, finalize=True), carry, unroll=SSD_UNROLL)


def _ssd_mixer(p3, pc3, dtt, dttc, conv_w, conv_b, alog8, dtb8, skip_exp, norm_g):
    b, s, _ = p3.shape
    c_len = pc3.shape[1]
    g = SSM_GROUPS
    n_lat, n_ctx = s // SSM_CHUNK, c_len // SSM_CHUNK
    assert n_lat % 2 == 0
    nx = GROUP_X
    ns = SSM_STATE
    nh = N_DIRS * HEADS_PER_GROUP
    in_specs = [
        pl.BlockSpec((None, s, nx), lambda i, j: (i, 0, COL_X // nx + j)),
        pl.BlockSpec((None, s, ns), lambda i, j: (i, 0, COL_B // ns + j)),
        pl.BlockSpec((None, s, ns), lambda i, j: (i, 0, COL_C // ns + j)),
        pl.BlockSpec((None, s, nx), lambda i, j: (i, 0, COL_ZSSM // nx + j)),
        pl.BlockSpec((None, c_len, nx), lambda i, j: (i, 0, CCOL_X // nx + j)),
        pl.BlockSpec((None, c_len, ns), lambda i, j: (i, 0, CCOL_B // ns + j)),
        pl.BlockSpec((None, nh, s), lambda i, j: (i, j, 0)),
        pl.BlockSpec((None, nh, c_len), lambda i, j: (i, j, 0)),
        pl.BlockSpec((SSM_CONV, nx), lambda i, j: (0, j)),
        pl.BlockSpec((SSM_CONV, ns), lambda i, j: (0, SSM_WIDTH // ns + j)),
        pl.BlockSpec((SSM_CONV, ns), lambda i, j: (0, SSM_WIDTH // ns + g + j)),
        pl.BlockSpec((1, nx), lambda i, j: (0, j)),
        pl.BlockSpec((1, ns), lambda i, j: (0, SSM_WIDTH // ns + j)),
        pl.BlockSpec((1, ns), lambda i, j: (0, SSM_WIDTH // ns + g + j)),
        pl.BlockSpec((None, nh, 1), lambda i, j: (j, 0, 0)),
        pl.BlockSpec((None, nh, 1), lambda i, j: (j, 0, 0)),
        pl.BlockSpec((None, 1, nx), lambda i, j: (j, 0, 0)),
        pl.BlockSpec((1, nx), lambda i, j: (0, j)),
    ]
    halo = CONV_HALO
    scratch = [
        pltpu.VMEM((s + 2 * halo, nx + 2 * ns), BF16),
        pltpu.VMEM((c_len + 2 * halo, nx + ns), BF16),
        pltpu.VMEM((SSM_CONV, CONV_BLK + 2 * halo, nx + 2 * ns), BF16),
        pltpu.VMEM((s, nx), F32),
        pltpu.VMEM((n_lat, ns, SSM_CHUNK), BF16),
        pltpu.VMEM((s, ns), BF16),
        pltpu.VMEM((c_len, nx), F32),
        pltpu.VMEM((n_ctx, ns, SSM_CHUNK), BF16),
        pltpu.VMEM((n_lat, 3 * nh, SSM_CHUNK), F32),
        pltpu.VMEM((n_lat, SSM_CHUNK, nh), F32),
        pltpu.VMEM((n_lat, N_DIRS, nx), F32),
        pltpu.VMEM((n_ctx, 3 * nh, SSM_CHUNK), F32),
        pltpu.VMEM((n_ctx, N_DIRS, nx), F32),
        pltpu.VMEM((s, nx), F32),
        pltpu.VMEM((N_DIRS, n_lat, ns, nx), F32),
        pltpu.VMEM((N_DIRS, n_lat, SSM_CHUNK, nx), F32),
    ]
    return pl.pallas_call(
        _ssd_kernel,
        grid=(b, g),
        in_specs=in_specs,
        out_specs=pl.BlockSpec((None, s, nx), lambda i, j: (i, 0, j)),
        out_shape=jax.ShapeDtypeStruct((b, s, SSM_WIDTH), BF16),
        scratch_shapes=scratch,
        compiler_params=pltpu.CompilerParams(
            dimension_semantics=("parallel", "parallel"), vmem_limit_bytes=VMEM_LIMIT),
        name="ssd_mixer",
    )(p3, p3, p3, p3, pc3, pc3, dtt, dttc, conv_w, conv_w, conv_w, conv_b, conv_b, conv_b,
      alog8, dtb8, skip_exp, norm_g)


def _merge_kernel(yna_ref, yssm_ref, gna_ref, gssm_ref, x_ref, gate_ref, wna_ref, wssm_ref, wout_ref, gpost_ref, o_ref):
    a = _dot(yna_ref[...], wna_ref[...])
    s = _dot(yssm_ref[...], wssm_ref[...])
    m = _sigmoid(gna_ref[...].astype(F32)) * a + _sigmoid(gssm_ref[...].astype(F32)) * s
    o = _dot(m.astype(BF16), wout_ref[...])
    r = o * lax.rsqrt(jnp.mean(o * o, axis=-1, keepdims=True) + EPS) * gpost_ref[...]
    o_ref[...] = x_ref[...] + gate_ref[...] * r


def _merge(y_na, y_ssm, p3, x3, gate, w_na, w_ssm, w_out, g_post, tm):
    b, s, d = x3.shape
    const = lambda i, j: (0, 0)
    return pl.pallas_call(
        _merge_kernel,
        grid=(b, s // tm),
        in_specs=[pl.BlockSpec((None, tm, y_na.shape[-1]), lambda i, j: (i, j, 0)),
                  pl.BlockSpec((None, tm, y_ssm.shape[-1]), lambda i, j: (i, j, 0)),
                  pl.BlockSpec((None, tm, d), lambda i, j: (i, j, COL_GNA // d)),
                  pl.BlockSpec((None, tm, d), lambda i, j: (i, j, COL_GSSM // d)),
                  pl.BlockSpec((None, tm, d), lambda i, j: (i, j, 0)),
                  pl.BlockSpec((None, 1, d), lambda i, j: (i, 0, 0)),
                  pl.BlockSpec(w_na.shape, const, pipeline_mode=pl.Buffered(1)),
                  pl.BlockSpec(w_ssm.shape, const, pipeline_mode=pl.Buffered(1)),
                  pl.BlockSpec(w_out.shape, const, pipeline_mode=pl.Buffered(1)),
                  pl.BlockSpec((1, d), const)],
        out_specs=pl.BlockSpec((None, tm, d), lambda i, j: (i, j, 0)),
        out_shape=jax.ShapeDtypeStruct((b, s, d), x3.dtype),
        compiler_params=pltpu.CompilerParams(
            dimension_semantics=("parallel", "parallel"), vmem_limit_bytes=VMEM_LIMIT),
        name="merge_out",
    )(y_na, y_ssm, p3, p3, x3, gate, w_na, w_ssm, w_out, g_post)


def kernel(x, c, ctx, c_ctx, w_mod, b_mod, g_pre, g_post, w_in, conv_w, conv_b, a_log, dt_bias, d_skip,
           ssm_norm_g, rpb, w_na_out, w_ssm_out, w_out):
    assert w_mod.shape[0] == 1, "single-layer block"
    b, s, d = x.shape
    g, hpg = SSM_GROUPS, HEADS_PER_GROUP

    w_main, w_dt = _prep_w_in(jnp.transpose(w_in[0]), COL_GNA, N_DIRS * SSM_HEADS, COL_K, NA_HEAD_DIM ** -0.5 * LOG2E)
    per_group = lambda p: jnp.transpose(p.reshape(N_DIRS, g, hpg), (1, 0, 2)).reshape(g, N_DIRS * hpg, 1)
    alog8 = per_group(a_log[0])
    dtb8 = per_group(dt_bias[0])
    skip_exp = jnp.repeat(d_skip[0], SSM_HEAD_DIM).reshape(g, 1, GROUP_X)

    cond = jnp.zeros((16, d), F32).at[:b].set(c).at[b].set(c_ctx)
    mod = _adaln(cond, w_mod[0], b_mod[0])
    shift_l, scale_l, gate_l = (mod[:b, k * d:(k + 1) * d].reshape(b, 1, d) for k in range(3))
    shift_c, scale_c = (jnp.broadcast_to(mod[b, k * d:(k + 1) * d].reshape(1, 1, d), (b, 1, d)) for k in range(2))

    p3, dtt = _projection(x, shift_l, scale_l, g_pre, w_main, w_dt, ((0, NP_COLS),), tm=PROJ_TM)
    pc3, dttc = _projection(ctx, shift_c, scale_c, g_pre, w_main, w_dt,
                            ((COL_K, COL_ZNA), (COL_X, COL_GNA)), tm=PROJ_TM)

    bias_tab = _bias_table(rpb[0])
    y_na = _neighborhood_attention(p3, pc3, bias_tab)

    y_ssm = _ssd_mixer(p3, pc3, dtt, dttc, conv_w[0], conv_b[0].reshape(1, -1), alog8, dtb8, skip_exp,
                       ssm_norm_g)

    return _merge(y_na, y_ssm, p3, x, gate_l, w_na_out[0].astype(BF16), w_ssm_out[0].astype(BF16),
                  w_out[0].astype(BF16), g_post, tm=MERGE_TM)
```

```python
import functools

import numpy as np
import jax
import jax.numpy as jnp
from jax import lax
from jax.experimental import pallas as pl
from jax.experimental.pallas import tpu as pltpu

F32 = jnp.float32
BF16 = jnp.bfloat16

D_MODEL = 1024
GRID_W = 64
NA_HEADS = 16
NA_HEAD_DIM = 64
NA_WIDTH = NA_HEADS * NA_HEAD_DIM
WIN_ROWS = 8
WIN_COLS = 16
SSM_WIDTH = 2 * D_MODEL
SSM_HEAD_DIM = 64
SSM_HEADS = SSM_WIDTH // SSM_HEAD_DIM
SSM_GROUPS = 8
HEADS_PER_GROUP = SSM_HEADS // SSM_GROUPS
SSM_STATE = 128
SSM_CONV = 5
SSM_CHUNK = 128
CONV_HALO = 16
CONV_BLK = 64
SSD_UNROLL = 8
N_DIRS = 2
GROUP_X = HEADS_PER_GROUP * SSM_HEAD_DIM
EPS = 1e-6
NEG = -1e30
LOG2E = 1.4426950408889634

COL_Q, COL_K, COL_V, COL_ZNA = 0, 1024, 2048, 3072
COL_ZSSM, COL_X, COL_B, COL_C = 4096, 6144, 8192, 9216
COL_GNA, COL_GSSM, NP_COLS = 10240, 11264, 12288
CCOL_K, CCOL_V, CCOL_X, CCOL_B, NPC_COLS = 0, 1024, 2048, 4096, 6144

Q_ROWS = 4
Q_BLK = Q_ROWS * GRID_W
KEY_ROWS = 12
KEY_BLK = KEY_ROWS * GRID_W
N_BIAS_CASES = 5
BIAS_CASE_BLOCK = (0, 1, 2, 6, 7)
BIAS_LANES = 128

V7X_VMEM_BYTES = 64 * 1024 * 1024
VMEM_LIMIT = V7X_VMEM_BYTES * 7 // 8
LANES = 128
PROJ_TM = 512
PROJ_CHUNK = 512
MERGE_TM = 1024
ADALN_TN = 1024
KEY_TILE = Q_BLK
NA_BATCH = 2


def _sigmoid(v):
    return 1.0 / (1.0 + jnp.exp(-v))


def _dot(a, b):
    return jnp.dot(a, b, preferred_element_type=F32)


def _dot_nt(a, b):
    return lax.dot_general(a, b, (((1,), (1,)), ((), ())), preferred_element_type=F32)


def _adaln_kernel(cond_ref, w_ref, b_ref, o_ref):
    cnd = cond_ref[...]
    act = cnd * _sigmoid(cnd)
    o_ref[...] = jnp.dot(act, w_ref[...], preferred_element_type=F32,
                         precision=lax.Precision.HIGHEST) + b_ref[...]


def _adaln(cond, w_mod, b_mod):
    rows, d = cond.shape
    n = w_mod.shape[1]
    tn = ADALN_TN
    return pl.pallas_call(
        _adaln_kernel,
        grid=(n // tn,),
        in_specs=[pl.BlockSpec((rows, d), lambda j: (0, 0)),
                  pl.BlockSpec((d, tn), lambda j: (0, j)),
                  pl.BlockSpec((1, tn), lambda j: (0, j))],
        out_specs=pl.BlockSpec((rows, tn), lambda j: (0, j)),
        out_shape=jax.ShapeDtypeStruct((rows, n), F32),
        name="adaln",
    )(cond, w_mod, b_mod.reshape(1, n))


W_TILE = 1024
DT_PAD = 128


def _wprep_kernel(a_ref, b_ref, dt_ref, w_ref, wdt_ref, *, n_aligned, shift, q_tiles, q_scale):
    j = pl.program_id(0)

    @pl.when(j < n_aligned)
    def _():
        scale = jnp.where(j < q_tiles, q_scale, 1.0)
        w_ref[...] = (a_ref[...] * scale).T.astype(BF16)

    @pl.when(j >= n_aligned)
    def _():
        w_ref[...] = jnp.concatenate([a_ref[shift:, :], b_ref[...]], axis=0).T.astype(BF16)

    @pl.when(j == 0)
    def _():
        dst = lax.broadcasted_iota(jnp.int32, (DT_PAD, DT_PAD), 1)
        src = lax.broadcasted_iota(jnp.int32, (DT_PAD, DT_PAD), 0)
        per_group = N_DIRS * HEADS_PER_GROUP
        grp, rem = dst // per_group, dst % per_group
        want = (rem // HEADS_PER_GROUP) * SSM_HEADS + grp * HEADS_PER_GROUP + rem % HEADS_PER_GROUP
        perm = jnp.where((src == want) & (dst < N_DIRS * SSM_HEADS), 1.0, 0.0).astype(BF16)
        wdt_ref[...] = _dot(dt_ref[...].T.astype(BF16), perm).astype(BF16)


def _prep_w_in(w_in_t, col_dt, n_dt, q_cols, q_scale):
    p, d = w_in_t.shape
    n_out = p - n_dt
    assert col_dt % W_TILE == 0 and n_out % W_TILE == 0 and q_cols % W_TILE == 0
    assert W_TILE % n_dt == 0 and n_dt % 8 == 0 and n_dt <= DT_PAD and col_dt % DT_PAD == 0
    return pl.pallas_call(
        functools.partial(_wprep_kernel, n_aligned=col_dt // W_TILE, shift=n_dt, q_tiles=q_cols // W_TILE,
                          q_scale=q_scale),
        grid=(n_out // W_TILE,),
        in_specs=[pl.BlockSpec((W_TILE, d), lambda j: (j, 0)),
                  pl.BlockSpec((n_dt, d), lambda j: ((j + 1) * (W_TILE // n_dt), 0)),
                  pl.BlockSpec((DT_PAD, d), lambda j: (col_dt // DT_PAD, 0))],
        out_specs=[pl.BlockSpec((d, W_TILE), lambda j: (0, j)),
                   pl.BlockSpec((d, DT_PAD), lambda j: (0, 0))],
        out_shape=[jax.ShapeDtypeStruct((d, n_out), BF16), jax.ShapeDtypeStruct((d, DT_PAD), BF16)],
        name="w_in_prep",
    )(w_in_t, w_in_t, w_in_t)


def _proj_kernel(x_ref, shift_ref, scale_ref, g_ref, w_ref, wdt_ref, p_ref, dtt_ref, *, chunks):
    x = x_ref[...]
    ms = jnp.mean(x * x, axis=-1, keepdims=True)
    h = x * lax.rsqrt(ms + EPS) * g_ref[...]
    h = h * (1.0 + scale_ref[...]) + shift_ref[...]
    hb = h.astype(BF16)
    for src, dst, width in chunks:
        p_ref[:, dst:dst + width] = _dot(hb, w_ref[:, src:src + width]).astype(BF16)
    dt = _dot(hb, wdt_ref[...])
    dtt_ref[...] = dt.T[:N_DIRS * SSM_HEADS]


def _projection(x3, shift, scale, g_pre, w_main, w_dt, col_ranges, tm):
    b, l, d = x3.shape
    tm = min(tm, l)
    chunks, dst = [], 0
    cw = PROJ_CHUNK
    for lo, hi in col_ranges:
        for src in range(lo, hi, cw):
            chunks.append((src, dst, cw))
            dst += cw
    n_out = dst
    n_dt = N_DIRS * SSM_HEADS
    return pl.pallas_call(
        functools.partial(_proj_kernel, chunks=tuple(chunks)),
        grid=(b, l // tm),
        in_specs=[pl.BlockSpec((None, tm, d), lambda i, j: (i, j, 0)),
                  pl.BlockSpec((None, 1, d), lambda i, j: (i, 0, 0)),
                  pl.BlockSpec((None, 1, d), lambda i, j: (i, 0, 0)),
                  pl.BlockSpec((1, d), lambda i, j: (0, 0)),
                  pl.BlockSpec(w_main.shape, lambda i, j: (0, 0), pipeline_mode=pl.Buffered(1)),
                  pl.BlockSpec(w_dt.shape, lambda i, j: (0, 0), pipeline_mode=pl.Buffered(1))],
        out_specs=[pl.BlockSpec((None, tm, n_out), lambda i, j: (i, j, 0)),
                   pl.BlockSpec((None, n_dt, tm), lambda i, j: (i, 0, j))],
        out_shape=[jax.ShapeDtypeStruct((b, l, n_out), BF16),
                   jax.ShapeDtypeStruct((b, n_dt, l), F32)],
        compiler_params=pltpu.CompilerParams(
            dimension_semantics=("parallel", "parallel"), vmem_limit_bytes=VMEM_LIMIT),
        name="projection",
    )(x3, shift, scale, g_pre, w_main, w_dt)


def _bias_block_plan():
    rows = 2048 // GRID_W
    plan = np.full((N_BIAS_CASES, Q_ROWS, KEY_ROWS), -1, np.int64)
    for case, blk in enumerate(BIAS_CASE_BLOCK):
        u0 = int(np.clip(Q_ROWS * blk - WIN_ROWS // 2, 0, rows - KEY_ROWS))
        for rho in range(Q_ROWS):
            r = Q_ROWS * blk + rho
            r0 = int(np.clip(r - WIN_ROWS // 2, 0, rows - WIN_ROWS))
            for a in range(KEY_ROWS):
                krow = u0 + a
                if r0 <= krow < r0 + WIN_ROWS:
                    plan[case, rho, a] = krow - r + WIN_ROWS - 1
    return plan


def _bias_kernel(rpb_ref, o_ref, toep_ref, *, plan):
    n_dr = 2 * WIN_ROWS - 1
    qc = lax.broadcasted_iota(jnp.int32, (GRID_W, GRID_W), 0)
    kc = lax.broadcasted_iota(jnp.int32, (GRID_W, GRID_W), 1)
    c0 = jnp.clip(qc - WIN_COLS // 2, 0, GRID_W - WIN_COLS)
    in_win = (kc >= c0) & (kc < c0 + WIN_COLS)
    for dr in range(n_dr):
        rows = jnp.broadcast_to(rpb_ref[dr:dr + 1, :], (GRID_W, BIAS_LANES))
        toep = pltpu.roll(rows, BIAS_LANES - (WIN_COLS - 1), 1, stride=1, stride_axis=0)[:, :GRID_W]
        toep_ref[dr] = jnp.where(in_win, toep * LOG2E, NEG)
    neg = jnp.full((GRID_W, GRID_W), NEG, F32)
    for case in range(N_BIAS_CASES):
        for rho in range(Q_ROWS):
            for a in range(KEY_ROWS):
                dr = int(plan[case, rho, a])
                val = neg if dr < 0 else toep_ref[dr]
                o_ref[case, rho * GRID_W:(rho + 1) * GRID_W, a * GRID_W:(a + 1) * GRID_W] = val.astype(BF16)


def _bias_table(rpb):
    plan = _bias_block_plan()
    heads, n_dr, n_dc = rpb.shape
    rpb_lanes = jnp.pad(rpb, ((0, 0), (0, 0), (0, BIAS_LANES - n_dc)))
    return pl.pallas_call(
        functools.partial(_bias_kernel, plan=plan),
        grid=(heads,),
        in_specs=[pl.BlockSpec((None, n_dr, BIAS_LANES), lambda h: (h, 0, 0))],
        out_specs=pl.BlockSpec((N_BIAS_CASES, None, Q_BLK, KEY_BLK), lambda h: (0, h, 0, 0)),
        out_shape=jax.ShapeDtypeStruct((N_BIAS_CASES, heads, Q_BLK, KEY_BLK), BF16),
        scratch_shapes=[pltpu.VMEM((n_dr, GRID_W, GRID_W), F32)],
        name="bias_table",
    )(rpb_lanes)


def _na_kernel(q_ref, k0_ref, k1_ref, k2_ref, kc_ref, v0_ref, v1_ref, v2_ref, vc_ref, z_ref, bias_ref, o_ref):
    lane = lax.broadcasted_iota(jnp.int32, (1, 2 * NA_HEAD_DIM), 1)
    k_refs = (k0_ref, k1_ref, k2_ref, kc_ref)
    v_refs = (v0_ref, v1_ref, v2_ref, vc_ref)
    ones_tile = jnp.where(lax.broadcasted_iota(jnp.int32, (Q_BLK, 2 * NA_HEAD_DIM), 1) == 0, 1.0, 0.0).astype(BF16)
    for bb, pair in ((bb, pair) for bb in range(q_ref.shape[0]) for pair in range(NA_HEADS // 2)):
        cs = slice(pair * LANES, (pair + 1) * LANES)
        qp = q_ref[bb, :, cs]
        q_stack = jnp.concatenate([jnp.where(lane < NA_HEAD_DIM, qp, jnp.zeros_like(qp)),
                                   jnp.where(lane >= NA_HEAD_DIM, qp, jnp.zeros_like(qp))], axis=0)
        s_both = [_dot_nt(q_stack, r[bb, :, cs]) for r in k_refs]
        e_both = [[], []]
        for hh in range(2):
            s = [sb[hh * Q_BLK:(hh + 1) * Q_BLK] for sb in s_both]
            for t in range(KEY_ROWS // Q_ROWS):
                s[t] = s[t] + bias_ref[2 * pair + hh, :, t * KEY_TILE:(t + 1) * KEY_TILE].astype(F32)
            mx = jnp.max(jnp.maximum(jnp.maximum(s[0], s[1]), jnp.maximum(s[2], s[3])), axis=-1, keepdims=True)
            e_both[hh] = [jnp.exp2((st - mx).astype(BF16)) for st in s]
        o = None
        for t, r in enumerate(v_refs):
            p_stack = jnp.concatenate([e_both[0][t], e_both[1][t]], axis=0)
            c = _dot(p_stack, jnp.concatenate([r[bb, :, cs], ones_tile], axis=1))
            o = c if o is None else o + c
        outs = [o[hh * Q_BLK:(hh + 1) * Q_BLK, :LANES] * (1.0 / o[hh * Q_BLK:(hh + 1) * Q_BLK, LANES:LANES + 1])
                for hh in range(2)]
        acc = jnp.where(lane < NA_HEAD_DIM, outs[0], outs[1])
        z = z_ref[bb, :, cs].astype(F32)
        o_ref[bb, :, cs] = (acc * (z * _sigmoid(z))).astype(BF16)


def _neighborhood_attention(p3, pc3, bias_tab):
    b, s, _ = p3.shape
    assert b % NA_BATCH == 0
    n_blk = s // Q_BLK

    def key_blk(i):
        return jnp.clip(i - 1, 0, n_blk - KEY_ROWS // Q_ROWS)

    def case_of(i):
        return jnp.minimum(i, 2) + jnp.maximum(i - (n_blk - 3), 0)

    cq, ck, cv, cz = COL_Q // NA_WIDTH, COL_K // NA_WIDTH, COL_V // NA_WIDTH, COL_ZNA // NA_WIDTH
    blk = (NA_BATCH, Q_BLK, NA_WIDTH)
    in_specs = [pl.BlockSpec(blk, lambda i, j: (j, i, cq))]
    in_specs += [pl.BlockSpec(blk, functools.partial(lambda i, j, t: (j, key_blk(i) + t, ck), t=t)) for t in range(3)]
    in_specs += [pl.BlockSpec(blk, lambda i, j: (j, 0, CCOL_K // NA_WIDTH))]
    in_specs += [pl.BlockSpec(blk, functools.partial(lambda i, j, t: (j, key_blk(i) + t, cv), t=t)) for t in range(3)]
    in_specs += [pl.BlockSpec(blk, lambda i, j: (j, 0, CCOL_V // NA_WIDTH))]
    in_specs += [pl.BlockSpec(blk, lambda i, j: (j, i, cz))]
    in_specs += [pl.BlockSpec((None, NA_HEADS, Q_BLK, KEY_BLK), lambda i, j: (case_of(i), 0, 0, 0))]
    return pl.pallas_call(
        _na_kernel,
        grid=(n_blk, b // NA_BATCH),
        in_specs=in_specs,
        out_specs=pl.BlockSpec(blk, lambda i, j: (j, i, 0)),
        out_shape=jax.ShapeDtypeStruct((b, s, NA_WIDTH), BF16),
        compiler_params=pltpu.CompilerParams(
            dimension_semantics=("parallel", "parallel"), vmem_limit_bytes=VMEM_LIMIT),
        name="neighborhood_attention",
    )(p3, p3, p3, p3, pc3, p3, p3, p3, pc3, p3, bias_tab)


def _softplus(v):
    return jnp.maximum(v, 0.0) + jnp.log(1.0 + jnp.exp(-jnp.abs(v)))


def _head_stack(v, head_lanes):
    return jnp.concatenate([jnp.where(m, v, jnp.zeros_like(v)) for m in head_lanes], axis=0)


def _ssd_kernel(xs_ref, bm_ref, cm_ref, z_ref, xsc_ref, bmc_ref, dtt_ref, dttc_ref,
                cwx_ref, cwb_ref, cwc_ref, cbx_ref, cbb_ref, cbc_ref,
                alog_ref, dtb_ref, skip_ref, ng_ref, o_ref,
                pad_ref, padc_ref, tap_ref, ux_ref, ubt_ref, uc_ref, uxc_ref, ubtc_ref,
                rows_ref, cols_ref, dec_ref, rowsc_ref, decc_ref, y_ref, st_ref, e_ref):
    q = SSM_CHUNK
    hpg = HEADS_PER_GROUP
    nh = N_DIRS * hpg
    s_len = xs_ref.shape[0]
    c_len = xsc_ref.shape[0]
    n_lat = s_len // q
    n_ctx = c_len // q
    halo = CONV_HALO
    blk = CONV_BLK
    win = blk + 2 * halo

    zeros_halo = jnp.zeros((halo, pad_ref.shape[1]), BF16)
    pad_ref[0:halo, :] = zeros_halo
    pad_ref[s_len + halo:s_len + 2 * halo, :] = zeros_halo
    pad_ref[halo:s_len + halo, 0:GROUP_X] = xs_ref[...]
    pad_ref[halo:s_len + halo, GROUP_X:GROUP_X + SSM_STATE] = bm_ref[...]
    pad_ref[halo:s_len + halo, GROUP_X + SSM_STATE:] = cm_ref[...]
    zeros_halo_c = jnp.zeros((halo, padc_ref.shape[1]), BF16)
    padc_ref[0:halo, :] = zeros_halo_c
    padc_ref[c_len + halo:c_len + 2 * halo, :] = zeros_halo_c
    padc_ref[halo:c_len + halo, 0:GROUP_X] = xsc_ref[...]
    padc_ref[halo:c_len + halo, GROUP_X:] = bmc_ref[...]

    cw = jnp.concatenate([cwx_ref[...], cwb_ref[...], cwc_ref[...]], axis=1)
    cb = jnp.concatenate([cbx_ref[...], cbb_ref[...], cbc_ref[...]], axis=1)
    for k in range(SSM_CONV):
        tap_ref[k] = jnp.broadcast_to(cw[k:k + 1, :], (win, cw.shape[1])).astype(BF16)
    off = lax.broadcasted_iota(jnp.int32, (blk, SSM_CONV * win), 1) - lax.broadcasted_iota(
        jnp.int32, (blk, SSM_CONV * win), 0)
    hit = off == halo - SSM_CONV // 2
    for k in range(1, SSM_CONV):
        hit = hit | (off == k * win + halo - SSM_CONV // 2 + k)
    shift = jnp.where(hit, 1.0, 0.0).astype(BF16)

    def conv_block(src_ref, start, width):
        w = src_ref[pl.ds(start, win), :]
        stack = jnp.concatenate([w * tap_ref[k, :, 0:width] for k in range(SSM_CONV)], axis=0)
        acc = _dot(shift, stack) + cb[:, :width]
        return acc * _sigmoid(acc)

    def conv_chunk(src_ref, start, width):
        return jnp.concatenate([conv_block(src_ref, start + i * blk, width) for i in range(q // blk)], axis=0)

    def conv_lat(c, carry):
        start = pl.multiple_of(c * q, q)
        u = conv_chunk(pad_ref, start, GROUP_X + 2 * SSM_STATE)
        ux_ref[pl.ds(start, q), :] = u[:, :GROUP_X]
        ubt_ref[c] = u[:, GROUP_X:GROUP_X + SSM_STATE].T.astype(BF16)
        uc_ref[pl.ds(start, q), :] = u[:, GROUP_X + SSM_STATE:].astype(BF16)
        return carry

    lax.fori_loop(0, n_lat, conv_lat, 0, unroll=True)
    for c in range(n_ctx):
        u = conv_chunk(padc_ref, c * q, GROUP_X + SSM_STATE)
        uxc_ref[c * q:(c + 1) * q, :] = u[:, :GROUP_X]
        ubtc_ref[c] = u[:, GROUP_X:].T.astype(BF16)

    ki = lax.broadcasted_iota(jnp.int32, (q, q), 0)
    ii = lax.broadcasted_iota(jnp.int32, (q, q), 1)
    tri_f = (ki <= ii).astype(F32)
    tri_b = (ki >= ii).astype(F32)
    lane_x = lax.broadcasted_iota(jnp.int32, (1, GROUP_X), 1)
    head_lanes = [(lane_x >= r * SSM_HEAD_DIM) & (lane_x < (r + 1) * SSM_HEAD_DIM) for r in range(hpg)]
    neg_a2 = -jnp.exp(alog_ref[...]) * LOG2E
    dt_bias = dtb_ref[...]

    def dt_forms(raw, n, rows_out, cols_out, dec_out):
        stack = lambda v: jnp.concatenate([v[:, c * q:(c + 1) * q] for c in range(n)], axis=0)
        dt_all = _softplus(raw + dt_bias)
        dt = stack(dt_all)
        a = stack(neg_a2 * dt_all)
        is_fwd = lax.broadcasted_iota(jnp.int32, (n * nh, 1), 0) % nh < hpg
        cum_f = jnp.dot(a, tri_f, preferred_element_type=F32, precision=lax.Precision.HIGHEST)
        cum_b = jnp.dot(a, tri_b, preferred_element_type=F32, precision=lax.Precision.HIGHEST)
        acum = jnp.where(is_fwd, cum_f, cum_b)
        atot = jnp.where(is_fwd, acum[:, q - 1:q], acum[:, 0:1])
        sw = dt * jnp.exp2(atot - acum)
        dec = jnp.exp2(atot)
        for c in range(n):
            sl = slice(c * nh, (c + 1) * nh)
            rows_out[c] = jnp.concatenate([dt[sl], acum[sl], sw[sl]], axis=0)
            if cols_out is not None:
                cols_out[c] = acum[sl].T
            dec_rows = []
            for d in range(N_DIRS):
                base = c * nh + d * hpg
                row = jnp.broadcast_to(dec[base + hpg - 1:base + hpg, :], (1, GROUP_X))
                for r in range(hpg - 2, -1, -1):
                    row = jnp.where(lane_x < (r + 1) * SSM_HEAD_DIM, dec[base + r:base + r + 1, :], row)
                dec_rows.append(row)
            dec_out[c] = jnp.concatenate(dec_rows, axis=0)

    dt_forms(dtt_ref[...], n_lat, rows_ref, cols_ref, dec_ref)
    dt_forms(dttc_ref[...], n_ctx, rowsc_ref, None, decc_ref)

    def state_term(d, rows_b, bt, xm):
        parts = [bt * rows_b[2 * N_DIRS * hpg + d * hpg + r:2 * N_DIRS * hpg + d * hpg + r + 1, :]
                 for r in range(hpg)]
        return _dot(jnp.concatenate(parts, axis=1), xm)

    h_init = []
    for d in range(N_DIRS):
        h = jnp.zeros((SSM_STATE, GROUP_X), F32)
        for c in (range(n_ctx) if d == 0 else reversed(range(n_ctx))):
            xm = _head_stack(uxc_ref[c * q:(c + 1) * q, :].astype(BF16), head_lanes)
            h = decc_ref[c][d:d + 1, :] * h + state_term(d, rowsc_ref[c].astype(BF16), ubtc_ref[c], xm)
        h_init.append(h)

    skip = skip_ref[...]
    norm_g = ng_ref[...]
    keep = (ki >= ii, ki <= ii)
    low_lanes = lax.broadcasted_iota(jnp.int32, (1, q), 1) < SSM_HEAD_DIM

    def chunk_body(c, carry):
        start = pl.multiple_of(c * q, q)
        rows = rows_ref[c]
        cols = cols_ref[c]
        xs = ux_ref[pl.ds(start, q), :]
        xm = _head_stack(xs.astype(BF16), head_lanes)
        cmat = uc_ref[pl.ds(start, q), :]
        bt = ubt_ref[c]
        cbm = _dot(cmat, bt)
        cbm_b = cbm.astype(BF16)
        rows_b = rows.astype(BF16)
        zero_b = jnp.zeros((q, q), BF16)
        m_parts = []
        acols = [[], []]
        for r in range(hpg):
            t = None
            for d in range(N_DIRS):
                k = d * hpg + r
                acol = jnp.broadcast_to(cols[:, k:k + 1], (q, q))
                arow = rows[N_DIRS * hpg + k:N_DIRS * hpg + k + 1, :]
                term = jnp.where(keep[d], jnp.exp2((acol - arow).astype(BF16)) * rows_b[k:k + 1, :], zero_b)
                t = term if t is None else t + term
                acols[d].append(acol)
            m_parts.append(cbm_b * t)
        y_ref[pl.ds(start, q), :] = xs * skip + _dot(jnp.concatenate(m_parts, axis=1), xm)
        for d in range(N_DIRS):
            st_ref[d, c] = state_term(d, rows_b, bt, xm)
            halves = [jnp.where(low_lanes, acols[d][2 * i], acols[d][2 * i + 1]) for i in range(hpg // 2)]
            e_ref[d, c] = jnp.exp2(jnp.concatenate(halves, axis=1))
        return carry

    lax.fori_loop(0, n_lat, chunk_body, 0, unroll=SSD_UNROLL)

    def scan_body(j, carry, finalize):
        new = []
        chunk_ids = (j, n_lat - 1 - j)
        for d in range(N_DIRS):
            h = carry[d]
            c = chunk_ids[d]
            start = pl.multiple_of(c * q, q)
            y_off = _dot(uc_ref[pl.ds(start, q), :], h.astype(BF16)) * e_ref[d, c]
            y_ref[pl.ds(start, q), :] = y_ref[pl.ds(start, q), :] + y_off
            new.append(dec_ref[c][d:d + 1, :] * h + st_ref[d, c])
        if finalize:
            for c in chunk_ids:
                start = pl.multiple_of(c * q, q)
                z = z_ref[pl.ds(start, q), :].astype(F32)
                u = y_ref[pl.ds(start, q), :] * (z * _sigmoid(z))
                u = u * lax.rsqrt(jnp.mean(u * u, axis=-1, keepdims=True) + EPS)
                o_ref[pl.ds(start, q), :] = (u * norm_g).astype(BF16)
        return tuple(new)

    half = n_lat // 2
    carry = lax.fori_loop(0, half, functools.partial(scan_body, finalize=False), tuple(h_init), unroll=SSD_UNROLL)
    lax.fori_loop(half, n_lat, functools.partial(scan_body, finalize=True), carry, unroll=SSD_UNROLL)


def _ssd_mixer(p3, pc3, dtt, dttc, conv_w, conv_b, alog8, dtb8, skip_exp, norm_g):
    b, s, _ = p3.shape
    c_len = pc3.shape[1]
    g = SSM_GROUPS
    n_lat, n_ctx = s // SSM_CHUNK, c_len // SSM_CHUNK
    assert n_lat % 2 == 0
    nx = GROUP_X
    ns = SSM_STATE
    nh = N_DIRS * HEADS_PER_GROUP
    in_specs = [
        pl.BlockSpec((None, s, nx), lambda i, j: (i, 0, COL_X // nx + j)),
        pl.BlockSpec((None, s, ns), lambda i, j: (i, 0, COL_B // ns + j)),
        pl.BlockSpec((None, s, ns), lambda i, j: (i, 0, COL_C // ns + j)),
        pl.BlockSpec((None, s, nx), lambda i, j: (i, 0, COL_ZSSM // nx + j)),
        pl.BlockSpec((None, c_len, nx), lambda i, j: (i, 0, CCOL_X // nx + j)),
        pl.BlockSpec((None, c_len, ns), lambda i, j: (i, 0, CCOL_B // ns + j)),
        pl.BlockSpec((None, nh, s), lambda i, j: (i, j, 0)),
        pl.BlockSpec((None, nh, c_len), lambda i, j: (i, j, 0)),
        pl.BlockSpec((SSM_CONV, nx), lambda i, j: (0, j)),
        pl.BlockSpec((SSM_CONV, ns), lambda i, j: (0, SSM_WIDTH // ns + j)),
        pl.BlockSpec((SSM_CONV, ns), lambda i, j: (0, SSM_WIDTH // ns + g + j)),
        pl.BlockSpec((1, nx), lambda i, j: (0, j)),
        pl.BlockSpec((1, ns), lambda i, j: (0, SSM_WIDTH // ns + j)),
        pl.BlockSpec((1, ns), lambda i, j: (0, SSM_WIDTH // ns + g + j)),
        pl.BlockSpec((None, nh, 1), lambda i, j: (j, 0, 0)),
        pl.BlockSpec((None, nh, 1), lambda i, j: (j, 0, 0)),
        pl.BlockSpec((None, 1, nx), lambda i, j: (j, 0, 0)),
        pl.BlockSpec((1, nx), lambda i, j: (0, j)),
    ]
    halo = CONV_HALO
    scratch = [
        pltpu.VMEM((s + 2 * halo, nx + 2 * ns), BF16),
        pltpu.VMEM((c_len + 2 * halo, nx + ns), BF16),
        pltpu.VMEM((SSM_CONV, CONV_BLK + 2 * halo, nx + 2 * ns), BF16),
        pltpu.VMEM((s, nx), F32),
        pltpu.VMEM((n_lat, ns, SSM_CHUNK), BF16),
        pltpu.VMEM((s, ns), BF16),
        pltpu.VMEM((c_len, nx), F32),
        pltpu.VMEM((n_ctx, ns, SSM_CHUNK), BF16),
        pltpu.VMEM((n_lat, 3 * nh, SSM_CHUNK), F32),
        pltpu.VMEM((n_lat, SSM_CHUNK, nh), F32),
        pltpu.VMEM((n_lat, N_DIRS, nx), F32),
        pltpu.VMEM((n_ctx, 3 * nh, SSM_CHUNK), F32),
        pltpu.VMEM((n_ctx, N_DIRS, nx), F32),
        pltpu.VMEM((s, nx), F32),
        pltpu.VMEM((N_DIRS, n_lat, ns, nx), F32),
        pltpu.VMEM((N_DIRS, n_lat, SSM_CHUNK, nx), F32),
    ]
    return pl.pallas_call(
        _ssd_kernel,
        grid=(b, g),
        in_specs=in_specs,
        out_specs=pl.BlockSpec((None, s, nx), lambda i, j: (i, 0, j)),
        out_shape=jax.ShapeDtypeStruct((b, s, SSM_WIDTH), BF16),
        scratch_shapes=scratch,
        compiler_params=pltpu.CompilerParams(
            dimension_semantics=("parallel", "parallel"), vmem_limit_bytes=VMEM_LIMIT),
        name="ssd_mixer",
    )(p3, p3, p3, p3, pc3, pc3, dtt, dttc, conv_w, conv_w, conv_w, conv_b, conv_b, conv_b,
      alog8, dtb8, skip_exp, norm_g)


def _merge_kernel(yna_ref, yssm_ref, gna_ref, gssm_ref, x_ref, gate_ref, wna_ref, wssm_ref, wout_ref, gpost_ref, o_ref):
    a = _dot(yna_ref[...], wna_ref[...])
    s = _dot(yssm_ref[...], wssm_ref[...])
    m = _sigmoid(gna_ref[...].astype(F32)) * a + _sigmoid(gssm_ref[...].astype(F32)) * s
    o = _dot(m.astype(BF16), wout_ref[...])
    r = o * lax.rsqrt(jnp.mean(o * o, axis=-1, keepdims=True) + EPS) * gpost_ref[...]
    o_ref[...] = x_ref[...] + gate_ref[...] * r


def _merge(y_na, y_ssm, p3, x3, gate, w_na, w_ssm, w_out, g_post, tm):
    b, s, d = x3.shape
    const = lambda i, j: (0, 0)
    return pl.pallas_call(
        _merge_kernel,
        grid=(b, s // tm),
        in_specs=[pl.BlockSpec((None, tm, y_na.shape[-1]), lambda i, j: (i, j, 0)),
                  pl.BlockSpec((None, tm, y_ssm.shape[-1]), lambda i, j: (i, j, 0)),
                  pl.BlockSpec((None, tm, d), lambda i, j: (i, j, COL_GNA // d)),
                  pl.BlockSpec((None, tm, d), lambda i, j: (i, j, COL_GSSM // d)),
                  pl.BlockSpec((None, tm, d), lambda i, j: (i, j, 0)),
                  pl.BlockSpec((None, 1, d), lambda i, j: (i, 0, 0)),
                  pl.BlockSpec(w_na.shape, const, pipeline_mode=pl.Buffered(1)),
                  pl.BlockSpec(w_ssm.shape, const, pipeline_mode=pl.Buffered(1)),
                  pl.BlockSpec(w_out.shape, const, pipeline_mode=pl.Buffered(1)),
                  pl.BlockSpec((1, d), const)],
        out_specs=pl.BlockSpec((None, tm, d), lambda i, j: (i, j, 0)),
        out_shape=jax.ShapeDtypeStruct((b, s, d), x3.dtype),
        compiler_params=pltpu.CompilerParams(
            dimension_semantics=("parallel", "parallel"), vmem_limit_bytes=VMEM_LIMIT),
        name="merge_out",
    )(y_na, y_ssm, p3, p3, x3, gate, w_na, w_ssm, w_out, g_post)


def kernel(x, c, ctx, c_ctx, w_mod, b_mod, g_pre, g_post, w_in, conv_w, conv_b, a_log, dt_bias, d_skip,
           ssm_norm_g, rpb, w_na_out, w_ssm_out, w_out):
    assert w_mod.shape[0] == 1, "single-layer block"
    b, s, d = x.shape
    g, hpg = SSM_GROUPS, HEADS_PER_GROUP

    w_main, w_dt = _prep_w_in(jnp.transpose(w_in[0]), COL_GNA, N_DIRS * SSM_HEADS, COL_K, NA_HEAD_DIM ** -0.5 * LOG2E)
    per_group = lambda p: jnp.transpose(p.reshape(N_DIRS, g, hpg), (1, 0, 2)).reshape(g, N_DIRS * hpg, 1)
    alog8 = per_group(a_log[0])
    dtb8 = per_group(dt_bias[0])
    skip_exp = jnp.repeat(d_skip[0], SSM_HEAD_DIM).reshape(g, 1, GROUP_X)

    cond = jnp.zeros((16, d), F32).at[:b].set(c).at[b].set(c_ctx)
    mod = _adaln(cond, w_mod[0], b_mod[0])
    shift_l, scale_l, gate_l = (mod[:b, k * d:(k + 1) * d].reshape(b, 1, d) for k in range(3))
    shift_c, scale_c = (jnp.broadcast_to(mod[b, k * d:(k + 1) * d].reshape(1, 1, d), (b, 1, d)) for k in range(2))

    p3, dtt = _projection(x, shift_l, scale_l, g_pre, w_main, w_dt, ((0, NP_COLS),), tm=PROJ_TM)
    pc3, dttc = _projection(ctx, shift_c, scale_c, g_pre, w_main, w_dt,
                            ((COL_K, COL_ZNA), (COL_X, COL_GNA)), tm=PROJ_TM)

    bias_tab = _bias_table(rpb[0])
    y_na = _neighborhood_attention(p3, pc3, bias_tab)

    y_ssm = _ssd_mixer(p3, pc3, dtt, dttc, conv_w[0], conv_b[0].reshape(1, -1), alog8, dtb8, skip_exp,
                       ssm_norm_g)

    return _merge(y_na, y_ssm, p3, x, gate_l, w_na_out[0].astype(BF16), w_ssm_out[0].astype(BF16),
                  w_out[0].astype(BF16), g_post, tm=MERGE_TM)
```

```python
import functools

import numpy as np
import jax
import jax.numpy as jnp
from jax import lax
from jax.experimental import pallas as pl
from jax.experimental.pallas import tpu as pltpu

F32 = jnp.float32
BF16 = jnp.bfloat16

D_MODEL = 1024
GRID_W = 64
NA_HEADS = 16
NA_HEAD_DIM = 64
NA_WIDTH = NA_HEADS * NA_HEAD_DIM
WIN_ROWS = 8
WIN_COLS = 16
SSM_WIDTH = 2 * D_MODEL
SSM_HEAD_DIM = 64
SSM_HEADS = SSM_WIDTH // SSM_HEAD_DIM
SSM_GROUPS = 8
HEADS_PER_GROUP = SSM_HEADS // SSM_GROUPS
SSM_STATE = 128
SSM_CONV = 5
SSM_CHUNK = 128
CONV_HALO = 16
CONV_BLK = 64
SSD_UNROLL = 8
N_DIRS = 2
GROUP_X = HEADS_PER_GROUP * SSM_HEAD_DIM
EPS = 1e-6
NEG = -1e30
LOG2E = 1.4426950408889634

COL_Q, COL_K, COL_V, COL_ZNA = 0, 1024, 2048, 3072
COL_ZSSM, COL_X, COL_B, COL_C = 4096, 6144, 8192, 9216
COL_GNA, COL_GSSM, NP_COLS = 10240, 11264, 12288
CCOL_K, CCOL_V, CCOL_X, CCOL_B, NPC_COLS = 0, 1024, 2048, 4096, 6144

Q_ROWS = 4
Q_BLK = Q_ROWS * GRID_W
KEY_ROWS = 12
KEY_BLK = KEY_ROWS * GRID_W
N_BIAS_CASES = 5
BIAS_CASE_BLOCK = (0, 1, 2, 6, 7)
BIAS_LANES = 128

V7X_VMEM_BYTES = 64 * 1024 * 1024
VMEM_LIMIT = V7X_VMEM_BYTES * 7 // 8
LANES = 128
PROJ_TM = 512
PROJ_CHUNK = 512
MERGE_TM = 512
ADALN_TN = 1024
KEY_TILE = Q_BLK
NA_BATCH = 2


def _sigmoid(v):
    return 1.0 / (1.0 + jnp.exp(-v))


def _dot(a, b):
    return jnp.dot(a, b, preferred_element_type=F32)


def _dot_nt(a, b):
    return lax.dot_general(a, b, (((1,), (1,)), ((), ())), preferred_element_type=F32)


def _adaln_kernel(cond_ref, w_ref, b_ref, o_ref):
    cnd = cond_ref[...]
    act = cnd * _sigmoid(cnd)
    o_ref[...] = jnp.dot(act, w_ref[...], preferred_element_type=F32,
                         precision=lax.Precision.HIGHEST) + b_ref[...]


def _adaln(cond, w_mod, b_mod):
    rows, d = cond.shape
    n = w_mod.shape[1]
    tn = ADALN_TN
    return pl.pallas_call(
        _adaln_kernel,
        grid=(n // tn,),
        in_specs=[pl.BlockSpec((rows, d), lambda j: (0, 0)),
                  pl.BlockSpec((d, tn), lambda j: (0, j)),
                  pl.BlockSpec((1, tn), lambda j: (0, j))],
        out_specs=pl.BlockSpec((rows, tn), lambda j: (0, j)),
        out_shape=jax.ShapeDtypeStruct((rows, n), F32),
        name="adaln",
    )(cond, w_mod, b_mod.reshape(1, n))


W_TILE = 1024
DT_PAD = 128


def _wprep_kernel(a_ref, b_ref, dt_ref, w_ref, wdt_ref, *, n_aligned, shift, q_tiles, q_scale):
    j = pl.program_id(0)

    @pl.when(j < n_aligned)
    def _():
        scale = jnp.where(j < q_tiles, q_scale, 1.0)
        w_ref[...] = (a_ref[...] * scale).T.astype(BF16)

    @pl.when(j >= n_aligned)
    def _():
        w_ref[...] = jnp.concatenate([a_ref[shift:, :], b_ref[...]], axis=0).T.astype(BF16)

    @pl.when(j == 0)
    def _():
        dst = lax.broadcasted_iota(jnp.int32, (DT_PAD, DT_PAD), 1)
        src = lax.broadcasted_iota(jnp.int32, (DT_PAD, DT_PAD), 0)
        per_group = N_DIRS * HEADS_PER_GROUP
        grp, rem = dst // per_group, dst % per_group
        want = (rem // HEADS_PER_GROUP) * SSM_HEADS + grp * HEADS_PER_GROUP + rem % HEADS_PER_GROUP
        perm = jnp.where((src == want) & (dst < N_DIRS * SSM_HEADS), 1.0, 0.0).astype(BF16)
        wdt_ref[...] = _dot(dt_ref[...].T.astype(BF16), perm).astype(BF16)


def _prep_w_in(w_in_t, col_dt, n_dt, q_cols, q_scale):
    p, d = w_in_t.shape
    n_out = p - n_dt
    assert col_dt % W_TILE == 0 and n_out % W_TILE == 0 and q_cols % W_TILE == 0
    assert W_TILE % n_dt == 0 and n_dt % 8 == 0 and n_dt <= DT_PAD and col_dt % DT_PAD == 0
    return pl.pallas_call(
        functools.partial(_wprep_kernel, n_aligned=col_dt // W_TILE, shift=n_dt, q_tiles=q_cols // W_TILE,
                          q_scale=q_scale),
        grid=(n_out // W_TILE,),
        in_specs=[pl.BlockSpec((W_TILE, d), lambda j: (j, 0)),
                  pl.BlockSpec((n_dt, d), lambda j: ((j + 1) * (W_TILE // n_dt), 0)),
                  pl.BlockSpec((DT_PAD, d), lambda j: (col_dt // DT_PAD, 0))],
        out_specs=[pl.BlockSpec((d, W_TILE), lambda j: (0, j)),
                   pl.BlockSpec((d, DT_PAD), lambda j: (0, 0))],
        out_shape=[jax.ShapeDtypeStruct((d, n_out), BF16), jax.ShapeDtypeStruct((d, DT_PAD), BF16)],
        name="w_in_prep",
    )(w_in_t, w_in_t, w_in_t)


def _proj_kernel(x_ref, shift_ref, scale_ref, g_ref, w_ref, wdt_ref, p_ref, dtt_ref, *, chunks):
    x = x_ref[...]
    ms = jnp.mean(x * x, axis=-1, keepdims=True)
    h = x * lax.rsqrt(ms + EPS) * g_ref[...]
    h = h * (1.0 + scale_ref[...]) + shift_ref[...]
    hb = h.astype(BF16)
    for src, dst, width in chunks:
        p_ref[:, dst:dst + width] = _dot(hb, w_ref[:, src:src + width]).astype(BF16)
    dt = _dot(hb, wdt_ref[...])
    dtt_ref[...] = dt.T[:N_DIRS * SSM_HEADS]


def _projection(x3, shift, scale, g_pre, w_main, w_dt, col_ranges, tm):
    b, l, d = x3.shape
    tm = min(tm, l)
    chunks, dst = [], 0
    cw = PROJ_CHUNK
    for lo, hi in col_ranges:
        for src in range(lo, hi, cw):
            chunks.append((src, dst, cw))
            dst += cw
    n_out = dst
    n_dt = N_DIRS * SSM_HEADS
    return pl.pallas_call(
        functools.partial(_proj_kernel, chunks=tuple(chunks)),
        grid=(b, l // tm),
        in_specs=[pl.BlockSpec((None, tm, d), lambda i, j: (i, j, 0)),
                  pl.BlockSpec((None, 1, d), lambda i, j: (i, 0, 0)),
                  pl.BlockSpec((None, 1, d), lambda i, j: (i, 0, 0)),
                  pl.BlockSpec((1, d), lambda i, j: (0, 0)),
                  pl.BlockSpec(w_main.shape, lambda i, j: (0, 0), pipeline_mode=pl.Buffered(1)),
                  pl.BlockSpec(w_dt.shape, lambda i, j: (0, 0), pipeline_mode=pl.Buffered(1))],
        out_specs=[pl.BlockSpec((None, tm, n_out), lambda i, j: (i, j, 0)),
                   pl.BlockSpec((None, n_dt, tm), lambda i, j: (i, 0, j))],
        out_shape=[jax.ShapeDtypeStruct((b, l, n_out), BF16),
                   jax.ShapeDtypeStruct((b, n_dt, l), F32)],
        compiler_params=pltpu.CompilerParams(
            dimension_semantics=("parallel", "parallel"), vmem_limit_bytes=VMEM_LIMIT),
        name="projection",
    )(x3, shift, scale, g_pre, w_main, w_dt)


def _bias_block_plan():
    rows = 2048 // GRID_W
    plan = np.full((N_BIAS_CASES, Q_ROWS, KEY_ROWS), -1, np.int64)
    for case, blk in enumerate(BIAS_CASE_BLOCK):
        u0 = int(np.clip(Q_ROWS * blk - WIN_ROWS // 2, 0, rows - KEY_ROWS))
        for rho in range(Q_ROWS):
            r = Q_ROWS * blk + rho
            r0 = int(np.clip(r - WIN_ROWS // 2, 0, rows - WIN_ROWS))
            for a in range(KEY_ROWS):
                krow = u0 + a
                if r0 <= krow < r0 + WIN_ROWS:
                    plan[case, rho, a] = krow - r + WIN_ROWS - 1
    return plan


def _bias_kernel(rpb_ref, o_ref, toep_ref, *, plan):
    n_dr = 2 * WIN_ROWS - 1
    qc = lax.broadcasted_iota(jnp.int32, (GRID_W, GRID_W), 0)
    kc = lax.broadcasted_iota(jnp.int32, (GRID_W, GRID_W), 1)
    c0 = jnp.clip(qc - WIN_COLS // 2, 0, GRID_W - WIN_COLS)
    in_win = (kc >= c0) & (kc < c0 + WIN_COLS)
    for dr in range(n_dr):
        rows = jnp.broadcast_to(rpb_ref[dr:dr + 1, :], (GRID_W, BIAS_LANES))
        toep = pltpu.roll(rows, BIAS_LANES - (WIN_COLS - 1), 1, stride=1, stride_axis=0)[:, :GRID_W]
        toep_ref[dr] = jnp.where(in_win, toep * LOG2E, NEG)
    neg = jnp.full((GRID_W, GRID_W), NEG, F32)
    for case in range(N_BIAS_CASES):
        for rho in range(Q_ROWS):
            for a in range(KEY_ROWS):
                dr = int(plan[case, rho, a])
                val = neg if dr < 0 else toep_ref[dr]
                o_ref[case, rho * GRID_W:(rho + 1) * GRID_W, a * GRID_W:(a + 1) * GRID_W] = val.astype(BF16)


def _bias_table(rpb):
    plan = _bias_block_plan()
    heads, n_dr, n_dc = rpb.shape
    rpb_lanes = jnp.pad(rpb, ((0, 0), (0, 0), (0, BIAS_LANES - n_dc)))
    return pl.pallas_call(
        functools.partial(_bias_kernel, plan=plan),
        grid=(heads,),
        in_specs=[pl.BlockSpec((None, n_dr, BIAS_LANES), lambda h: (h, 0, 0))],
        out_specs=pl.BlockSpec((N_BIAS_CASES, None, Q_BLK, KEY_BLK), lambda h: (0, h, 0, 0)),
        out_shape=jax.ShapeDtypeStruct((N_BIAS_CASES, heads, Q_BLK, KEY_BLK), BF16),
        scratch_shapes=[pltpu.VMEM((n_dr, GRID_W, GRID_W), F32)],
        name="bias_table",
    )(rpb_lanes)


def _na_kernel(q_ref, k0_ref, k1_ref, k2_ref, kc_ref, v0_ref, v1_ref, v2_ref, vc_ref, z_ref, bias_ref, o_ref):
    lane = lax.broadcasted_iota(jnp.int32, (1, 2 * NA_HEAD_DIM), 1)
    k_refs = (k0_ref, k1_ref, k2_ref, kc_ref)
    v_refs = (v0_ref, v1_ref, v2_ref, vc_ref)
    ones_tile = jnp.where(lax.broadcasted_iota(jnp.int32, (Q_BLK, 2 * NA_HEAD_DIM), 1) == 0, 1.0, 0.0).astype(BF16)
    for bb, pair in ((bb, pair) for bb in range(q_ref.shape[0]) for pair in range(NA_HEADS // 2)):
        cs = slice(pair * LANES, (pair + 1) * LANES)
        qp = q_ref[bb, :, cs]
        q_stack = jnp.concatenate([jnp.where(lane < NA_HEAD_DIM, qp, jnp.zeros_like(qp)),
                                   jnp.where(lane >= NA_HEAD_DIM, qp, jnp.zeros_like(qp))], axis=0)
        s_both = [_dot_nt(q_stack, r[bb, :, cs]) for r in k_refs]
        e_both = [[], []]
        for hh in range(2):
            s = [sb[hh * Q_BLK:(hh + 1) * Q_BLK] for sb in s_both]
            for t in range(KEY_ROWS // Q_ROWS):
                s[t] = s[t] + bias_ref[2 * pair + hh, :, t * KEY_TILE:(t + 1) * KEY_TILE].astype(F32)
            mx = jnp.max(jnp.maximum(jnp.maximum(s[0], s[1]), jnp.maximum(s[2], s[3])), axis=-1, keepdims=True)
            e_both[hh] = [jnp.exp2((st - mx).astype(BF16)) for st in s]
        o = None
        for t, r in enumerate(v_refs):
            p_stack = jnp.concatenate([e_both[0][t], e_both[1][t]], axis=0)
            c = _dot(p_stack, jnp.concatenate([r[bb, :, cs], ones_tile], axis=1))
            o = c if o is None else o + c
        outs = [o[hh * Q_BLK:(hh + 1) * Q_BLK, :LANES] * (1.0 / o[hh * Q_BLK:(hh + 1) * Q_BLK, LANES:LANES + 1])
                for hh in range(2)]
        acc = jnp.where(lane < NA_HEAD_DIM, outs[0], outs[1])
        z = z_ref[bb, :, cs].astype(F32)
        o_ref[bb, :, cs] = (acc * (z * _sigmoid(z))).astype(BF16)


def _neighborhood_attention(p3, pc3, bias_tab):
    b, s, _ = p3.shape
    assert b % NA_BATCH == 0
    n_blk = s // Q_BLK

    def key_blk(i):
        return jnp.clip(i - 1, 0, n_blk - KEY_ROWS // Q_ROWS)

    def case_of(i):
        return jnp.minimum(i, 2) + jnp.maximum(i - (n_blk - 3), 0)

    cq, ck, cv, cz = COL_Q // NA_WIDTH, COL_K // NA_WIDTH, COL_V // NA_WIDTH, COL_ZNA // NA_WIDTH
    blk = (NA_BATCH, Q_BLK, NA_WIDTH)
    in_specs = [pl.BlockSpec(blk, lambda i, j: (j, i, cq))]
    in_specs += [pl.BlockSpec(blk, functools.partial(lambda i, j, t: (j, key_blk(i) + t, ck), t=t)) for t in range(3)]
    in_specs += [pl.BlockSpec(blk, lambda i, j: (j, 0, CCOL_K // NA_WIDTH))]
    in_specs += [pl.BlockSpec(blk, functools.partial(lambda i, j, t: (j, key_blk(i) + t, cv), t=t)) for t in range(3)]
    in_specs += [pl.BlockSpec(blk, lambda i, j: (j, 0, CCOL_V // NA_WIDTH))]
    in_specs += [pl.BlockSpec(blk, lambda i, j: (j, i, cz))]
    in_specs += [pl.BlockSpec((None, NA_HEADS, Q_BLK, KEY_BLK), lambda i, j: (case_of(i), 0, 0, 0))]
    return pl.pallas_call(
        _na_kernel,
        grid=(n_blk, b // NA_BATCH),
        in_specs=in_specs,
        out_specs=pl.BlockSpec(blk, lambda i, j: (j, i, 0)),
        out_shape=jax.ShapeDtypeStruct((b, s, NA_WIDTH), BF16),
        compiler_params=pltpu.CompilerParams(
            dimension_semantics=("parallel", "parallel"), vmem_limit_bytes=VMEM_LIMIT),
        name="neighborhood_attention",
    )(p3, p3, p3, p3, pc3, p3, p3, p3, pc3, p3, bias_tab)


def _softplus(v):
    return jnp.maximum(v, 0.0) + jnp.log(1.0 + jnp.exp(-jnp.abs(v)))


def _head_stack(v, head_lanes):
    return jnp.concatenate([jnp.where(m, v, jnp.zeros_like(v)) for m in head_lanes], axis=0)


def _ssd_kernel(xs_ref, bm_ref, cm_ref, z_ref, xsc_ref, bmc_ref, dtt_ref, dttc_ref,
                cwx_ref, cwb_ref, cwc_ref, cbx_ref, cbb_ref, cbc_ref,
                alog_ref, dtb_ref, skip_ref, ng_ref, o_ref,
                pad_ref, padc_ref, tap_ref, ux_ref, ubt_ref, uc_ref, uxc_ref, ubtc_ref,
                rows_ref, cols_ref, dec_ref, rowsc_ref, decc_ref, y_ref, st_ref, e_ref):
    q = SSM_CHUNK
    hpg = HEADS_PER_GROUP
    nh = N_DIRS * hpg
    s_len = xs_ref.shape[0]
    c_len = xsc_ref.shape[0]
    n_lat = s_len // q
    n_ctx = c_len // q
    halo = CONV_HALO
    blk = CONV_BLK
    win = blk + 2 * halo

    zeros_halo = jnp.zeros((halo, pad_ref.shape[1]), BF16)
    pad_ref[0:halo, :] = zeros_halo
    pad_ref[s_len + halo:s_len + 2 * halo, :] = zeros_halo
    pad_ref[halo:s_len + halo, 0:GROUP_X] = xs_ref[...]
    pad_ref[halo:s_len + halo, GROUP_X:GROUP_X + SSM_STATE] = bm_ref[...]
    pad_ref[halo:s_len + halo, GROUP_X + SSM_STATE:] = cm_ref[...]
    zeros_halo_c = jnp.zeros((halo, padc_ref.shape[1]), BF16)
    padc_ref[0:halo, :] = zeros_halo_c
    padc_ref[c_len + halo:c_len + 2 * halo, :] = zeros_halo_c
    padc_ref[halo:c_len + halo, 0:GROUP_X] = xsc_ref[...]
    padc_ref[halo:c_len + halo, GROUP_X:] = bmc_ref[...]

    cw = jnp.concatenate([cwx_ref[...], cwb_ref[...], cwc_ref[...]], axis=1)
    cb = jnp.concatenate([cbx_ref[...], cbb_ref[...], cbc_ref[...]], axis=1)
    for k in range(SSM_CONV):
        tap_ref[k] = jnp.broadcast_to(cw[k:k + 1, :], (win, cw.shape[1])).astype(BF16)
    off = lax.broadcasted_iota(jnp.int32, (blk, SSM_CONV * win), 1) - lax.broadcasted_iota(
        jnp.int32, (blk, SSM_CONV * win), 0)
    hit = off == halo - SSM_CONV // 2
    for k in range(1, SSM_CONV):
        hit = hit | (off == k * win + halo - SSM_CONV // 2 + k)
    shift = jnp.where(hit, 1.0, 0.0).astype(BF16)

    def conv_block(src_ref, start, width):
        w = src_ref[pl.ds(start, win), :]
        stack = jnp.concatenate([w * tap_ref[k, :, 0:width] for k in range(SSM_CONV)], axis=0)
        acc = _dot(shift, stack) + cb[:, :width]
        return acc * _sigmoid(acc)

    def conv_chunk(src_ref, start, width):
        return jnp.concatenate([conv_block(src_ref, start + i * blk, width) for i in range(q // blk)], axis=0)

    def conv_lat(c, carry):
        start = pl.multiple_of(c * q, q)
        u = conv_chunk(pad_ref, start, GROUP_X + 2 * SSM_STATE)
        ux_ref[pl.ds(start, q), :] = u[:, :GROUP_X]
        ubt_ref[c] = u[:, GROUP_X:GROUP_X + SSM_STATE].T.astype(BF16)
        uc_ref[pl.ds(start, q), :] = u[:, GROUP_X + SSM_STATE:].astype(BF16)
        return carry

    lax.fori_loop(0, n_lat, conv_lat, 0, unroll=True)
    for c in range(n_ctx):
        u = conv_chunk(padc_ref, c * q, GROUP_X + SSM_STATE)
        uxc_ref[c * q:(c + 1) * q, :] = u[:, :GROUP_X]
        ubtc_ref[c] = u[:, GROUP_X:].T.astype(BF16)

    ki = lax.broadcasted_iota(jnp.int32, (q, q), 0)
    ii = lax.broadcasted_iota(jnp.int32, (q, q), 1)
    tri_f = (ki <= ii).astype(F32)
    tri_b = (ki >= ii).astype(F32)
    lane_x = lax.broadcasted_iota(jnp.int32, (1, GROUP_X), 1)
    head_lanes = [(lane_x >= r * SSM_HEAD_DIM) & (lane_x < (r + 1) * SSM_HEAD_DIM) for r in range(hpg)]
    neg_a2 = -jnp.exp(alog_ref[...]) * LOG2E
    dt_bias = dtb_ref[...]

    def dt_forms(raw, n, rows_out, cols_out, dec_out):
        stack = lambda v: jnp.concatenate([v[:, c * q:(c + 1) * q] for c in range(n)], axis=0)
        dt_all = _softplus(raw + dt_bias)
        dt = stack(dt_all)
        a = stack(neg_a2 * dt_all)
        is_fwd = lax.broadcasted_iota(jnp.int32, (n * nh, 1), 0) % nh < hpg
        cum_f = jnp.dot(a, tri_f, preferred_element_type=F32, precision=lax.Precision.HIGHEST)
        cum_b = jnp.dot(a, tri_b, preferred_element_type=F32, precision=lax.Precision.HIGHEST)
        acum = jnp.where(is_fwd, cum_f, cum_b)
        atot = jnp.where(is_fwd, acum[:, q - 1:q], acum[:, 0:1])
        sw = dt * jnp.exp2(atot - acum)
        dec = jnp.exp2(atot)
        for c in range(n):
            sl = slice(c * nh, (c + 1) * nh)
            rows_out[c] = jnp.concatenate([dt[sl], acum[sl], sw[sl]], axis=0)
            if cols_out is not None:
                cols_out[c] = acum[sl].T
            dec_rows = []
            for d in range(N_DIRS):
                base = c * nh + d * hpg
                row = jnp.broadcast_to(dec[base + hpg - 1:base + hpg, :], (1, GROUP_X))
                for r in range(hpg - 2, -1, -1):
                    row = jnp.where(lane_x < (r + 1) * SSM_HEAD_DIM, dec[base + r:base + r + 1, :], row)
                dec_rows.append(row)
            dec_out[c] = jnp.concatenate(dec_rows, axis=0)

    dt_forms(dtt_ref[...], n_lat, rows_ref, cols_ref, dec_ref)
    dt_forms(dttc_ref[...], n_ctx, rowsc_ref, None, decc_ref)

    def state_term(d, rows_b, bt, xm):
        parts = [bt * rows_b[2 * N_DIRS * hpg + d * hpg + r:2 * N_DIRS * hpg + d * hpg + r + 1, :]
                 for r in range(hpg)]
        return _dot(jnp.concatenate(parts, axis=1), xm)

    h_init = []
    for d in range(N_DIRS):
        h = jnp.zeros((SSM_STATE, GROUP_X), F32)
        for c in (range(n_ctx) if d == 0 else reversed(range(n_ctx))):
            xm = _head_stack(uxc_ref[c * q:(c + 1) * q, :].astype(BF16), head_lanes)
            h = decc_ref[c][d:d + 1, :] * h + state_term(d, rowsc_ref[c].astype(BF16), ubtc_ref[c], xm)
        h_init.append(h)

    skip = skip_ref[...]
    norm_g = ng_ref[...]
    keep = (ki >= ii, ki <= ii)
    low_lanes = lax.broadcasted_iota(jnp.int32, (1, q), 1) < SSM_HEAD_DIM

    def chunk_body(c, carry):
        start = pl.multiple_of(c * q, q)
        rows = rows_ref[c]
        cols = cols_ref[c]
        xs = ux_ref[pl.ds(start, q), :]
        xm = _head_stack(xs.astype(BF16), head_lanes)
        cmat = uc_ref[pl.ds(start, q), :]
        bt = ubt_ref[c]
        cbm = _dot(cmat, bt)
        cbm_b = cbm.astype(BF16)
        rows_b = rows.astype(BF16)
        zero_b = jnp.zeros((q, q), BF16)
        m_parts = []
        acols = [[], []]
        for r in range(hpg):
            t = None
            for d in range(N_DIRS):
                k = d * hpg + r
                acol = jnp.broadcast_to(cols[:, k:k + 1], (q, q))
                arow = rows[N_DIRS * hpg + k:N_DIRS * hpg + k + 1, :]
                term = jnp.where(keep[d], jnp.exp2((acol - arow).astype(BF16)) * rows_b[k:k + 1, :], zero_b)
                t = term if t is None else t + term
                acols[d].append(acol)
            m_parts.append(cbm_b * t)
        y_ref[pl.ds(start, q), :] = xs * skip + _dot(jnp.concatenate(m_parts, axis=1), xm)
        for d in range(N_DIRS):
            st_ref[d, c] = state_term(d, rows_b, bt, xm)
            halves = [jnp.where(low_lanes, acols[d][2 * i], acols[d][2 * i + 1]) for i in range(hpg // 2)]
            e_ref[d, c] = jnp.exp2(jnp.concatenate(halves, axis=1))
        return carry

    lax.fori_loop(0, n_lat, chunk_body, 0, unroll=SSD_UNROLL)

    def scan_body(j, carry, finalize):
        new = []
        chunk_ids = (j, n_lat - 1 - j)
        for d in range(N_DIRS):
            h = carry[d]
            c = chunk_ids[d]
            start = pl.multiple_of(c * q, q)
            y_off = _dot(uc_ref[pl.ds(start, q), :], h.astype(BF16)) * e_ref[d, c]
            y_ref[pl.ds(start, q), :] = y_ref[pl.ds(start, q), :] + y_off
            new.append(dec_ref[c][d:d + 1, :] * h + st_ref[d, c])
        if finalize:
            for c in chunk_ids:
                start = pl.multiple_of(c * q, q)
                z = z_ref[pl.ds(start, q), :].astype(F32)
                u = y_ref[pl.ds(start, q), :] * (z * _sigmoid(z))
                u = u * lax.rsqrt(jnp.mean(u * u, axis=-1, keepdims=True) + EPS)
                o_ref[pl.ds(start, q), :] = (u * norm_g).astype(BF16)
        return tuple(new)

    half = n_lat // 2
    carry = lax.fori_loop(0, half, functools.partial(scan_body, finalize=False), tuple(h_init), unroll=SSD_UNROLL)
    lax.fori_loop(half, n_lat, functools.partial(scan_body, finalize=True), carry, unroll=SSD_UNROLL)


def _ssd_mixer(p3, pc3, dtt, dttc, conv_w, conv_b, alog8, dtb8, skip_exp, norm_g):
    b, s, _ = p3.shape
    c_len = pc3.shape[1]
    g = SSM_GROUPS
    n_lat, n_ctx = s // SSM_CHUNK, c_len // SSM_CHUNK
    assert n_lat % 2 == 0
    nx = GROUP_X
    ns = SSM_STATE
    nh = N_DIRS * HEADS_PER_GROUP
    in_specs = [
        pl.BlockSpec((None, s, nx), lambda i, j: (i, 0, COL_X // nx + j)),
        pl.BlockSpec((None, s, ns), lambda i, j: (i, 0, COL_B // ns + j)),
        pl.BlockSpec((None, s, ns), lambda i, j: (i, 0, COL_C // ns + j)),
        pl.BlockSpec((None, s, nx), lambda i, j: (i, 0, COL_ZSSM // nx + j)),
        pl.BlockSpec((None, c_len, nx), lambda i, j: (i, 0, CCOL_X // nx + j)),
        pl.BlockSpec((None, c_len, ns), lambda i, j: (i, 0, CCOL_B // ns + j)),
        pl.BlockSpec((None, nh, s), lambda i, j: (i, j, 0)),
        pl.BlockSpec((None, nh, c_len), lambda i, j: (i, j, 0)),
        pl.BlockSpec((SSM_CONV, nx), lambda i, j: (0, j)),
        pl.BlockSpec((SSM_CONV, ns), lambda i, j: (0, SSM_WIDTH // ns + j)),
        pl.BlockSpec((SSM_CONV, ns), lambda i, j: (0, SSM_WIDTH // ns + g + j)),
        pl.BlockSpec((1, nx), lambda i, j: (0, j)),
        pl.BlockSpec((1, ns), lambda i, j: (0, SSM_WIDTH // ns + j)),
        pl.BlockSpec((1, ns), lambda i, j: (0, SSM_WIDTH // ns + g + j)),
        pl.BlockSpec((None, nh, 1), lambda i, j: (j, 0, 0)),
        pl.BlockSpec((None, nh, 1), lambda i, j: (j, 0, 0)),
        pl.BlockSpec((None, 1, nx), lambda i, j: (j, 0, 0)),
        pl.BlockSpec((1, nx), lambda i, j: (0, j)),
    ]
    halo = CONV_HALO
    scratch = [
        pltpu.VMEM((s + 2 * halo, nx + 2 * ns), BF16),
        pltpu.VMEM((c_len + 2 * halo, nx + ns), BF16),
        pltpu.VMEM((SSM_CONV, CONV_BLK + 2 * halo, nx + 2 * ns), BF16),
        pltpu.VMEM((s, nx), F32),
        pltpu.VMEM((n_lat, ns, SSM_CHUNK), BF16),
        pltpu.VMEM((s, ns), BF16),
        pltpu.VMEM((c_len, nx), F32),
        pltpu.VMEM((n_ctx, ns, SSM_CHUNK), BF16),
        pltpu.VMEM((n_lat, 3 * nh, SSM_CHUNK), F32),
        pltpu.VMEM((n_lat, SSM_CHUNK, nh), F32),
        pltpu.VMEM((n_lat, N_DIRS, nx), F32),
        pltpu.VMEM((n_ctx, 3 * nh, SSM_CHUNK), F32),
        pltpu.VMEM((n_ctx, N_DIRS, nx), F32),
        pltpu.VMEM((s, nx), F32),
        pltpu.VMEM((N_DIRS, n_lat, ns, nx), F32),
        pltpu.VMEM((N_DIRS, n_lat, SSM_CHUNK, nx), F32),
    ]
    return pl.pallas_call(
        _ssd_kernel,
        grid=(b, g),
        in_specs=in_specs,
        out_specs=pl.BlockSpec((None, s, nx), lambda i, j: (i, 0, j)),
        out_shape=jax.ShapeDtypeStruct((b, s, SSM_WIDTH), BF16),
        scratch_shapes=scratch,
        compiler_params=pltpu.CompilerParams(
            dimension_semantics=("parallel", "parallel"), vmem_limit_bytes=VMEM_LIMIT),
        name="ssd_mixer",
    )(p3, p3, p3, p3, pc3, pc3, dtt, dttc, conv_w, conv_w, conv_w, conv_b, conv_b, conv_b,
      alog8, dtb8, skip_exp, norm_g)


def _merge_kernel(yna_ref, yssm_ref, gna_ref, gssm_ref, x_ref, gate_ref, wna_ref, wssm_ref, wout_ref, gpost_ref, o_ref):
    a = _dot(yna_ref[...], wna_ref[...].astype(BF16))
    s = _dot(yssm_ref[...], wssm_ref[...].astype(BF16))
    m = _sigmoid(gna_ref[...].astype(F32)) * a + _sigmoid(gssm_ref[...].astype(F32)) * s
    o = _dot(m.astype(BF16), wout_ref[...].astype(BF16))
    r = o * lax.rsqrt(jnp.mean(o * o, axis=-1, keepdims=True) + EPS) * gpost_ref[...]
    o_ref[...] = x_ref[...] + gate_ref[...] * r


def _merge(y_na, y_ssm, p3, x3, gate, w_na, w_ssm, w_out, g_post, tm):
    b, s, d = x3.shape
    const = lambda i, j: (0, 0)
    return pl.pallas_call(
        _merge_kernel,
        grid=(b, s // tm),
        in_specs=[pl.BlockSpec((None, tm, y_na.shape[-1]), lambda i, j: (i, j, 0)),
                  pl.BlockSpec((None, tm, y_ssm.shape[-1]), lambda i, j: (i, j, 0)),
                  pl.BlockSpec((None, tm, d), lambda i, j: (i, j, COL_GNA // d)),
                  pl.BlockSpec((None, tm, d), lambda i, j: (i, j, COL_GSSM // d)),
                  pl.BlockSpec((None, tm, d), lambda i, j: (i, j, 0)),
                  pl.BlockSpec((None, 1, d), lambda i, j: (i, 0, 0)),
                  pl.BlockSpec(w_na.shape, const, pipeline_mode=pl.Buffered(1)),
                  pl.BlockSpec(w_ssm.shape, const, pipeline_mode=pl.Buffered(1)),
                  pl.BlockSpec(w_out.shape, const, pipeline_mode=pl.Buffered(1)),
                  pl.BlockSpec((1, d), const)],
        out_specs=pl.BlockSpec((None, tm, d), lambda i, j: (i, j, 0)),
        out_shape=jax.ShapeDtypeStruct((b, s, d), x3.dtype),
        compiler_params=pltpu.CompilerParams(
            dimension_semantics=("parallel", "parallel"), vmem_limit_bytes=VMEM_LIMIT),
        name="merge_out",
    )(y_na, y_ssm, p3, p3, x3, gate, w_na, w_ssm, w_out, g_post)


def kernel(x, c, ctx, c_ctx, w_mod, b_mod, g_pre, g_post, w_in, conv_w, conv_b, a_log, dt_bias, d_skip,
           ssm_norm_g, rpb, w_na_out, w_ssm_out, w_out):
    assert w_mod.shape[0] == 1, "single-layer block"
    b, s, d = x.shape
    g, hpg = SSM_GROUPS, HEADS_PER_GROUP

    w_main, w_dt = _prep_w_in(jnp.transpose(w_in[0]), COL_GNA, N_DIRS * SSM_HEADS, COL_K, NA_HEAD_DIM ** -0.5 * LOG2E)
    per_group = lambda p: jnp.transpose(p.reshape(N_DIRS, g, hpg), (1, 0, 2)).reshape(g, N_DIRS * hpg, 1)
    alog8 = per_group(a_log[0])
    dtb8 = per_group(dt_bias[0])
    skip_exp = jnp.repeat(d_skip[0], SSM_HEAD_DIM).reshape(g, 1, GROUP_X)

    cond = jnp.zeros((16, d), F32).at[:b].set(c).at[b].set(c_ctx)
    mod = _adaln(cond, w_mod[0], b_mod[0])
    shift_l, scale_l, gate_l = (mod[:b, k * d:(k + 1) * d].reshape(b, 1, d) for k in range(3))
    shift_c, scale_c = (jnp.broadcast_to(mod[b, k * d:(k + 1) * d].reshape(1, 1, d), (b, 1, d)) for k in range(2))

    p3, dtt = _projection(x, shift_l, scale_l, g_pre, w_main, w_dt, ((0, NP_COLS),), tm=PROJ_TM)
    pc3, dttc = _projection(ctx, shift_c, scale_c, g_pre, w_main, w_dt,
                            ((COL_K, COL_ZNA), (COL_X, COL_GNA)), tm=PROJ_TM)

    bias_tab = _bias_table(rpb[0])
    y_na = _neighborhood_attention(p3, pc3, bias_tab)

    y_ssm = _ssd_mixer(p3, pc3, dtt, dttc, conv_w[0], conv_b[0].reshape(1, -1), alog8, dtb8, skip_exp,
                       ssm_norm_g)

    return _merge(y_na, y_ssm, p3, x, gate_l, w_na_out[0], w_ssm_out[0], w_out[0], g_post, tm=MERGE_TM)
```
